```python
import math
import numpy as np
import jax
import jax.numpy as jnp
from jax import lax

D_MODEL = 2048
BATCH = 2
SEQ = 8192
DEPTH = 1

HEAD_DIM = 128
ROT_DIM = HEAD_DIM // 4
ROPE_THETA = 500000.0
NORM_EPS = 1e-6
NEG_INF = -1e30

NSA_HEADS = 8
NSA_KV_HEADS = 2
NSA_GROUP = NSA_HEADS // NSA_KV_HEADS
N_BRANCH = 3
CMP_LEN = 32
CMP_STRIDE = 16
CMP_HIDDEN = 2 * HEAD_DIM
SEL_BLOCK = 64
N_SELECTED = 16
WINDOW = 512
FORCED_SCORE = 1e9
NSA_Q_BLOCK = 64

DIFF_HEADS = 4
DIFF_V_DIM = 2 * HEAD_DIM
DIFF_Q_BLOCK = 128

MIX_WIDTH = NSA_HEADS * HEAD_DIM + DIFF_HEADS * DIFF_V_DIM

NSA_Q_COLS = NSA_HEADS * HEAD_DIM
NSA_KV_COLS = NSA_KV_HEADS * HEAD_DIM
NSA_GATE_COLS = NSA_HEADS * N_BRANCH
DIFF_QK_COLS = DIFF_HEADS * 2 * HEAD_DIM
DIFF_V_COLS = DIFF_HEADS * DIFF_V_DIM
IN_SPLITS = (NSA_Q_COLS,
             NSA_KV_COLS, NSA_KV_COLS,
             NSA_KV_COLS, NSA_KV_COLS,
             NSA_KV_COLS, NSA_KV_COLS,
             NSA_GATE_COLS,
             DIFF_QK_COLS, DIFF_QK_COLS, DIFF_V_COLS)
IN_COLS = sum(IN_SPLITS)
IN_SPLIT_IDX = [int(v) for v in np.cumsum(IN_SPLITS)[:-1]]

N_EXPERTS = 32
TOP_K = 4
D_FF = D_MODEL
SWIGLU_ALPHA = 1.702
SWIGLU_LIMIT = 7.0
MOE_BLOCK = 256

kernel_name = "hybrid_nsa_diffattn_moe_block"


def rms_norm(x, g):
    xf = x.astype(jnp.float32)
    y = xf * lax.rsqrt(jnp.mean(xf * xf, axis=-1, keepdims=True) + NORM_EPS)
    return (y * g.astype(jnp.float32)).astype(x.dtype)


def partial_rope(x, pos):
    inv = jnp.power(ROPE_THETA, -jnp.arange(0, ROT_DIM, 2, dtype=jnp.float32) / ROT_DIM)
    ang = pos.astype(jnp.float32)[:, None] * inv[None, :]
    cos, sin = jnp.cos(ang), jnp.sin(ang)
    xr = x[..., :ROT_DIM].astype(jnp.float32)
    x1, x2 = xr[..., : ROT_DIM // 2], xr[..., ROT_DIM // 2:]
    rot = jnp.concatenate([x1 * cos - x2 * sin, x2 * cos + x1 * sin], axis=-1).astype(x.dtype)
    return jnp.concatenate([rot, x[..., ROT_DIM:]], axis=-1)


def masked_softmax(s, mask):
    s = jnp.where(mask, s, NEG_INF)
    m = jnp.max(s, axis=-1, keepdims=True)
    p = jnp.exp(s - m) * mask
    return p / jnp.maximum(jnp.sum(p, axis=-1, keepdims=True), 1e-30)


def nsa_mixer(q, kc, vc, ks, vs, kw, vw, g, q_norm_g, k_norm_g, pos_k, pos_v,
              k_w1, k_w2, v_w1, v_w2, pos):
    B, S, _ = q.shape
    dt = q.dtype
    scale = HEAD_DIM ** -0.5

    def heads(t, n):
        return t.reshape(B, S, n, HEAD_DIM).transpose(0, 2, 1, 3)

    qn = partial_rope(rms_norm(heads(q, NSA_HEADS), q_norm_g), pos)
    qn = qn.reshape(B, NSA_KV_HEADS, NSA_GROUP, S, HEAD_DIM)
    k_slc = partial_rope(rms_norm(heads(ks, NSA_KV_HEADS), k_norm_g[1]), pos)
    v_slc = heads(vs, NSA_KV_HEADS)
    k_win = partial_rope(rms_norm(heads(kw, NSA_KV_HEADS), k_norm_g[2]), pos)
    v_win = heads(vw, NSA_KV_HEADS)

    n_cmp = (S - CMP_LEN) // CMP_STRIDE + 1
    idx = np.arange(n_cmp)[:, None] * CMP_STRIDE + np.arange(CMP_LEN)[None, :]
    cmp_end = jnp.asarray(idx[:, -1], jnp.int32)

    def compress(t, pe, w1, w2):
        blocks = heads(t, NSA_KV_HEADS)[:, :, idx] + pe
        flat = blocks.reshape(B, NSA_KV_HEADS, n_cmp, CMP_LEN * HEAD_DIM)
        return jax.nn.gelu(flat @ w1) @ w2

    k_cmp = partial_rope(rms_norm(compress(kc, pos_k, k_w1, k_w2), k_norm_g[0]),
                         cmp_end.astype(jnp.float32))
    v_cmp = compress(vc, pos_v, v_w1, v_w2)

    n_sel = S // SEL_BLOCK
    n_top = min(N_SELECTED, n_sel)
    c_start = idx[:, 0]
    s_start = np.arange(n_sel) * SEL_BLOCK
    overlap = (c_start[:, None] < s_start[None, :] + SEL_BLOCK) & (c_start[:, None] + CMP_LEN > s_start[None, :])
    sel_map = jnp.asarray(overlap, jnp.float32)

    k_slc_blocks = k_slc.reshape(B, NSA_KV_HEADS, n_sel, SEL_BLOCK, HEAD_DIM)
    v_slc_blocks = v_slc.reshape(B, NSA_KV_HEADS, n_sel, SEL_BLOCK, HEAD_DIM)
    pad = ((0, 0), (0, 0), (WINDOW, 0), (0, 0))
    k_win_p = jnp.pad(k_win, pad)
    v_win_p = jnp.pad(v_win, pad)
    b_ix = jnp.arange(B)[:, None, None, None]
    h_ix = jnp.arange(NSA_KV_HEADS)[None, :, None, None]
    j_ids = jnp.arange(n_sel)
    f32 = jnp.float32

    def block(qi):
        q0 = qi * NSA_Q_BLOCK
        t = q0 + jnp.arange(NSA_Q_BLOCK)
        qb = lax.dynamic_slice_in_dim(qn, q0, NSA_Q_BLOCK, axis=3)
        s_c = jnp.einsum('bhgqd,bhnd->bhgqn', qb, k_cmp, preferred_element_type=f32) * scale
        p_c = masked_softmax(s_c, cmp_end[None, :] <= t[:, None])
        o_c = jnp.einsum('bhgqn,bhnd->bhgqd', p_c.astype(dt), v_cmp)
        imp = jnp.einsum('bhgqn,nj->bhqj', p_c, sel_map)
        cur = t // SEL_BLOCK
        forced = (j_ids[None, :] == 0) | (j_ids[None, :] == cur[:, None]) | (j_ids[None, :] == cur[:, None] - 1)
        imp = jnp.where(forced, FORCED_SCORE, imp)
        imp = jnp.where(j_ids[None, :] <= cur[:, None], imp, NEG_INF)
        top_s, sel = lax.top_k(imp, n_top)
        valid = top_s > -1.0
        ks_g = k_slc_blocks[b_ix, h_ix, sel]
        vs_g = v_slc_blocks[b_ix, h_ix, sel]
        s_s = jnp.einsum('bhgqd,bhqnkd->bhgqnk', qb, ks_g, preferred_element_type=f32) * scale
        key_pos = sel[..., None] * SEL_BLOCK + jnp.arange(SEL_BLOCK)
        m_s = (key_pos <= t[:, None, None]) & valid[..., None]
        p_s = masked_softmax(s_s.reshape(B, NSA_KV_HEADS, NSA_GROUP, NSA_Q_BLOCK, n_top * SEL_BLOCK),
                             m_s.reshape(B, NSA_KV_HEADS, 1, NSA_Q_BLOCK, n_top * SEL_BLOCK))
        o_s = jnp.einsum('bhgqnk,bhqnkd->bhgqd', p_s.reshape(s_s.shape).astype(dt), vs_g)
        kwb = lax.dynamic_slice_in_dim(k_win_p, q0, NSA_Q_BLOCK + WINDOW, axis=2)
        vwb = lax.dynamic_slice_in_dim(v_win_p, q0, NSA_Q_BLOCK + WINDOW, axis=2)
        kpos = q0 - WINDOW + jnp.arange(NSA_Q_BLOCK + WINDOW)
        m_w = (kpos[None, :] >= 0) & (kpos[None, :] <= t[:, None]) & (t[:, None] - kpos[None, :] < WINDOW)
        s_w = jnp.einsum('bhgqd,bhkd->bhgqk', qb, kwb, preferred_element_type=f32) * scale
        p_w = masked_softmax(s_w, m_w)
        o_w = jnp.einsum('bhgqk,bhkd->bhgqd', p_w.astype(dt), vwb)
        return jnp.stack([o_c, o_s, o_w], axis=-1)

    outs = lax.map(block, jnp.arange(S // NSA_Q_BLOCK))
    outs = outs.transpose(1, 0, 4, 2, 3, 5, 6).reshape(B, S, NSA_KV_HEADS, NSA_GROUP, HEAD_DIM, N_BRANCH)
    gates = jax.nn.sigmoid(g.astype(jnp.float32)).reshape(B, S, NSA_KV_HEADS, NSA_GROUP, 1, N_BRANCH)
    o = jnp.sum(outs.astype(jnp.float32) * gates, axis=-1)
    return o.reshape(B, S, NSA_HEADS * HEAD_DIM).astype(dt)


def diff_mixer(q, k, v, q_norm_g, k_norm_g, lq1, lk1, lq2, lk2, subln_g, lambda_init, pos):
    B, S, _ = q.shape
    scale = HEAD_DIM ** -0.5
    qd = q.reshape(B, S, DIFF_HEADS, 2, HEAD_DIM).transpose(0, 2, 3, 1, 4)
    kd = k.reshape(B, S, DIFF_HEADS, 2, HEAD_DIM).transpose(0, 2, 3, 1, 4)
    qd = partial_rope(rms_norm(qd, q_norm_g), pos)
    kd = partial_rope(rms_norm(kd, k_norm_g), pos)
    vd = v.reshape(B, S, DIFF_HEADS, DIFF_V_DIM).transpose(0, 2, 1, 3)
    lam = (jnp.exp(jnp.sum(lq1.astype(jnp.float32) * lk1.astype(jnp.float32)))
           - jnp.exp(jnp.sum(lq2.astype(jnp.float32) * lk2.astype(jnp.float32))) + lambda_init)
    kpos = jnp.arange(S)

    def block(qi):
        q0 = qi * DIFF_Q_BLOCK
        t = q0 + jnp.arange(DIFF_Q_BLOCK)
        qb = lax.dynamic_slice_in_dim(qd, q0, DIFF_Q_BLOCK, axis=3)
        s = jnp.einsum('bhcqd,bhckd->bhcqk', qb, kd, preferred_element_type=jnp.float32) * scale
        p = masked_softmax(s, kpos[None, :] <= t[:, None])
        a = p[:, :, 0] - lam * p[:, :, 1]
        return jnp.einsum('bhqk,bhkd->bhqd', a.astype(vd.dtype), vd)

    o = lax.map(block, jnp.arange(S // DIFF_Q_BLOCK))
    o = o.transpose(1, 0, 3, 2, 4).reshape(B, S, DIFF_HEADS, DIFF_V_DIM)
    o = rms_norm(o, subln_g) * (1.0 - lambda_init)
    return o.reshape(B, S, DIFF_HEADS * DIFF_V_DIM)


def moe_ffn(h, w_router, b_router, w_gate_up, b_gate_up, w_down, b_down):
    B, S, D = h.shape
    T = B * S
    TK = T * TOP_K
    xt = h.reshape(T, D)
    logits = (xt @ w_router + b_router).astype(jnp.float32)
    top_v, top_e = lax.top_k(logits, TOP_K)
    gate = jax.nn.softmax(top_v, axis=-1)
    flat_e = top_e.reshape(TK)
    flat_tok = (jnp.arange(TK) // TOP_K).astype(jnp.int32)
    flat_g = gate.reshape(TK)
    order = jnp.argsort(flat_e)
    se, st, sg = flat_e[order], flat_tok[order], flat_g[order]
    counts = jnp.bincount(flat_e, length=N_EXPERTS)
    padded = (counts + MOE_BLOCK - 1) // MOE_BLOCK * MOE_BLOCK
    pad_end = jnp.cumsum(padded)
    pad_start = pad_end - padded
    cnt_start = jnp.cumsum(counts) - counts
    dest = pad_start[se] + (jnp.arange(TK) - cnt_start[se])
    n_blocks = -(-TK // MOE_BLOCK) + N_EXPERTS
    n_rows = n_blocks * MOE_BLOCK
    row_tok = jnp.full((n_rows,), T, jnp.int32).at[dest].set(st)
    row_g = jnp.zeros((n_rows,), jnp.float32).at[dest].set(sg)
    blk_e = jnp.minimum(jnp.searchsorted(pad_end, jnp.arange(n_blocks) * MOE_BLOCK, side='right'),
                        N_EXPERTS - 1)
    x_pad = jnp.concatenate([xt, jnp.zeros((1, D), xt.dtype)], axis=0)

    def expert_block(args):
        tok, e = args
        xb = x_pad[tok]
        gu = xb @ w_gate_up[e] + b_gate_up[e]
        gl, up = gu[:, :D_FF], gu[:, D_FF:]
        gl = jnp.minimum(gl, SWIGLU_LIMIT)
        up = jnp.clip(up, -SWIGLU_LIMIT, SWIGLU_LIMIT)
        hb = (up + 1.0) * (gl * jax.nn.sigmoid(SWIGLU_ALPHA * gl))
        return hb @ w_down[e] + b_down[e]

    y = lax.map(expert_block, (row_tok.reshape(n_blocks, MOE_BLOCK), blk_e))
    y = y.reshape(n_rows, D).astype(jnp.float32) * row_g[:, None]
    out = jax.ops.segment_sum(y, row_tok, num_segments=T + 1)[:T]
    return out.reshape(B, S, D).astype(h.dtype)


def setup_inputs(seed: int = 0) -> dict:
    key = jax.random.key(seed)
    ks = jax.random.split(key, 32)
    L = DEPTH

    def nrm(k, shape, fan_in):
        return jax.random.normal(k, shape, jnp.float32) * fan_in ** -0.5

    def gain(k, shape):
        return 1.0 + 0.05 * jax.random.normal(k, shape, jnp.float32)

    def small(k, shape, s):
        return s * jax.random.normal(k, shape, jnp.float32)

    return {
        "x": jax.random.normal(ks[0], (BATCH, SEQ, D_MODEL), jnp.float32),
        "attn_norm_g": gain(ks[1], (L, D_MODEL)),
        "w_in": nrm(ks[2], (L, D_MODEL, IN_COLS), D_MODEL),
        "nsa_q_norm_g": gain(ks[3], (L, HEAD_DIM)),
        "nsa_k_norm_g": gain(ks[4], (L, N_BRANCH, HEAD_DIM)),
        "nsa_cmp_pos_k": small(ks[5], (L, CMP_LEN, HEAD_DIM), 0.1),
        "nsa_cmp_pos_v": small(ks[6], (L, CMP_LEN, HEAD_DIM), 0.1),
        "nsa_cmp_k_w1": nrm(ks[7], (L, CMP_LEN * HEAD_DIM, CMP_HIDDEN), CMP_LEN * HEAD_DIM),
        "nsa_cmp_k_w2": nrm(ks[8], (L, CMP_HIDDEN, HEAD_DIM), CMP_HIDDEN),
        "nsa_cmp_v_w1": nrm(ks[9], (L, CMP_LEN * HEAD_DIM, CMP_HIDDEN), CMP_LEN * HEAD_DIM),
        "nsa_cmp_v_w2": nrm(ks[10], (L, CMP_HIDDEN, HEAD_DIM), CMP_HIDDEN),
        "diff_q_norm_g": gain(ks[11], (L, HEAD_DIM)),
        "diff_k_norm_g": gain(ks[12], (L, HEAD_DIM)),
        "diff_lambda_q1": small(ks[13], (L, HEAD_DIM), 0.1),
        "diff_lambda_k1": small(ks[14], (L, HEAD_DIM), 0.1),
        "diff_lambda_q2": small(ks[15], (L, HEAD_DIM), 0.1),
        "diff_lambda_k2": small(ks[16], (L, HEAD_DIM), 0.1),
        "diff_subln_g": gain(ks[17], (L, DIFF_V_DIM)),
        "w_out": nrm(ks[18], (L, MIX_WIDTH, D_MODEL), MIX_WIDTH),
        "ffn_norm_g": gain(ks[19], (L, D_MODEL)),
        "w_router": nrm(ks[20], (L, D_MODEL, N_EXPERTS), D_MODEL),
        "b_router": small(ks[21], (L, N_EXPERTS), 0.01),
        "w_gate_up": nrm(ks[22], (L, N_EXPERTS, D_MODEL, 2 * D_FF), D_MODEL),
        "b_gate_up": small(ks[23], (L, N_EXPERTS, 2 * D_FF), 0.01),
        "w_down": nrm(ks[24], (L, N_EXPERTS, D_FF, D_MODEL), D_FF),
        "b_down": small(ks[25], (L, N_EXPERTS, D_MODEL), 0.01),
    }


def reference(x, attn_norm_g, w_in, nsa_q_norm_g, nsa_k_norm_g, nsa_cmp_pos_k, nsa_cmp_pos_v,
              nsa_cmp_k_w1, nsa_cmp_k_w2, nsa_cmp_v_w1, nsa_cmp_v_w2,
              diff_q_norm_g, diff_k_norm_g, diff_lambda_q1, diff_lambda_k1,
              diff_lambda_q2, diff_lambda_k2, diff_subln_g, w_out, ffn_norm_g,
              w_router, b_router, w_gate_up, b_gate_up, w_down, b_down):
    S = x.shape[1]
    pos = jnp.arange(S, dtype=jnp.float32)
    h = x
    for l in range(DEPTH):
        lambda_init = 0.8 - 0.6 * math.exp(-0.3 * l)
        xn = rms_norm(h, attn_norm_g[l])
        proj = xn @ w_in[l]
        q_nsa, kc, vc, ks_, vs_, kw, vw, g_nsa, q_d, k_d, v_d = jnp.split(proj, IN_SPLIT_IDX, axis=-1)
        o_nsa = nsa_mixer(q_nsa, kc, vc, ks_, vs_, kw, vw, g_nsa, nsa_q_norm_g[l], nsa_k_norm_g[l],
                          nsa_cmp_pos_k[l], nsa_cmp_pos_v[l], nsa_cmp_k_w1[l], nsa_cmp_k_w2[l],
                          nsa_cmp_v_w1[l], nsa_cmp_v_w2[l], pos)
        o_diff = diff_mixer(q_d, k_d, v_d, diff_q_norm_g[l], diff_k_norm_g[l], diff_lambda_q1[l],
                            diff_lambda_k1[l], diff_lambda_q2[l], diff_lambda_k2[l], diff_subln_g[l],
                            lambda_init, pos)
        mix = jnp.concatenate([o_nsa.astype(h.dtype), o_diff.astype(h.dtype)], axis=-1)
        h = h + (mix @ w_out[l]).astype(h.dtype)
        h = h + moe_ffn(rms_norm(h, ffn_norm_g[l]), w_router[l], b_router[l], w_gate_up[l],
                        b_gate_up[l], w_down[l], b_down[l])
    return h
```

```python
import functools
import math

import numpy as np
import jax
import jax.numpy as jnp
from jax import lax
from jax.experimental import pallas as pl
from jax.experimental.pallas import tpu as pltpu

f32 = jnp.float32
bf16 = jnp.bfloat16
i32 = jnp.int32

D_MODEL = 2048
HEAD_DIM = 128
ROT_DIM = HEAD_DIM // 4
ROPE_THETA = 500000.0
NORM_EPS = 1e-6
NEG_INF = -1e30

NSA_HEADS = 8
NSA_KV_HEADS = 2
NSA_GROUP = NSA_HEADS // NSA_KV_HEADS
N_BRANCH = 3
CMP_LEN = 32
CMP_STRIDE = 16
CMP_HIDDEN = 2 * HEAD_DIM
SEL_BLOCK = 64
SEL_SHIFT = 6
N_SELECTED = 16
WINDOW = 512
FORCED_SCORE = 1e9

DIFF_HEADS = 4
DIFF_V_DIM = 2 * HEAD_DIM

N_EXPERTS = 32
TOP_K = 4
D_FF = D_MODEL
SWIGLU_ALPHA = 1.702
SWIGLU_LIMIT = 7.0

LANES = 128

HD_Q = 0
HD_KS = 8
HD_KW = 10
HD_QD = 12
HD_KD = 20
N_NORM_HEADS = 28
HD_KC = 28
HD_VC = 30
HD_VS = 32
HD_VW = 34
HD_VD = 36
HD_GATE = 44
N_HEADS_PAD = 48
HP = 4

MASK_VAL = -1e30
M_INIT = -5e29

VMEM_LIMIT = 56 * 1024 * 1024


def _cparams(sem):
    return pltpu.CompilerParams(dimension_semantics=sem, vmem_limit_bytes=VMEM_LIMIT)


def _rope_tables(pos):
    inv = np.power(ROPE_THETA, -np.arange(0, ROT_DIM, 2, dtype=np.float64) / ROT_DIM)
    ang = pos.astype(np.float64)[:, None] * inv[None, :]
    cos, sin = np.cos(ang), np.sin(ang)
    n = pos.shape[0]
    half = ROT_DIM // 2
    c = np.concatenate([cos, cos, np.ones((n, HEAD_DIM - ROT_DIM))], axis=1)
    sa = np.concatenate([-sin, np.zeros((n, HEAD_DIM - half))], axis=1)
    sb = np.concatenate([np.zeros((n, half)), sin, np.zeros((n, HEAD_DIM - ROT_DIM))], axis=1)
    return (jnp.asarray(c, f32), jnp.asarray(sa, f32), jnp.asarray(sb, f32))


def _rope(y, c, sa, sb):
    half = ROT_DIM // 2
    return (y * c + pltpu.roll(y, HEAD_DIM - half, 1) * sa + pltpu.roll(y, half, 1) * sb)


def _rms(y, gain):
    ms = jnp.mean(y * y, axis=-1, keepdims=True)
    return y * lax.rsqrt(ms + NORM_EPS) * gain


def _inproj_kernel(x_ref, g_ref, w_ref, gain_ref, c_ref, sa_ref, sb_ref,
                   o_ref, gate_ref, xn_ref):
    j = pl.program_id(1)

    @pl.when(j == 0)
    def _():
        x = x_ref[...]
        xn_ref[...] = _rms(x, g_ref[...]).astype(bf16)

    acc = jnp.dot(xn_ref[...], w_ref[...], preferred_element_type=f32)

    @pl.when(j < N_NORM_HEADS // HP)
    def _():
        c, sa, sb = c_ref[...], sa_ref[...], sb_ref[...]
        for u in range(HP):
            y = _rms(acc[:, u * HEAD_DIM:(u + 1) * HEAD_DIM], gain_ref[u])
            o_ref[u] = _rope(y, c, sa, sb).astype(bf16)

    @pl.when(j >= N_NORM_HEADS // HP)
    def _():
        for u in range(HP):
            o_ref[u] = acc[:, u * HEAD_DIM:(u + 1) * HEAD_DIM].astype(bf16)

    @pl.when(j == HD_GATE // HP)
    def _():
        gate_ref[0] = acc[:, 0:HEAD_DIM]
        gate_ref[1] = acc[:, HEAD_DIM:2 * HEAD_DIM]


def _inproj(x2, attn_g, w_r, gains, rope_c, rope_sa, rope_sb, S, tm):
    T = x2.shape[0]
    nsb = S // tm
    n_norm_steps = N_NORM_HEADS // HP
    return pl.pallas_call(
        _inproj_kernel,
        grid=(T // tm, N_HEADS_PAD // HP),
        in_specs=[
            pl.BlockSpec((tm, D_MODEL), lambda i, j: (i, 0)),
            pl.BlockSpec((1, D_MODEL), lambda i, j: (0, 0)),
            pl.BlockSpec((D_MODEL, HP * HEAD_DIM), lambda i, j: (0, j)),
            pl.BlockSpec((HP, 1, HEAD_DIM), lambda i, j: (jnp.minimum(j, n_norm_steps - 1), 0, 0)),
            pl.BlockSpec((tm, HEAD_DIM), lambda i, j: (i % nsb, 0)),
            pl.BlockSpec((tm, HEAD_DIM), lambda i, j: (i % nsb, 0)),
            pl.BlockSpec((tm, HEAD_DIM), lambda i, j: (i % nsb, 0)),
        ],
        out_specs=[
            pl.BlockSpec((HP, tm, HEAD_DIM), lambda i, j: (j, i, 0)),
            pl.BlockSpec((2, tm, HEAD_DIM), lambda i, j: (0, i, 0)),
        ],
        out_shape=[
            jax.ShapeDtypeStruct((N_HEADS_PAD, T, HEAD_DIM), bf16),
            jax.ShapeDtypeStruct((2, T, HEAD_DIM), f32),
        ],
        scratch_shapes=[pltpu.VMEM((tm, D_MODEL), bf16)],
        compiler_params=_cparams(("parallel", "arbitrary")),
        name="inproj",
    )(x2, attn_g, w_r, gains, rope_c, rope_sa, rope_sb)


def _gelu_tanh(x):
    return 0.5 * x * (1.0 + jnp.tanh(math.sqrt(2.0 / math.pi) * (x + 0.044715 * (x * x * x))))


def _compress_kernel(c_ref, pe_ref, w1_ref, w2_ref, gain_ref, rc_ref, rsa_ref, rsb_ref,
                     o_ref, *, do_norm):
    half = CMP_STRIDE * HEAD_DIM
    c = c_ref[0]
    nc = c.shape[0]
    w1 = w1_ref[...].astype(bf16)
    a = jnp.dot(c, w1[:half], preferred_element_type=f32)
    b = jnp.dot(c, w1[half:], preferred_element_type=f32)
    peb = jnp.dot(pe_ref[...], w1, preferred_element_type=f32)[0:1]
    hid = a + pltpu.roll(b, nc - 1, 0) + peb
    act = _gelu_tanh(hid)
    out = jnp.dot(act.astype(bf16), w2_ref[...].astype(bf16), preferred_element_type=f32)
    if do_norm:
        out = _rope(_rms(out, gain_ref[...]), rc_ref[...], rsa_ref[...], rsb_ref[...])
    o_ref[0] = out.astype(bf16)


def _compress(chunks, pe, w1, w2, gain, rc, rsa, rsb, do_norm):
    ng, nc, width = chunks.shape
    pe8 = jnp.broadcast_to(pe.reshape(1, CMP_LEN * HEAD_DIM), (8, CMP_LEN * HEAD_DIM)).astype(bf16)
    full = lambda shape: pl.BlockSpec(shape, lambda g: (0,) * len(shape))
    return pl.pallas_call(
        functools.partial(_compress_kernel, do_norm=do_norm),
        grid=(ng,),
        in_specs=[
            pl.BlockSpec((1, nc, width), lambda g: (g, 0, 0)),
            full((8, CMP_LEN * HEAD_DIM)),
            full((CMP_LEN * HEAD_DIM, CMP_HIDDEN)),
            full((CMP_HIDDEN, HEAD_DIM)),
            full((1, HEAD_DIM)),
            full((nc, HEAD_DIM)), full((nc, HEAD_DIM)), full((nc, HEAD_DIM)),
        ],
        out_specs=pl.BlockSpec((1, nc, HEAD_DIM), lambda g: (g, 0, 0)),
        out_shape=jax.ShapeDtypeStruct((ng, nc, HEAD_DIM), bf16),
        compiler_params=_cparams(("parallel",)),
        name="compress_k" if do_norm else "compress_v",
    )(chunks, pe8, w1, w2, gain, rc, rsa, rsb)


def _sigmoid(x):
    return 1.0 / (1.0 + jnp.exp(-x))


def _nsa_cmp_kernel(q_ref, kc_ref, vc_ref, g_ref, smt_ref, oc_ref, bias_ref, *, tq):
    q0 = pl.program_id(2) * tq
    R = NSA_GROUP * tq
    q = q_ref[...].reshape(R, HEAD_DIM)
    kc = kc_ref[0]
    vc = vc_ref[0]
    nc = kc.shape[0]
    s = lax.dot_general(q, kc, (((1,), (1,)), ((), ())), preferred_element_type=f32)
    t = q0 + (lax.broadcasted_iota(i32, (R, nc), 0) & (tq - 1))
    cend = lax.broadcasted_iota(i32, (R, nc), 1) * CMP_STRIDE + (CMP_LEN - 1)
    mask = cend <= t
    s = jnp.where(mask, s, NEG_INF)
    m = jnp.max(s, axis=-1, keepdims=True)
    p = jnp.where(mask, jnp.exp(s - m), 0.0)
    p = p / jnp.maximum(jnp.sum(p, axis=-1, keepdims=True), 1e-30)
    o = jnp.dot(p.astype(bf16), vc, preferred_element_type=f32)

    psum = p[0:tq] + p[tq:2 * tq] + p[2 * tq:3 * tq] + p[3 * tq:4 * tq]
    imp = lax.dot_general(smt_ref[...], psum, (((1,), (1,)), ((), ())),
                          precision=lax.Precision.HIGHEST, preferred_element_type=f32)
    nselp = imp.shape[0]
    jf = lax.broadcasted_iota(i32, (nselp, tq), 0).astype(f32)
    cur = ((q0 + lax.broadcasted_iota(i32, (nselp, tq), 1)) >> SEL_SHIFT).astype(f32)
    forced = (jf == 0.0) | (jf == cur) | (jf == cur - 1.0)
    imp = jnp.where(forced, FORCED_SCORE, imp)
    imp = jnp.where(jf <= cur, imp, NEG_INF)
    sel = jnp.zeros((nselp, tq), f32)
    work = imp
    for _ in range(N_SELECTED):
        mx = jnp.max(work, axis=0, keepdims=True)
        first = jnp.min(jnp.where(work == mx, jf, float(nselp)), axis=0, keepdims=True)
        pick = jf == first
        sel = jnp.where(pick & (mx > -1.0), 1.0, sel)
        work = jnp.where(pick, -3e38, work)
    bias_t = jnp.where(sel > 0.5, 0.0, MASK_VAL)
    bias_ref[0] = bias_t.T.astype(bf16)

    sig = _sigmoid(g_ref[0])
    for gi in range(NSA_GROUP):
        col = gi * N_BRANCH
        oc_ref[:, gi * HEAD_DIM:(gi + 1) * HEAD_DIM] = o[gi * tq:(gi + 1) * tq] * sig[:, col:col + 1]


def _nsa_cmp(P, kcmp, vcmp, gate, smt, B, S, tq):
    T = B * S
    nqb = S // tq
    nc = kcmp.shape[1]
    nselp = smt.shape[0]
    return pl.pallas_call(
        functools.partial(_nsa_cmp_kernel, tq=tq),
        grid=(NSA_KV_HEADS, B, nqb),
        in_specs=[
            pl.BlockSpec((NSA_GROUP, tq, HEAD_DIM), lambda h, b, i: (h, b * nqb + i, 0)),
            pl.BlockSpec((1, nc, HEAD_DIM), lambda h, b, i: (h * B + b, 0, 0)),
            pl.BlockSpec((1, nc, HEAD_DIM), lambda h, b, i: (h * B + b, 0, 0)),
            pl.BlockSpec((1, tq, HEAD_DIM), lambda h, b, i: (h, b * nqb + i, 0)),
            pl.BlockSpec((nselp, nc), lambda h, b, i: (0, 0)),
        ],
        out_specs=[
            pl.BlockSpec((tq, NSA_GROUP * HEAD_DIM), lambda h, b, i: (b * nqb + i, h)),
            pl.BlockSpec((1, tq, nselp), lambda h, b, i: (h, b * nqb + i, 0)),
        ],
        out_shape=[
            jax.ShapeDtypeStruct((T, NSA_HEADS * HEAD_DIM), f32),
            jax.ShapeDtypeStruct((NSA_KV_HEADS, T, nselp), bf16),
        ],
        compiler_params=_cparams(("parallel", "parallel", "parallel")),
        name="nsa_cmp",
    )(P, kcmp, vcmp, gate, smt)


def _lane_tile(x, n):
    return x if n == 1 else jnp.concatenate([x] * n, axis=1)


def _softmax_step(s, v_aug, m_ref, acc_ref):
    tk = s.shape[1]
    m_prev = m_ref[...]
    m_new = jnp.maximum(m_prev, jnp.max(s, axis=1, keepdims=True))
    alpha = jnp.exp(m_prev - m_new)
    p = jnp.exp(s - _lane_tile(m_new, tk // LANES))
    acc_ref[...] = _lane_tile(alpha, 2) * acc_ref[...] + jnp.dot(
        p.astype(bf16), v_aug, preferred_element_type=f32)
    m_ref[...] = m_new


def _nsa_sel_kernel(q_ref, bias_ref, ks_ref, vs_ref, kw_ref, vw_ref, g_ref, oc_ref,
                    o_ref, qaug_ref, m_ref, acc_ref, *, tq, tk, tkw):
    q0 = pl.program_id(2) * tq
    R = NSA_GROUP * tq
    nselp = bias_ref.shape[2]
    q = q_ref[...].reshape(R, HEAD_DIM)
    qaug_ref[:, 0:HEAD_DIM] = q
    bias = bias_ref[0]
    for gi in range(NSA_GROUP):
        qaug_ref[gi * tq:(gi + 1) * tq, HEAD_DIM:HEAD_DIM + nselp] = bias
    sig = _sigmoid(g_ref[0])

    def init():
        m_ref[...] = jnp.full(m_ref.shape, M_INIT, f32)
        acc_ref[...] = jnp.zeros(acc_ref.shape, f32)

    def finish():
        return acc_ref[:, 0:HEAD_DIM] / jnp.maximum(acc_ref[:, HEAD_DIM:2 * HEAD_DIM], 1e-30)

    init()
    ones_k = jnp.ones((tk, HEAD_DIM), bf16)

    def sel_step(kt, carry):
        k0 = pl.multiple_of(kt * tk, tk)
        k = ks_ref[0, pl.ds(k0, tk), :]
        blk = (k0 + lax.broadcasted_iota(i32, (tk, nselp), 0)) >> SEL_SHIFT
        onehot = jnp.where(blk == lax.broadcasted_iota(i32, (tk, nselp), 1), 1.0, 0.0).astype(bf16)
        kaug = jnp.concatenate([k, onehot], axis=1)
        s = lax.dot_general(qaug_ref[...], kaug, (((1,), (1,)), ((), ())),
                            preferred_element_type=f32)
        t = q0 + (lax.broadcasted_iota(i32, (R, tk), 0) & (tq - 1))
        kp = k0 + lax.broadcasted_iota(i32, (R, tk), 1)
        s = jnp.where(kp <= t, s, MASK_VAL)
        vaug = jnp.concatenate([vs_ref[0, pl.ds(k0, tk), :], ones_k], axis=1)
        _softmax_step(s, vaug, m_ref, acc_ref)
        return carry

    lax.fori_loop(0, (q0 + tq - 1) // tk + 1, sel_step, 0)
    o_s = finish()

    init()
    ones_w = jnp.ones((tkw, HEAD_DIM), bf16)

    def win_step(kt, carry):
        k0 = pl.multiple_of(kt * tkw, tkw)
        k = kw_ref[0, pl.ds(k0, tkw), :]
        s = lax.dot_general(qaug_ref[:, 0:HEAD_DIM], k, (((1,), (1,)), ((), ())),
                            preferred_element_type=f32)
        t = q0 + (lax.broadcasted_iota(i32, (R, tkw), 0) & (tq - 1))
        kp = k0 + lax.broadcasted_iota(i32, (R, tkw), 1)
        s = jnp.where((kp <= t) & (t - kp < WINDOW), s, MASK_VAL)
        vaug = jnp.concatenate([vw_ref[0, pl.ds(k0, tkw), :], ones_w], axis=1)
        _softmax_step(s, vaug, m_ref, acc_ref)
        return carry

    lo = jnp.maximum(q0 - WINDOW + 1, 0) // tkw
    lax.fori_loop(lo, (q0 + tq - 1) // tkw + 1, win_step, 0)
    o_w = finish()

    for gi in range(NSA_GROUP):
        rows = slice(gi * tq, (gi + 1) * tq)
        cols = slice(gi * HEAD_DIM, (gi + 1) * HEAD_DIM)
        c = gi * N_BRANCH
        o_ref[:, cols] = (oc_ref[:, cols] + o_s[rows] * sig[:, c + 1:c + 2]
                          + o_w[rows] * sig[:, c + 2:c + 3]).astype(bf16)


def _nsa_sel(P, bias, gate, o_c, B, S, tq, tk, tkw):
    T = B * S
    nqb = S // tq
    nselp = bias.shape[2]
    R = NSA_GROUP * tq
    kv_spec = lambda base: pl.BlockSpec((1, S, HEAD_DIM), lambda h, b, i: (base + h, b, 0))
    return pl.pallas_call(
        functools.partial(_nsa_sel_kernel, tq=tq, tk=tk, tkw=tkw),
        grid=(NSA_KV_HEADS, B, nqb),
        in_specs=[
            pl.BlockSpec((NSA_GROUP, tq, HEAD_DIM), lambda h, b, i: (h, b * nqb + i, 0)),
            pl.BlockSpec((1, tq, nselp), lambda h, b, i: (h, b * nqb + i, 0)),
            kv_spec(HD_KS), kv_spec(HD_VS), kv_spec(HD_KW), kv_spec(HD_VW),
            pl.BlockSpec((1, tq, HEAD_DIM), lambda h, b, i: (h, b * nqb + i, 0)),
            pl.BlockSpec((tq, NSA_GROUP * HEAD_DIM), lambda h, b, i: (b * nqb + i, h)),
        ],
        out_specs=pl.BlockSpec((tq, NSA_GROUP * HEAD_DIM), lambda h, b, i: (b * nqb + i, h)),
        out_shape=jax.ShapeDtypeStruct((T, NSA_HEADS * HEAD_DIM), bf16),
        scratch_shapes=[
            pltpu.VMEM((R, HEAD_DIM + nselp), bf16),
            pltpu.VMEM((R, LANES), f32),
            pltpu.VMEM((R, 2 * HEAD_DIM), f32),
        ],
        compiler_params=_cparams(("parallel", "parallel", "arbitrary")),
        name="nsa_sel_win",
    )(P, bias, P, P, P, P, gate, o_c)


def _diff_kernel(q_ref, k_ref, v_ref, lam_ref, sg_ref, o_ref, m_ref, l_ref, acc_ref,
                 *, tq, tk, lambda_init):
    q0 = pl.program_id(2) * tq
    R = 2 * tq
    q1 = q_ref[0]
    q2 = q_ref[1]
    m_ref[...] = jnp.full(m_ref.shape, M_INIT, f32)
    l_ref[...] = jnp.zeros(l_ref.shape, f32)
    acc_ref[...] = jnp.zeros(acc_ref.shape, f32)

    def step(kt, carry):
        k0 = pl.multiple_of(kt * tk, tk)
        nt = (((1,), (1,)), ((), ()))
        s1 = lax.dot_general(q1, k_ref[0, pl.ds(k0, tk), :], nt, preferred_element_type=f32)
        s2 = lax.dot_general(q2, k_ref[1, pl.ds(k0, tk), :], nt, preferred_element_type=f32)
        s = jnp.concatenate([s1, s2], axis=0)
        t = q0 + (lax.broadcasted_iota(i32, (R, tk), 0) & (tq - 1))
        kp = k0 + lax.broadcasted_iota(i32, (R, tk), 1)
        s = jnp.where(kp <= t, s, MASK_VAL)
        v = jnp.concatenate([v_ref[0, pl.ds(k0, tk), :], v_ref[1, pl.ds(k0, tk), :]], axis=1)
        m_prev = m_ref[...]
        m_new = jnp.maximum(m_prev, jnp.max(s, axis=1, keepdims=True))
        alpha = jnp.exp(m_prev - m_new)
        p = jnp.exp(s - _lane_tile(m_new, tk // LANES))
        l_ref[...] = alpha * l_ref[...] + jnp.sum(p, axis=1, keepdims=True)
        acc_ref[...] = _lane_tile(alpha, 2) * acc_ref[...] + jnp.dot(
            p.astype(bf16), v, preferred_element_type=f32)
        m_ref[...] = m_new
        return carry

    lax.fori_loop(0, (q0 + tq - 1) // tk + 1, step, 0)

    lq = lam_ref[...]
    lam = (jnp.exp(jnp.sum(lq[0:1] * lq[1:2], axis=1, keepdims=True))
           - jnp.exp(jnp.sum(lq[2:3] * lq[3:4], axis=1, keepdims=True)) + lambda_init)
    o = acc_ref[...] / _lane_tile(jnp.maximum(l_ref[...], 1e-30), 2)
    a = o[0:tq] - lam * o[tq:R]
    o_ref[...] = (_rms(a, sg_ref[...]) * (1.0 - lambda_init)).astype(bf16)


def _diff_attn(P, lam_vecs, subln_g, B, S, tq, tk, lambda_init):
    T = B * S
    nqb = S // tq
    return pl.pallas_call(
        functools.partial(_diff_kernel, tq=tq, tk=tk, lambda_init=lambda_init),
        grid=(DIFF_HEADS, B, nqb),
        in_specs=[
            pl.BlockSpec((2, tq, HEAD_DIM), lambda h, b, i: (HD_QD // 2 + h, b * nqb + i, 0)),
            pl.BlockSpec((2, S, HEAD_DIM), lambda h, b, i: (HD_KD // 2 + h, b, 0)),
            pl.BlockSpec((2, S, HEAD_DIM), lambda h, b, i: (HD_VD // 2 + h, b, 0)),
            pl.BlockSpec((8, HEAD_DIM), lambda h, b, i: (0, 0)),
            pl.BlockSpec((1, DIFF_V_DIM), lambda h, b, i: (0, 0)),
        ],
        out_specs=pl.BlockSpec((tq, DIFF_V_DIM), lambda h, b, i: (b * nqb + i, h)),
        out_shape=jax.ShapeDtypeStruct((T, DIFF_HEADS * DIFF_V_DIM), bf16),
        scratch_shapes=[
            pltpu.VMEM((2 * tq, LANES), f32),
            pltpu.VMEM((2 * tq, LANES), f32),
            pltpu.VMEM((2 * tq, DIFF_V_DIM), f32),
        ],
        compiler_params=_cparams(("parallel", "parallel", "arbitrary")),
        name="diff_attn",
    )(P, P, P, lam_vecs, subln_g)


def _outproj_kernel(x_ref, on_ref, od_ref, w_ref, g_ref, wr_ref, br_ref,
                    h_ref, hn_ref, lg_ref):
    half = NSA_HEADS * HEAD_DIM
    h = x_ref[...] + jnp.dot(on_ref[...], w_ref[0:half], preferred_element_type=f32) \
        + jnp.dot(od_ref[...], w_ref[half:], preferred_element_type=f32)
    h_ref[...] = h
    hn = _rms(h, g_ref[...])
    hn_ref[...] = hn
    lg_ref[...] = jnp.dot(hn, wr_ref[...], precision=lax.Precision.HIGHEST,
                          preferred_element_type=f32) + br_ref[...]


def _outproj(x2, o_nsa, o_diff, w_out_b, ffn_g, w_router_p, b_router_p, tm):
    T = x2.shape[0]
    full = lambda shape: pl.BlockSpec(shape, lambda i: (0,) * len(shape))
    return pl.pallas_call(
        _outproj_kernel,
        grid=(T // tm,),
        in_specs=[
            pl.BlockSpec((tm, D_MODEL), lambda i: (i, 0)),
            pl.BlockSpec((tm, NSA_HEADS * HEAD_DIM), lambda i: (i, 0)),
            pl.BlockSpec((tm, DIFF_HEADS * DIFF_V_DIM), lambda i: (i, 0)),
            full((D_MODEL, D_MODEL)),
            full((1, D_MODEL)),
            full((D_MODEL, LANES)),
            full((1, LANES)),
        ],
        out_specs=[
            pl.BlockSpec((tm, D_MODEL), lambda i: (i, 0)),
            pl.BlockSpec((tm, D_MODEL), lambda i: (i, 0)),
            pl.BlockSpec((tm, LANES), lambda i: (i, 0)),
        ],
        out_shape=[
            jax.ShapeDtypeStruct((T, D_MODEL), f32),
            jax.ShapeDtypeStruct((T, D_MODEL), f32),
            jax.ShapeDtypeStruct((T, LANES), f32),
        ],
        compiler_params=_cparams(("parallel",)),
        name="outproj_router",
    )(x2, o_nsa, o_diff, w_out_b, ffn_g, w_router_p, b_router_p)


def _gather_kernel(tok_ref, src_ref, dst_ref, sem, *, chunk):
    base = pl.program_id(0) * chunk

    def row_copy(r):
        return pltpu.make_async_copy(src_ref.at[pl.ds(tok_ref[base + r], 1)],
                                     dst_ref.at[pl.ds(base + r, 1)], sem)

    def start(r, c):
        row_copy(r).start()
        return c

    def wait(r, c):
        row_copy(r).wait()
        return c

    lax.fori_loop(0, chunk, start, 0)
    lax.fori_loop(0, chunk, wait, 0)


def _gather_rows(row_tok, src, n_rows, chunk):
    return pl.pallas_call(
        functools.partial(_gather_kernel, chunk=chunk),
        grid_spec=pltpu.PrefetchScalarGridSpec(
            num_scalar_prefetch=1,
            grid=(n_rows // chunk,),
            in_specs=[pl.BlockSpec(memory_space=pl.ANY)],
            out_specs=pl.BlockSpec(memory_space=pl.ANY),
            scratch_shapes=[pltpu.SemaphoreType.DMA(())],
        ),
        out_shape=jax.ShapeDtypeStruct((n_rows, src.shape[1]), src.dtype),
        compiler_params=_cparams(("arbitrary",)),
        name="moe_gather",
    )(row_tok, src)


def _expert_changed(be_ref, b):
    prev = be_ref[jnp.maximum(b - 1, 0)]
    return (b == 0) | (be_ref[b] != prev)


def _gateup_kernel(be_ref, nu_ref, x_ref, wg_ref, wu_ref, bg_ref, bu_ref, o_ref, wgb_ref, wub_ref):
    b = pl.program_id(1)

    @pl.when(b < nu_ref[0])
    def _():
        @pl.when(_expert_changed(be_ref, b))
        def _():
            wgb_ref[...] = wg_ref[0].astype(bf16)
            wub_ref[...] = wu_ref[0].astype(bf16)

        x = x_ref[...].astype(bf16)
        gl = jnp.dot(x, wgb_ref[...], preferred_element_type=f32) + bg_ref[0]
        up = jnp.dot(x, wub_ref[...], preferred_element_type=f32) + bu_ref[0]
        gl = jnp.minimum(gl, SWIGLU_LIMIT)
        up = jnp.clip(up, -SWIGLU_LIMIT, SWIGLU_LIMIT)
        o_ref[...] = ((up + 1.0) * (gl * _sigmoid(SWIGLU_ALPHA * gl))).astype(bf16)

    @pl.when(b >= nu_ref[0])
    def _():
        o_ref[...] = jnp.zeros(o_ref.shape, o_ref.dtype)


def _gateup(blk_e, n_used, xs, w_gate_up, b_gate_up3, tm, tn):
    n_rows = xs.shape[0]
    nb = n_rows // tm
    nj = D_FF // tn
    blk = lambda b, nu: jnp.minimum(b, nu[0] - 1)
    return pl.pallas_call(
        _gateup_kernel,
        grid_spec=pltpu.PrefetchScalarGridSpec(
            num_scalar_prefetch=2,
            grid=(nj, nb),
            in_specs=[
                pl.BlockSpec((tm, D_MODEL), lambda j, b, be, nu: (blk(b, nu), 0)),
                pl.BlockSpec((1, D_MODEL, tn), lambda j, b, be, nu: (be[blk(b, nu)], 0, j)),
                pl.BlockSpec((1, D_MODEL, tn), lambda j, b, be, nu: (be[blk(b, nu)], 0, nj + j)),
                pl.BlockSpec((1, 1, tn), lambda j, b, be, nu: (be[blk(b, nu)], 0, j)),
                pl.BlockSpec((1, 1, tn), lambda j, b, be, nu: (be[blk(b, nu)], 0, nj + j)),
            ],
            out_specs=pl.BlockSpec((tm, tn), lambda j, b, be, nu: (b, j)),
            scratch_shapes=[pltpu.VMEM((D_MODEL, tn), bf16), pltpu.VMEM((D_MODEL, tn), bf16)],
        ),
        out_shape=jax.ShapeDtypeStruct((n_rows, D_FF), bf16),
        compiler_params=_cparams(("arbitrary", "arbitrary")),
        name="moe_gate_up",
    )(blk_e, n_used, xs, w_gate_up, w_gate_up, b_gate_up3, b_gate_up3)


def _down_kernel(be_ref, nu_ref, h_ref, w_ref, b_ref, o_ref, wb_ref):
    b = pl.program_id(1)

    @pl.when(b < nu_ref[0])
    def _():
        @pl.when(_expert_changed(be_ref, b))
        def _():
            wb_ref[...] = w_ref[0].astype(bf16)

        o_ref[...] = jnp.dot(h_ref[...], wb_ref[...], preferred_element_type=f32) + b_ref[0]

    @pl.when(b >= nu_ref[0])
    def _():
        o_ref[...] = jnp.zeros(o_ref.shape, o_ref.dtype)


def _down(blk_e, n_used, hb, w_down, b_down3, tm, tn):
    n_rows = hb.shape[0]
    nb = n_rows // tm
    nj = D_MODEL // tn
    blk = lambda b, nu: jnp.minimum(b, nu[0] - 1)
    return pl.pallas_call(
        _down_kernel,
        grid_spec=pltpu.PrefetchScalarGridSpec(
            num_scalar_prefetch=2,
            grid=(nj, nb),
            in_specs=[
                pl.BlockSpec((tm, D_FF), lambda j, b, be, nu: (blk(b, nu), 0)),
                pl.BlockSpec((1, D_FF, tn), lambda j, b, be, nu: (be[blk(b, nu)], 0, j)),
                pl.BlockSpec((1, 1, tn), lambda j, b, be, nu: (be[blk(b, nu)], 0, j)),
            ],
            out_specs=pl.BlockSpec((tm, tn), lambda j, b, be, nu: (b, j)),
            scratch_shapes=[pltpu.VMEM((D_FF, tn), bf16)],
        ),
        out_shape=jax.ShapeDtypeStruct((n_rows, D_MODEL), f32),
        compiler_params=_cparams(("arbitrary", "arbitrary")),
        name="moe_down",
    )(blk_e, n_used, hb, w_down, b_down3)


def _combine_kernel(pos_ref, h_ref, gt_ref, y_ref, o_ref, buf_ref, sem, *, tc):
    base = pl.program_id(0) * (TOP_K * tc)

    def row_copy(r):
        return pltpu.make_async_copy(y_ref.at[pl.ds(pos_ref[base + r], 1)],
                                     buf_ref.at[pl.ds(r, 1)], sem)

    def start(r, c):
        row_copy(r).start()
        return c

    def wait(r, c):
        row_copy(r).wait()
        return c

    lax.fori_loop(0, TOP_K * tc, start, 0)
    lax.fori_loop(0, TOP_K * tc, wait, 0)
    gt = gt_ref[...]
    out = h_ref[...]
    for k in range(TOP_K):
        out = out + gt[:, k:k + 1] * buf_ref[k * tc:(k + 1) * tc, :]
    o_ref[...] = out


def _combine(pos_blocks, h, gate_pad, y, tc):
    T = h.shape[0]
    return pl.pallas_call(
        functools.partial(_combine_kernel, tc=tc),
        grid_spec=pltpu.PrefetchScalarGridSpec(
            num_scalar_prefetch=1,
            grid=(T // tc,),
            in_specs=[
                pl.BlockSpec((tc, D_MODEL), lambda i, pos: (i, 0)),
                pl.BlockSpec((tc, LANES), lambda i, pos: (i, 0)),
                pl.BlockSpec(memory_space=pl.ANY),
            ],
            out_specs=pl.BlockSpec((tc, D_MODEL), lambda i, pos: (i, 0)),
            scratch_shapes=[pltpu.VMEM((TOP_K * tc, D_MODEL), f32), pltpu.SemaphoreType.DMA(())],
        ),
        out_shape=jax.ShapeDtypeStruct((T, D_MODEL), f32),
        compiler_params=_cparams(("arbitrary",)),
        name="moe_combine",
    )(pos_blocks, h, gate_pad, y)


def _rearranged_w_in(w_in):
    splits = np.cumsum([0, 1024, 256, 256, 256, 256, 256, 256, 24, 1024, 1024, 1024])
    q, kc, vc, ks, vs, kw, vw, g, qd, kd, vd = [w_in[:, splits[n]:splits[n + 1]] for n in range(11)]
    gcols = NSA_GROUP * N_BRANCH
    zpad = lambda n: jnp.zeros((D_MODEL, n), w_in.dtype)
    parts = [q, ks, kw, qd, kd, kc, vc, vs, vw, vd,
             g[:, :gcols], zpad(HEAD_DIM - gcols), g[:, gcols:], zpad(HEAD_DIM - gcols),
             zpad((N_HEADS_PAD - HD_GATE - 2) * HEAD_DIM)]
    return jnp.concatenate(parts, axis=1).astype(bf16)


def _sel_map_t(S, nselp):
    n_chunks = S // CMP_STRIDE
    n_cmp = (S - CMP_LEN) // CMP_STRIDE + 1
    n_sel = S // SEL_BLOCK
    c_start = np.arange(n_chunks) * CMP_STRIDE
    s_start = np.arange(nselp) * SEL_BLOCK
    ov = (c_start[None, :] < s_start[:, None] + SEL_BLOCK) & (c_start[None, :] + CMP_LEN > s_start[:, None])
    ov &= (np.arange(n_chunks)[None, :] < n_cmp) & (np.arange(nselp)[:, None] < n_sel)
    return jnp.asarray(ov, f32)


def _attention_block(x2, B, S, attn_norm_g, w_in, nsa_q_norm_g, nsa_k_norm_g, pos_k, pos_v,
                     k_w1, k_w2, v_w1, v_w2, diff_q_norm_g, diff_k_norm_g, lq1, lk1, lq2, lk2,
                     subln_g, lambda_init):
    T = B * S
    scale = HEAD_DIM ** -0.5
    tm = min(1024, S)
    rope_c, rope_sa, rope_sb = _rope_tables(np.arange(S))
    gains = jnp.concatenate([
        jnp.broadcast_to(nsa_q_norm_g * scale, (NSA_HEADS, HEAD_DIM)),
        jnp.broadcast_to(nsa_k_norm_g[1], (NSA_KV_HEADS, HEAD_DIM)),
        jnp.broadcast_to(nsa_k_norm_g[2], (NSA_KV_HEADS, HEAD_DIM)),
        jnp.broadcast_to(diff_q_norm_g * scale, (2 * DIFF_HEADS, HEAD_DIM)),
        jnp.broadcast_to(diff_k_norm_g, (2 * DIFF_HEADS, HEAD_DIM)),
    ], axis=0).reshape(N_NORM_HEADS, 1, HEAD_DIM)
    P, gate = _inproj(x2, attn_norm_g.reshape(1, D_MODEL), _rearranged_w_in(w_in), gains,
                      rope_c, rope_sa, rope_sb, S, tm)

    n_chunks = S // CMP_STRIDE
    cmp_pos = np.arange(n_chunks) * CMP_STRIDE + (CMP_LEN - 1)
    cc, csa, csb = _rope_tables(cmp_pos)
    chunk_view = lambda hd: P[hd:hd + NSA_KV_HEADS].reshape(NSA_KV_HEADS * B, n_chunks, CMP_STRIDE * HEAD_DIM)
    kcmp = _compress(chunk_view(HD_KC), pos_k, k_w1, k_w2, nsa_k_norm_g[0].reshape(1, HEAD_DIM),
                     cc, csa, csb, True)
    vcmp = _compress(chunk_view(HD_VC), pos_v, v_w1, v_w2, nsa_k_norm_g[0].reshape(1, HEAD_DIM),
                     cc, csa, csb, False)

    nselp = -(-(S // SEL_BLOCK) // LANES) * LANES
    tq = min(256, S)
    o_c, bias = _nsa_cmp(P, kcmp, vcmp, gate, _sel_map_t(S, nselp), B, S, tq)
    o_nsa = _nsa_sel(P, bias, gate, o_c, B, S, tq, min(512, S), min(256, S))

    lam_vecs = jnp.concatenate([lq1[None], lk1[None], lq2[None], lk2[None],
                                jnp.zeros((4, HEAD_DIM), f32)], axis=0)
    o_diff = _diff_attn(P, lam_vecs, subln_g.reshape(1, DIFF_V_DIM), B, S,
                        min(512, S), min(512, S), lambda_init)
    return o_nsa, o_diff


def _moe_block(h, hn, logits, w_gate_up, b_gate_up, w_down, b_down, tm=256):
    T = h.shape[0]
    TK = T * TOP_K
    top_v, top_e = lax.top_k(logits[:, :N_EXPERTS], TOP_K)
    gate = jax.nn.softmax(top_v, axis=-1)
    flat_e = top_e.reshape(TK).astype(i32)
    order = jnp.argsort(flat_e)
    se = flat_e[order]
    st = (order // TOP_K).astype(i32)
    counts = jnp.bincount(flat_e, length=N_EXPERTS)
    padded = (counts + tm - 1) // tm * tm
    pad_end = jnp.cumsum(padded)
    pad_start = pad_end - padded
    cnt_start = jnp.cumsum(counts) - counts
    dest = (pad_start[se] + (jnp.arange(TK) - cnt_start[se])).astype(i32)
    n_blocks = -(-TK // tm) + N_EXPERTS
    n_rows = n_blocks * tm
    row_tok = jnp.zeros((n_rows,), i32).at[dest].set(st)
    pos = jnp.zeros((TK,), i32).at[order].set(dest)
    blk_e = jnp.minimum(jnp.searchsorted(pad_end, jnp.arange(n_blocks) * tm, side='right'),
                        N_EXPERTS - 1).astype(i32)
    n_used = (pad_end[-1] // tm).astype(i32).reshape(1)

    xs = _gather_rows(row_tok, hn, n_rows, 1024)
    hb = _gateup(blk_e, n_used, xs, w_gate_up, b_gate_up.reshape(N_EXPERTS, 1, 2 * D_FF), tm, 512)
    y = _down(blk_e, n_used, hb, w_down, b_down.reshape(N_EXPERTS, 1, D_MODEL), tm, 1024)

    tc = 64
    pos_blocks = pos.reshape(T // tc, tc, TOP_K).transpose(0, 2, 1).reshape(TK)
    gate_pad = jnp.pad(gate, ((0, 0), (0, LANES - TOP_K)))
    return _combine(pos_blocks, h, gate_pad, y, tc)


def kernel(x, attn_norm_g, w_in, nsa_q_norm_g, nsa_k_norm_g, nsa_cmp_pos_k, nsa_cmp_pos_v, nsa_cmp_k_w1, nsa_cmp_k_w2, nsa_cmp_v_w1, nsa_cmp_v_w2, diff_q_norm_g, diff_k_norm_g, diff_lambda_q1, diff_lambda_k1, diff_lambda_q2, diff_lambda_k2, diff_subln_g, w_out, ffn_norm_g, w_router, b_router, w_gate_up, b_gate_up, w_down, b_down):
    B, S, _ = x.shape
    T = B * S
    depth = attn_norm_g.shape[0]
    h = x.reshape(T, D_MODEL)
    for l in range(depth):
        lambda_init = 0.8 - 0.6 * math.exp(-0.3 * l)
        o_nsa, o_diff = _attention_block(
            h, B, S, attn_norm_g[l], w_in[l], nsa_q_norm_g[l], nsa_k_norm_g[l], nsa_cmp_pos_k[l],
            nsa_cmp_pos_v[l], nsa_cmp_k_w1[l], nsa_cmp_k_w2[l], nsa_cmp_v_w1[l], nsa_cmp_v_w2[l],
            diff_q_norm_g[l], diff_k_norm_g[l], diff_lambda_q1[l], diff_lambda_k1[l],
            diff_lambda_q2[l], diff_lambda_k2[l], diff_subln_g[l], lambda_init)
        w_router_p = jnp.pad(w_router[l], ((0, 0), (0, LANES - N_EXPERTS)))
        b_router_p = jnp.pad(b_router[l], (0, LANES - N_EXPERTS)).reshape(1, LANES)
        h_mid, hn, logits = _outproj(h, o_nsa, o_diff, w_out[l].astype(bf16),
                                     ffn_norm_g[l].reshape(1, D_MODEL), w_router_p, b_router_p,
                                     min(512, T))
        h = _moe_block(h_mid, hn, logits, w_gate_up[l], b_gate_up[l], w_down[l], b_down[l])
    return h.reshape(B, S, D_MODEL)
```

```python
import functools
import math

import numpy as np
import jax
import jax.numpy as jnp
from jax import lax
from jax.experimental import pallas as pl
from jax.experimental.pallas import tpu as pltpu

f32 = jnp.float32
bf16 = jnp.bfloat16
i32 = jnp.int32

D_MODEL = 2048
HEAD_DIM = 128
ROT_DIM = HEAD_DIM // 4
ROPE_THETA = 500000.0
NORM_EPS = 1e-6
NEG_INF = -1e30

NSA_HEADS = 8
NSA_KV_HEADS = 2
NSA_GROUP = NSA_HEADS // NSA_KV_HEADS
N_BRANCH = 3
CMP_LEN = 32
CMP_STRIDE = 16
CMP_HIDDEN = 2 * HEAD_DIM
SEL_BLOCK = 64
SEL_SHIFT = 6
N_SELECTED = 16
WINDOW = 512
FORCED_SCORE = 1e9

DIFF_HEADS = 4
DIFF_V_DIM = 2 * HEAD_DIM

N_EXPERTS = 32
TOP_K = 4
D_FF = D_MODEL
SWIGLU_ALPHA = 1.702
SWIGLU_LIMIT = 7.0

LANES = 128

HD_Q = 0
HD_KS = 8
HD_KW = 10
HD_QD = 12
HD_KD = 20
N_NORM_HEADS = 28
HD_KC = 28
HD_VC = 30
HD_VS = 32
HD_VW = 34
HD_VD = 36
HD_GATE = 44
N_HEADS_PAD = 48
HP = 4

MASK_VAL = -1e30
M_INIT = -5e29

VMEM_LIMIT = 56 * 1024 * 1024


def _cparams(sem):
    return pltpu.CompilerParams(dimension_semantics=sem, vmem_limit_bytes=VMEM_LIMIT)


def _rope_tables(pos):
    inv = np.power(ROPE_THETA, -np.arange(0, ROT_DIM, 2, dtype=np.float64) / ROT_DIM)
    ang = pos.astype(np.float64)[:, None] * inv[None, :]
    cos, sin = np.cos(ang), np.sin(ang)
    n = pos.shape[0]
    half = ROT_DIM // 2
    c = np.concatenate([cos, cos, np.ones((n, HEAD_DIM - ROT_DIM))], axis=1)
    sa = np.concatenate([-sin, np.zeros((n, HEAD_DIM - half))], axis=1)
    sb = np.concatenate([np.zeros((n, half)), sin, np.zeros((n, HEAD_DIM - ROT_DIM))], axis=1)
    return (jnp.asarray(c, f32), jnp.asarray(sa, f32), jnp.asarray(sb, f32))


def _rope(y, c, sa, sb):
    half = ROT_DIM // 2
    return (y * c + pltpu.roll(y, HEAD_DIM - half, 1) * sa + pltpu.roll(y, half, 1) * sb)


def _rms(y, gain):
    ms = jnp.mean(y * y, axis=-1, keepdims=True)
    return y * lax.rsqrt(ms + NORM_EPS) * gain


def _inproj_kernel(x_ref, g_ref, w_ref, gain_ref, c_ref, sa_ref, sb_ref,
                   o_ref, gate_ref, xn_ref):
    j = pl.program_id(1)

    @pl.when(j == 0)
    def _():
        x = x_ref[...]
        xn_ref[...] = _rms(x, g_ref[...]).astype(bf16)

    acc = jnp.dot(xn_ref[...], w_ref[...], preferred_element_type=f32)

    @pl.when(j < N_NORM_HEADS // HP)
    def _():
        c, sa, sb = c_ref[...], sa_ref[...], sb_ref[...]
        for u in range(HP):
            y = _rms(acc[:, u * HEAD_DIM:(u + 1) * HEAD_DIM], gain_ref[u])
            o_ref[u] = _rope(y, c, sa, sb).astype(bf16)

    @pl.when(j >= N_NORM_HEADS // HP)
    def _():
        for u in range(HP):
            o_ref[u] = acc[:, u * HEAD_DIM:(u + 1) * HEAD_DIM].astype(bf16)

    @pl.when(j == HD_GATE // HP)
    def _():
        gate_ref[0] = acc[:, 0:HEAD_DIM]
        gate_ref[1] = acc[:, HEAD_DIM:2 * HEAD_DIM]


def _inproj(x2, attn_g, w_r, gains, rope_c, rope_sa, rope_sb, S, tm):
    T = x2.shape[0]
    nsb = S // tm
    n_norm_steps = N_NORM_HEADS // HP
    return pl.pallas_call(
        _inproj_kernel,
        grid=(T // tm, N_HEADS_PAD // HP),
        in_specs=[
            pl.BlockSpec((tm, D_MODEL), lambda i, j: (i, 0)),
            pl.BlockSpec((1, D_MODEL), lambda i, j: (0, 0)),
            pl.BlockSpec((D_MODEL, HP * HEAD_DIM), lambda i, j: (0, j)),
            pl.BlockSpec((HP, 1, HEAD_DIM), lambda i, j: (jnp.minimum(j, n_norm_steps - 1), 0, 0)),
            pl.BlockSpec((tm, HEAD_DIM), lambda i, j: (i % nsb, 0)),
            pl.BlockSpec((tm, HEAD_DIM), lambda i, j: (i % nsb, 0)),
            pl.BlockSpec((tm, HEAD_DIM), lambda i, j: (i % nsb, 0)),
        ],
        out_specs=[
            pl.BlockSpec((HP, tm, HEAD_DIM), lambda i, j: (j, i, 0)),
            pl.BlockSpec((2, tm, HEAD_DIM), lambda i, j: (0, i, 0)),
        ],
        out_shape=[
            jax.ShapeDtypeStruct((N_HEADS_PAD, T, HEAD_DIM), bf16),
            jax.ShapeDtypeStruct((2, T, HEAD_DIM), f32),
        ],
        scratch_shapes=[pltpu.VMEM((tm, D_MODEL), bf16)],
        compiler_params=_cparams(("parallel", "arbitrary")),
        name="inproj",
    )(x2, attn_g, w_r, gains, rope_c, rope_sa, rope_sb)


def _gelu_tanh(x):
    return 0.5 * x * (1.0 + jnp.tanh(math.sqrt(2.0 / math.pi) * (x + 0.044715 * (x * x * x))))


def _compress_kernel(c_ref, pe_ref, w1_ref, w2_ref, gain_ref, rc_ref, rsa_ref, rsb_ref,
                     o_ref, *, do_norm):
    half = CMP_STRIDE * HEAD_DIM
    c = c_ref[0]
    nc = c.shape[0]
    w1 = w1_ref[...].astype(bf16)
    a = jnp.dot(c, w1[:half], preferred_element_type=f32)
    b = jnp.dot(c, w1[half:], preferred_element_type=f32)
    peb = jnp.dot(pe_ref[...], w1, preferred_element_type=f32)[0:1]
    hid = a + pltpu.roll(b, nc - 1, 0) + peb
    act = _gelu_tanh(hid)
    out = jnp.dot(act.astype(bf16), w2_ref[...].astype(bf16), preferred_element_type=f32)
    if do_norm:
        out = _rope(_rms(out, gain_ref[...]), rc_ref[...], rsa_ref[...], rsb_ref[...])
    o_ref[0] = out.astype(bf16)


def _compress(chunks, pe, w1, w2, gain, rc, rsa, rsb, do_norm):
    ng, nc, width = chunks.shape
    pe8 = jnp.broadcast_to(pe.reshape(1, CMP_LEN * HEAD_DIM), (8, CMP_LEN * HEAD_DIM)).astype(bf16)
    full = lambda shape: pl.BlockSpec(shape, lambda g: (0,) * len(shape))
    return pl.pallas_call(
        functools.partial(_compress_kernel, do_norm=do_norm),
        grid=(ng,),
        in_specs=[
            pl.BlockSpec((1, nc, width), lambda g: (g, 0, 0)),
            full((8, CMP_LEN * HEAD_DIM)),
            full((CMP_LEN * HEAD_DIM, CMP_HIDDEN)),
            full((CMP_HIDDEN, HEAD_DIM)),
            full((1, HEAD_DIM)),
            full((nc, HEAD_DIM)), full((nc, HEAD_DIM)), full((nc, HEAD_DIM)),
        ],
        out_specs=pl.BlockSpec((1, nc, HEAD_DIM), lambda g: (g, 0, 0)),
        out_shape=jax.ShapeDtypeStruct((ng, nc, HEAD_DIM), bf16),
        compiler_params=_cparams(("parallel",)),
        name="compress_k" if do_norm else "compress_v",
    )(chunks, pe8, w1, w2, gain, rc, rsa, rsb)


def _sigmoid(x):
    return 1.0 / (1.0 + jnp.exp(-x))


def _nsa_cmp_kernel(q_ref, kc_ref, vc_ref, g_ref, smt_ref, oc_ref, bias_ref, *, tq):
    q0 = pl.program_id(2) * tq
    R = NSA_GROUP * tq
    q = q_ref[...].reshape(R, HEAD_DIM)
    kc = kc_ref[0]
    vc = vc_ref[0]
    nc = kc.shape[0]
    s = lax.dot_general(q, kc, (((1,), (1,)), ((), ())), preferred_element_type=f32)
    t = q0 + (lax.broadcasted_iota(i32, (R, nc), 0) & (tq - 1))
    cend = lax.broadcasted_iota(i32, (R, nc), 1) * CMP_STRIDE + (CMP_LEN - 1)
    mask = cend <= t
    s = jnp.where(mask, s, NEG_INF)
    m = jnp.max(s, axis=-1, keepdims=True)
    p = jnp.where(mask, jnp.exp(s - m), 0.0)
    p = p / jnp.maximum(jnp.sum(p, axis=-1, keepdims=True), 1e-30)
    o = jnp.dot(p.astype(bf16), vc, preferred_element_type=f32)

    psum = p[0:tq] + p[tq:2 * tq] + p[2 * tq:3 * tq] + p[3 * tq:4 * tq]
    imp = lax.dot_general(smt_ref[...], psum, (((1,), (1,)), ((), ())),
                          precision=lax.Precision.HIGHEST, preferred_element_type=f32)
    nselp = imp.shape[0]
    jf = lax.broadcasted_iota(i32, (nselp, tq), 0).astype(f32)
    cur = ((q0 + lax.broadcasted_iota(i32, (nselp, tq), 1)) >> SEL_SHIFT).astype(f32)
    forced = (jf == 0.0) | (jf == cur) | (jf == cur - 1.0)
    imp = jnp.where(forced, FORCED_SCORE, imp)
    imp = jnp.where(jf <= cur, imp, NEG_INF)
    sel = jnp.zeros((nselp, tq), f32)
    work = imp
    for _ in range(N_SELECTED):
        mx = jnp.max(work, axis=0, keepdims=True)
        first = jnp.min(jnp.where(work == mx, jf, float(nselp)), axis=0, keepdims=True)
        pick = jf == first
        sel = jnp.where(pick & (mx > -1.0), 1.0, sel)
        work = jnp.where(pick, -3e38, work)
    bias_t = jnp.where(sel > 0.5, 0.0, MASK_VAL)
    bias_ref[0] = bias_t.T.astype(bf16)

    sig = _sigmoid(g_ref[0])
    for gi in range(NSA_GROUP):
        col = gi * N_BRANCH
        oc_ref[:, gi * HEAD_DIM:(gi + 1) * HEAD_DIM] = o[gi * tq:(gi + 1) * tq] * sig[:, col:col + 1]


def _nsa_cmp(P, kcmp, vcmp, gate, smt, B, S, tq):
    T = B * S
    nqb = S // tq
    nc = kcmp.shape[1]
    nselp = smt.shape[0]
    return pl.pallas_call(
        functools.partial(_nsa_cmp_kernel, tq=tq),
        grid=(NSA_KV_HEADS, B, nqb),
        in_specs=[
            pl.BlockSpec((NSA_GROUP, tq, HEAD_DIM), lambda h, b, i: (h, b * nqb + i, 0)),
            pl.BlockSpec((1, nc, HEAD_DIM), lambda h, b, i: (h * B + b, 0, 0)),
            pl.BlockSpec((1, nc, HEAD_DIM), lambda h, b, i: (h * B + b, 0, 0)),
            pl.BlockSpec((1, tq, HEAD_DIM), lambda h, b, i: (h, b * nqb + i, 0)),
            pl.BlockSpec((nselp, nc), lambda h, b, i: (0, 0)),
        ],
        out_specs=[
            pl.BlockSpec((tq, NSA_GROUP * HEAD_DIM), lambda h, b, i: (b * nqb + i, h)),
            pl.BlockSpec((1, tq, nselp), lambda h, b, i: (h, b * nqb + i, 0)),
        ],
        out_shape=[
            jax.ShapeDtypeStruct((T, NSA_HEADS * HEAD_DIM), f32),
            jax.ShapeDtypeStruct((NSA_KV_HEADS, T, nselp), bf16),
        ],
        compiler_params=_cparams(("parallel", "parallel", "parallel")),
        name="nsa_cmp",
    )(P, kcmp, vcmp, gate, smt)


def _lane_tile(x, n):
    return x if n == 1 else jnp.concatenate([x] * n, axis=1)


def _softmax_step(s, v_aug, m_ref, acc_ref):
    tk = s.shape[1]
    m_prev = m_ref[...]
    m_new = jnp.maximum(m_prev, jnp.max(s, axis=1, keepdims=True))
    alpha = jnp.exp(m_prev - m_new)
    p = jnp.exp(s - _lane_tile(m_new, tk // LANES))
    acc_ref[...] = _lane_tile(alpha, 2) * acc_ref[...] + jnp.dot(
        p.astype(bf16), v_aug, preferred_element_type=f32)
    m_ref[...] = m_new


def _nsa_sel_kernel(q_ref, bias_ref, ks_ref, vs_ref, kw_ref, vw_ref, g_ref, oc_ref,
                    o_ref, qaug_ref, m_ref, acc_ref, *, tq, tk, tkw):
    q0 = pl.program_id(2) * tq
    R = NSA_GROUP * tq
    nselp = bias_ref.shape[2]
    q = q_ref[...].reshape(R, HEAD_DIM)
    qaug_ref[:, 0:HEAD_DIM] = q
    bias = bias_ref[0]
    for gi in range(NSA_GROUP):
        qaug_ref[gi * tq:(gi + 1) * tq, HEAD_DIM:HEAD_DIM + nselp] = bias
    sig = _sigmoid(g_ref[0])

    def init():
        m_ref[...] = jnp.full(m_ref.shape, M_INIT, f32)
        acc_ref[...] = jnp.zeros(acc_ref.shape, f32)

    def finish():
        return acc_ref[:, 0:HEAD_DIM] / jnp.maximum(acc_ref[:, HEAD_DIM:2 * HEAD_DIM], 1e-30)

    init()
    ones_k = jnp.ones((tk, HEAD_DIM), bf16)

    def sel_step(kt, carry):
        k0 = pl.multiple_of(kt * tk, tk)
        k = ks_ref[0, pl.ds(k0, tk), :]
        blk = (k0 + lax.broadcasted_iota(i32, (tk, nselp), 0)) >> SEL_SHIFT
        onehot = jnp.where(blk == lax.broadcasted_iota(i32, (tk, nselp), 1), 1.0, 0.0).astype(bf16)
        kaug = jnp.concatenate([k, onehot], axis=1)
        s = lax.dot_general(qaug_ref[...], kaug, (((1,), (1,)), ((), ())),
                            preferred_element_type=f32)
        t = q0 + (lax.broadcasted_iota(i32, (R, tk), 0) & (tq - 1))
        kp = k0 + lax.broadcasted_iota(i32, (R, tk), 1)
        s = jnp.where(kp <= t, s, MASK_VAL)
        vaug = jnp.concatenate([vs_ref[0, pl.ds(k0, tk), :], ones_k], axis=1)
        _softmax_step(s, vaug, m_ref, acc_ref)
        return carry

    lax.fori_loop(0, (q0 + tq - 1) // tk + 1, sel_step, 0)
    o_s = finish()

    init()
    ones_w = jnp.ones((tkw, HEAD_DIM), bf16)

    def win_step(kt, carry):
        k0 = pl.multiple_of(kt * tkw, tkw)
        k = kw_ref[0, pl.ds(k0, tkw), :]
        s = lax.dot_general(qaug_ref[:, 0:HEAD_DIM], k, (((1,), (1,)), ((), ())),
                            preferred_element_type=f32)
        t = q0 + (lax.broadcasted_iota(i32, (R, tkw), 0) & (tq - 1))
        kp = k0 + lax.broadcasted_iota(i32, (R, tkw), 1)
        s = jnp.where((kp <= t) & (t - kp < WINDOW), s, MASK_VAL)
        vaug = jnp.concatenate([vw_ref[0, pl.ds(k0, tkw), :], ones_w], axis=1)
        _softmax_step(s, vaug, m_ref, acc_ref)
        return carry

    lo = jnp.maximum(q0 - WINDOW + 1, 0) // tkw
    lax.fori_loop(lo, (q0 + tq - 1) // tkw + 1, win_step, 0)
    o_w = finish()

    for gi in range(NSA_GROUP):
        rows = slice(gi * tq, (gi + 1) * tq)
        cols = slice(gi * HEAD_DIM, (gi + 1) * HEAD_DIM)
        c = gi * N_BRANCH
        o_ref[:, cols] = (oc_ref[:, cols] + o_s[rows] * sig[:, c + 1:c + 2]
                          + o_w[rows] * sig[:, c + 2:c + 3]).astype(bf16)


def _nsa_sel(P, bias, gate, o_c, B, S, tq, tk, tkw):
    T = B * S
    nqb = S // tq
    nselp = bias.shape[2]
    R = NSA_GROUP * tq
    kv_spec = lambda base: pl.BlockSpec((1, S, HEAD_DIM), lambda h, b, i: (base + h, b, 0))
    return pl.pallas_call(
        functools.partial(_nsa_sel_kernel, tq=tq, tk=tk, tkw=tkw),
        grid=(NSA_KV_HEADS, B, nqb),
        in_specs=[
            pl.BlockSpec((NSA_GROUP, tq, HEAD_DIM), lambda h, b, i: (h, b * nqb + i, 0)),
            pl.BlockSpec((1, tq, nselp), lambda h, b, i: (h, b * nqb + i, 0)),
            kv_spec(HD_KS), kv_spec(HD_VS), kv_spec(HD_KW), kv_spec(HD_VW),
            pl.BlockSpec((1, tq, HEAD_DIM), lambda h, b, i: (h, b * nqb + i, 0)),
            pl.BlockSpec((tq, NSA_GROUP * HEAD_DIM), lambda h, b, i: (b * nqb + i, h)),
        ],
        out_specs=pl.BlockSpec((tq, NSA_GROUP * HEAD_DIM), lambda h, b, i: (b * nqb + i, h)),
        out_shape=jax.ShapeDtypeStruct((T, NSA_HEADS * HEAD_DIM), bf16),
        scratch_shapes=[
            pltpu.VMEM((R, HEAD_DIM + nselp), bf16),
            pltpu.VMEM((R, LANES), f32),
            pltpu.VMEM((R, 2 * HEAD_DIM), f32),
        ],
        compiler_params=_cparams(("parallel", "parallel", "arbitrary")),
        name="nsa_sel_win",
    )(P, bias, P, P, P, P, gate, o_c)


def _diff_kernel(q_ref, k_ref, v_ref, lam_ref, sg_ref, o_ref, m_ref, l_ref, acc_ref,
                 *, tq, tk, lambda_init):
    q0 = pl.program_id(2) * tq
    R = 2 * tq
    q1 = q_ref[0]
    q2 = q_ref[1]
    m_ref[...] = jnp.full(m_ref.shape, M_INIT, f32)
    l_ref[...] = jnp.zeros(l_ref.shape, f32)
    acc_ref[...] = jnp.zeros(acc_ref.shape, f32)

    def step(kt, carry):
        k0 = pl.multiple_of(kt * tk, tk)
        nt = (((1,), (1,)), ((), ()))
        s1 = lax.dot_general(q1, k_ref[0, pl.ds(k0, tk), :], nt, preferred_element_type=f32)
        s2 = lax.dot_general(q2, k_ref[1, pl.ds(k0, tk), :], nt, preferred_element_type=f32)
        s = jnp.concatenate([s1, s2], axis=0)
        t = q0 + (lax.broadcasted_iota(i32, (R, tk), 0) & (tq - 1))
        kp = k0 + lax.broadcasted_iota(i32, (R, tk), 1)
        s = jnp.where(kp <= t, s, MASK_VAL)
        v = jnp.concatenate([v_ref[0, pl.ds(k0, tk), :], v_ref[1, pl.ds(k0, tk), :]], axis=1)
        m_prev = m_ref[...]
        m_new = jnp.maximum(m_prev, jnp.max(s, axis=1, keepdims=True))
        alpha = jnp.exp(m_prev - m_new)
        p = jnp.exp(s - _lane_tile(m_new, tk // LANES))
        l_ref[...] = alpha * l_ref[...] + jnp.sum(p, axis=1, keepdims=True)
        acc_ref[...] = _lane_tile(alpha, 2) * acc_ref[...] + jnp.dot(
            p.astype(bf16), v, preferred_element_type=f32)
        m_ref[...] = m_new
        return carry

    lax.fori_loop(0, (q0 + tq - 1) // tk + 1, step, 0)

    lq = lam_ref[...]
    lam = (jnp.exp(jnp.sum(lq[0:1] * lq[1:2], axis=1, keepdims=True))
           - jnp.exp(jnp.sum(lq[2:3] * lq[3:4], axis=1, keepdims=True)) + lambda_init)
    o = acc_ref[...] / _lane_tile(jnp.maximum(l_ref[...], 1e-30), 2)
    a = o[0:tq] - lam * o[tq:R]
    o_ref[...] = (_rms(a, sg_ref[...]) * (1.0 - lambda_init)).astype(bf16)


def _diff_attn(P, lam_vecs, subln_g, B, S, tq, tk, lambda_init):
    T = B * S
    nqb = S // tq
    return pl.pallas_call(
        functools.partial(_diff_kernel, tq=tq, tk=tk, lambda_init=lambda_init),
        grid=(DIFF_HEADS, B, nqb),
        in_specs=[
            pl.BlockSpec((2, tq, HEAD_DIM), lambda h, b, i: (HD_QD // 2 + h, b * nqb + i, 0)),
            pl.BlockSpec((2, S, HEAD_DIM), lambda h, b, i: (HD_KD // 2 + h, b, 0)),
            pl.BlockSpec((2, S, HEAD_DIM), lambda h, b, i: (HD_VD // 2 + h, b, 0)),
            pl.BlockSpec((8, HEAD_DIM), lambda h, b, i: (0, 0)),
            pl.BlockSpec((1, DIFF_V_DIM), lambda h, b, i: (0, 0)),
        ],
        out_specs=pl.BlockSpec((tq, DIFF_V_DIM), lambda h, b, i: (b * nqb + i, h)),
        out_shape=jax.ShapeDtypeStruct((T, DIFF_HEADS * DIFF_V_DIM), bf16),
        scratch_shapes=[
            pltpu.VMEM((2 * tq, LANES), f32),
            pltpu.VMEM((2 * tq, LANES), f32),
            pltpu.VMEM((2 * tq, DIFF_V_DIM), f32),
        ],
        compiler_params=_cparams(("parallel", "parallel", "arbitrary")),
        name="diff_attn",
    )(P, P, P, lam_vecs, subln_g)


def _outproj_kernel(x_ref, on_ref, od_ref, w_ref, g_ref, wr_ref, br_ref,
                    h_ref, hn_ref, lg_ref):
    half = NSA_HEADS * HEAD_DIM
    h = x_ref[...] + jnp.dot(on_ref[...], w_ref[0:half], preferred_element_type=f32) \
        + jnp.dot(od_ref[...], w_ref[half:], preferred_element_type=f32)
    h_ref[...] = h
    hn = _rms(h, g_ref[...])
    hn_ref[...] = hn
    lg_ref[...] = jnp.dot(hn, wr_ref[...], precision=lax.Precision.HIGHEST,
                          preferred_element_type=f32) + br_ref[...]


def _outproj(x2, o_nsa, o_diff, w_out_b, ffn_g, w_router_p, b_router_p, tm):
    T = x2.shape[0]
    full = lambda shape: pl.BlockSpec(shape, lambda i: (0,) * len(shape))
    return pl.pallas_call(
        _outproj_kernel,
        grid=(T // tm,),
        in_specs=[
            pl.BlockSpec((tm, D_MODEL), lambda i: (i, 0)),
            pl.BlockSpec((tm, NSA_HEADS * HEAD_DIM), lambda i: (i, 0)),
            pl.BlockSpec((tm, DIFF_HEADS * DIFF_V_DIM), lambda i: (i, 0)),
            full((D_MODEL, D_MODEL)),
            full((1, D_MODEL)),
            full((D_MODEL, LANES)),
            full((1, LANES)),
        ],
        out_specs=[
            pl.BlockSpec((tm, D_MODEL), lambda i: (i, 0)),
            pl.BlockSpec((tm, D_MODEL), lambda i: (i, 0)),
            pl.BlockSpec((tm, LANES), lambda i: (i, 0)),
        ],
        out_shape=[
            jax.ShapeDtypeStruct((T, D_MODEL), f32),
            jax.ShapeDtypeStruct((T, D_MODEL), f32),
            jax.ShapeDtypeStruct((T, LANES), f32),
        ],
        compiler_params=_cparams(("parallel",)),
        name="outproj_router",
    )(x2, o_nsa, o_diff, w_out_b, ffn_g, w_router_p, b_router_p)


ISSUE_UNROLL = 8


def _start_row_gathers(idx_fn, n, src_ref, dst_ref, sem):
    def body(c, carry):
        for u in range(ISSUE_UNROLL):
            r = c * ISSUE_UNROLL + u
            pltpu.make_async_copy(src_ref.at[pl.ds(idx_fn(r), 1)], dst_ref.at[pl.ds(r, 1)], sem).start()
        return carry

    lax.fori_loop(0, n // ISSUE_UNROLL, body, 0)


def _wait_row_gathers(n, src_ref, dst_ref, sem):
    for r in range(n):
        pltpu.make_async_copy(src_ref.at[pl.ds(0, 1)], dst_ref.at[pl.ds(r, 1)], sem).wait()


def _gather_kernel(st_ref, sb_ref, src_ref, o_ref, buf_ref, sem, *, tm, n_src):
    b = pl.program_id(0)
    nb = pl.num_programs(0)

    def issue(blk, slot):
        base = sb_ref[blk]
        _start_row_gathers(lambda r: st_ref[jnp.minimum(base + r, n_src - 1)], tm,
                           src_ref, buf_ref.at[slot], sem.at[slot])

    @pl.when(b == 0)
    def _():
        issue(0, 0)

    @pl.when(b + 1 < nb)
    def _():
        issue(b + 1, (b + 1) % 2)

    slot = b % 2
    _wait_row_gathers(tm, src_ref, buf_ref.at[slot], sem.at[slot])
    o_ref[...] = buf_ref[slot]


def _gather_rows(sorted_tok, src_base, src, n_blocks, tm):
    return pl.pallas_call(
        functools.partial(_gather_kernel, tm=tm, n_src=sorted_tok.shape[0]),
        grid_spec=pltpu.PrefetchScalarGridSpec(
            num_scalar_prefetch=2,
            grid=(n_blocks,),
            in_specs=[pl.BlockSpec(memory_space=pl.ANY)],
            out_specs=pl.BlockSpec((tm, src.shape[1]), lambda b, st, sb: (b, 0)),
            scratch_shapes=[pltpu.VMEM((2, tm, src.shape[1]), src.dtype), pltpu.SemaphoreType.DMA((2,))],
        ),
        out_shape=jax.ShapeDtypeStruct((n_blocks * tm, src.shape[1]), src.dtype),
        compiler_params=_cparams(("arbitrary",)),
        name="moe_gather",
    )(sorted_tok, src_base, src)


def _expert_changed(be_ref, b):
    prev = be_ref[jnp.maximum(b - 1, 0)]
    return (b == 0) | (be_ref[b] != prev)


def _gateup_kernel(be_ref, nu_ref, x_ref, wg_ref, wu_ref, bg_ref, bu_ref, o_ref, wgb_ref, wub_ref):
    b = pl.program_id(1)

    @pl.when(b < nu_ref[0])
    def _():
        @pl.when(_expert_changed(be_ref, b))
        def _():
            wgb_ref[...] = wg_ref[0].astype(bf16)
            wub_ref[...] = wu_ref[0].astype(bf16)

        x = x_ref[...].astype(bf16)
        gl = jnp.dot(x, wgb_ref[...], preferred_element_type=f32) + bg_ref[0]
        up = jnp.dot(x, wub_ref[...], preferred_element_type=f32) + bu_ref[0]
        gl = jnp.minimum(gl, SWIGLU_LIMIT)
        up = jnp.clip(up, -SWIGLU_LIMIT, SWIGLU_LIMIT)
        o_ref[...] = ((up + 1.0) * (gl * _sigmoid(SWIGLU_ALPHA * gl))).astype(bf16)

    @pl.when(b >= nu_ref[0])
    def _():
        o_ref[...] = jnp.zeros(o_ref.shape, o_ref.dtype)


def _gateup(blk_e, n_used, xs, w_gate_up, b_gate_up3, tm, tn):
    n_rows = xs.shape[0]
    nb = n_rows // tm
    nj = D_FF // tn
    blk = lambda b, nu: jnp.minimum(b, nu[0] - 1)
    return pl.pallas_call(
        _gateup_kernel,
        grid_spec=pltpu.PrefetchScalarGridSpec(
            num_scalar_prefetch=2,
            grid=(nj, nb),
            in_specs=[
                pl.BlockSpec((tm, D_MODEL), lambda j, b, be, nu: (blk(b, nu), 0)),
                pl.BlockSpec((1, D_MODEL, tn), lambda j, b, be, nu: (be[blk(b, nu)], 0, j)),
                pl.BlockSpec((1, D_MODEL, tn), lambda j, b, be, nu: (be[blk(b, nu)], 0, nj + j)),
                pl.BlockSpec((1, 1, tn), lambda j, b, be, nu: (be[blk(b, nu)], 0, j)),
                pl.BlockSpec((1, 1, tn), lambda j, b, be, nu: (be[blk(b, nu)], 0, nj + j)),
            ],
            out_specs=pl.BlockSpec((tm, tn), lambda j, b, be, nu: (b, j)),
            scratch_shapes=[pltpu.VMEM((D_MODEL, tn), bf16), pltpu.VMEM((D_MODEL, tn), bf16)],
        ),
        out_shape=jax.ShapeDtypeStruct((n_rows, D_FF), bf16),
        compiler_params=_cparams(("arbitrary", "arbitrary")),
        name="moe_gate_up",
    )(blk_e, n_used, xs, w_gate_up, w_gate_up, b_gate_up3, b_gate_up3)


def _down_kernel(be_ref, nu_ref, h_ref, w_ref, b_ref, o_ref, wb_ref):
    b = pl.program_id(1)

    @pl.when(b < nu_ref[0])
    def _():
        @pl.when(_expert_changed(be_ref, b))
        def _():
            wb_ref[...] = w_ref[0].astype(bf16)

        o_ref[...] = jnp.dot(h_ref[...], wb_ref[...], preferred_element_type=f32) + b_ref[0]

    @pl.when(b >= nu_ref[0])
    def _():
        o_ref[...] = jnp.zeros(o_ref.shape, o_ref.dtype)


def _down(blk_e, n_used, hb, w_down, b_down3, tm, tn):
    n_rows = hb.shape[0]
    nb = n_rows // tm
    nj = D_MODEL // tn
    blk = lambda b, nu: jnp.minimum(b, nu[0] - 1)
    return pl.pallas_call(
        _down_kernel,
        grid_spec=pltpu.PrefetchScalarGridSpec(
            num_scalar_prefetch=2,
            grid=(nj, nb),
            in_specs=[
                pl.BlockSpec((tm, D_FF), lambda j, b, be, nu: (blk(b, nu), 0)),
                pl.BlockSpec((1, D_FF, tn), lambda j, b, be, nu: (be[blk(b, nu)], 0, j)),
                pl.BlockSpec((1, 1, tn), lambda j, b, be, nu: (be[blk(b, nu)], 0, j)),
            ],
            out_specs=pl.BlockSpec((tm, tn), lambda j, b, be, nu: (b, j)),
            scratch_shapes=[pltpu.VMEM((D_FF, tn), bf16)],
        ),
        out_shape=jax.ShapeDtypeStruct((n_rows, D_MODEL), f32),
        compiler_params=_cparams(("arbitrary", "arbitrary")),
        name="moe_down",
    )(blk_e, n_used, hb, w_down, b_down3)


def _combine_kernel(pos_ref, h_ref, gt_ref, y_ref, o_ref, buf_ref, sem, *, tc):
    i = pl.program_id(0)
    n = pl.num_programs(0)
    rows = TOP_K * tc

    def issue(step, slot):
        base = step * rows
        _start_row_gathers(lambda r: pos_ref[base + r], rows, y_ref, buf_ref.at[slot], sem.at[slot])

    @pl.when(i == 0)
    def _():
        issue(0, 0)

    @pl.when(i + 1 < n)
    def _():
        issue(i + 1, (i + 1) % 2)

    slot = i % 2
    _wait_row_gathers(rows, y_ref, buf_ref.at[slot], sem.at[slot])
    gt = gt_ref[...]
    out = h_ref[...]
    for k in range(TOP_K):
        out = out + gt[:, k:k + 1] * buf_ref[slot, k * tc:(k + 1) * tc, :]
    o_ref[...] = out


def _combine(pos_blocks, h, gate_pad, y, tc):
    T = h.shape[0]
    return pl.pallas_call(
        functools.partial(_combine_kernel, tc=tc),
        grid_spec=pltpu.PrefetchScalarGridSpec(
            num_scalar_prefetch=1,
            grid=(T // tc,),
            in_specs=[
                pl.BlockSpec((tc, D_MODEL), lambda i, pos: (i, 0)),
                pl.BlockSpec((tc, LANES), lambda i, pos: (i, 0)),
                pl.BlockSpec(memory_space=pl.ANY),
            ],
            out_specs=pl.BlockSpec((tc, D_MODEL), lambda i, pos: (i, 0)),
            scratch_shapes=[pltpu.VMEM((2, TOP_K * tc, D_MODEL), f32), pltpu.SemaphoreType.DMA((2,))],
        ),
        out_shape=jax.ShapeDtypeStruct((T, D_MODEL), f32),
        compiler_params=_cparams(("arbitrary",)),
        name="moe_combine",
    )(pos_blocks, h, gate_pad, y)


def _rearranged_w_in(w_in):
    splits = np.cumsum([0, 1024, 256, 256, 256, 256, 256, 256, 24, 1024, 1024, 1024])
    q, kc, vc, ks, vs, kw, vw, g, qd, kd, vd = [w_in[:, splits[n]:splits[n + 1]] for n in range(11)]
    gcols = NSA_GROUP * N_BRANCH
    zpad = lambda n: jnp.zeros((D_MODEL, n), w_in.dtype)
    parts = [q, ks, kw, qd, kd, kc, vc, vs, vw, vd,
             g[:, :gcols], zpad(HEAD_DIM - gcols), g[:, gcols:], zpad(HEAD_DIM - gcols),
             zpad((N_HEADS_PAD - HD_GATE - 2) * HEAD_DIM)]
    return jnp.concatenate(parts, axis=1).astype(bf16)


def _sel_map_t(S, nselp):
    n_chunks = S // CMP_STRIDE
    n_cmp = (S - CMP_LEN) // CMP_STRIDE + 1
    n_sel = S // SEL_BLOCK
    c_start = np.arange(n_chunks) * CMP_STRIDE
    s_start = np.arange(nselp) * SEL_BLOCK
    ov = (c_start[None, :] < s_start[:, None] + SEL_BLOCK) & (c_start[None, :] + CMP_LEN > s_start[:, None])
    ov &= (np.arange(n_chunks)[None, :] < n_cmp) & (np.arange(nselp)[:, None] < n_sel)
    return jnp.asarray(ov, f32)


def _attention_block(x2, B, S, attn_norm_g, w_in, nsa_q_norm_g, nsa_k_norm_g, pos_k, pos_v,
                     k_w1, k_w2, v_w1, v_w2, diff_q_norm_g, diff_k_norm_g, lq1, lk1, lq2, lk2,
                     subln_g, lambda_init):
    T = B * S
    scale = HEAD_DIM ** -0.5
    tm = min(1024, S)
    rope_c, rope_sa, rope_sb = _rope_tables(np.arange(S))
    gains = jnp.concatenate([
        jnp.broadcast_to(nsa_q_norm_g * scale, (NSA_HEADS, HEAD_DIM)),
        jnp.broadcast_to(nsa_k_norm_g[1], (NSA_KV_HEADS, HEAD_DIM)),
        jnp.broadcast_to(nsa_k_norm_g[2], (NSA_KV_HEADS, HEAD_DIM)),
        jnp.broadcast_to(diff_q_norm_g * scale, (2 * DIFF_HEADS, HEAD_DIM)),
        jnp.broadcast_to(diff_k_norm_g, (2 * DIFF_HEADS, HEAD_DIM)),
    ], axis=0).reshape(N_NORM_HEADS, 1, HEAD_DIM)
    P, gate = _inproj(x2, attn_norm_g.reshape(1, D_MODEL), _rearranged_w_in(w_in), gains,
                      rope_c, rope_sa, rope_sb, S, tm)

    n_chunks = S // CMP_STRIDE
    cmp_pos = np.arange(n_chunks) * CMP_STRIDE + (CMP_LEN - 1)
    cc, csa, csb = _rope_tables(cmp_pos)
    chunk_view = lambda hd: P[hd:hd + NSA_KV_HEADS].reshape(NSA_KV_HEADS * B, n_chunks, CMP_STRIDE * HEAD_DIM)
    kcmp = _compress(chunk_view(HD_KC), pos_k, k_w1, k_w2, nsa_k_norm_g[0].reshape(1, HEAD_DIM),
                     cc, csa, csb, True)
    vcmp = _compress(chunk_view(HD_VC), pos_v, v_w1, v_w2, nsa_k_norm_g[0].reshape(1, HEAD_DIM),
                     cc, csa, csb, False)

    nselp = -(-(S // SEL_BLOCK) // LANES) * LANES
    tq = min(256, S)
    o_c, bias = _nsa_cmp(P, kcmp, vcmp, gate, _sel_map_t(S, nselp), B, S, tq)
    o_nsa = _nsa_sel(P, bias, gate, o_c, B, S, tq, min(512, S), min(256, S))

    lam_vecs = jnp.concatenate([lq1[None], lk1[None], lq2[None], lk2[None],
                                jnp.zeros((4, HEAD_DIM), f32)], axis=0)
    o_diff = _diff_attn(P, lam_vecs, subln_g.reshape(1, DIFF_V_DIM), B, S,
                        min(512, S), min(512, S), lambda_init)
    return o_nsa, o_diff


def _moe_block(h, hn, logits, w_gate_up, b_gate_up, w_down, b_down, tm=256):
    T = h.shape[0]
    TK = T * TOP_K
    top_v, top_e = lax.top_k(logits[:, :N_EXPERTS], TOP_K)
    gate = jax.nn.softmax(top_v, axis=-1)
    flat_e = top_e.reshape(TK).astype(i32)
    se, order = lax.sort((flat_e, jnp.arange(TK, dtype=i32)), num_keys=1, is_stable=True)
    st = order // TOP_K
    experts = jnp.arange(N_EXPERTS, dtype=i32)
    counts = jnp.sum((flat_e[:, None] == experts[None, :]).astype(i32), axis=0)
    padded = (counts + tm - 1) // tm * tm
    pad_end = jnp.cumsum(padded)
    pad_start = pad_end - padded
    cnt_start = jnp.cumsum(counts) - counts
    dest = pad_start[se] + (jnp.arange(TK, dtype=i32) - cnt_start[se])
    _, pos = lax.sort((order, dest), num_keys=1)
    n_blocks = -(-TK // tm) + N_EXPERTS
    blk_start = jnp.arange(n_blocks, dtype=i32) * tm
    blk_e = jnp.minimum(jnp.sum((pad_end[None, :] <= blk_start[:, None]).astype(i32), axis=1), N_EXPERTS - 1)
    src_base = cnt_start[blk_e] + (blk_start - pad_start[blk_e])
    n_used = (pad_end[-1] // tm).astype(i32).reshape(1)

    xs = _gather_rows(st, src_base.astype(i32), hn, n_blocks, tm)
    hb = _gateup(blk_e, n_used, xs, w_gate_up, b_gate_up.reshape(N_EXPERTS, 1, 2 * D_FF), tm, 512)
    y = _down(blk_e, n_used, hb, w_down, b_down.reshape(N_EXPERTS, 1, D_MODEL), tm, 1024)

    tc = 64
    pos_blocks = pos.reshape(T // tc, tc, TOP_K).transpose(0, 2, 1).reshape(TK)
    gate_pad = jnp.pad(gate, ((0, 0), (0, LANES - TOP_K)))
    return _combine(pos_blocks, h, gate_pad, y, tc)


def kernel(x, attn_norm_g, w_in, nsa_q_norm_g, nsa_k_norm_g, nsa_cmp_pos_k, nsa_cmp_pos_v, nsa_cmp_k_w1, nsa_cmp_k_w2, nsa_cmp_v_w1, nsa_cmp_v_w2, diff_q_norm_g, diff_k_norm_g, diff_lambda_q1, diff_lambda_k1, diff_lambda_q2, diff_lambda_k2, diff_subln_g, w_out, ffn_norm_g, w_router, b_router, w_gate_up, b_gate_up, w_down, b_down):
    B, S, _ = x.shape
    T = B * S
    depth = attn_norm_g.shape[0]
    h = x.reshape(T, D_MODEL)
    for l in range(depth):
        lambda_init = 0.8 - 0.6 * math.exp(-0.3 * l)
        o_nsa, o_diff = _attention_block(
            h, B, S, attn_norm_g[l], w_in[l], nsa_q_norm_g[l], nsa_k_norm_g[l], nsa_cmp_pos_k[l],
            nsa_cmp_pos_v[l], nsa_cmp_k_w1[l], nsa_cmp_k_w2[l], nsa_cmp_v_w1[l], nsa_cmp_v_w2[l],
            diff_q_norm_g[l], diff_k_norm_g[l], diff_lambda_q1[l], diff_lambda_k1[l],
            diff_lambda_q2[l], diff_lambda_k2[l], diff_subln_g[l], lambda_init)
        w_router_p = jnp.pad(w_router[l], ((0, 0), (0, LANES - N_EXPERTS)))
        b_router_p = jnp.pad(b_router[l], (0, LANES - N_EXPERTS)).reshape(1, LANES)
        h_mid, hn, logits = _outproj(h, o_nsa, o_diff, w_out[l].astype(bf16),
                                     ffn_norm_g[l].reshape(1, D_MODEL), w_router_p, b_router_p,
                                     min(512, T))
        h = _moe_block(h_mid, hn, logits, w_gate_up[l], b_gate_up[l], w_down[l], b_down[l])
    return h.reshape(B, S, D_MODEL)
```

```python
import functools
import math

import numpy as np
import jax
import jax.numpy as jnp
from jax import lax
from jax.experimental import pallas as pl
from jax.experimental.pallas import tpu as pltpu

f32 = jnp.float32
bf16 = jnp.bfloat16
i32 = jnp.int32

D_MODEL = 2048
HEAD_DIM = 128
ROT_DIM = HEAD_DIM // 4
ROPE_THETA = 500000.0
NORM_EPS = 1e-6
NEG_INF = -1e30

NSA_HEADS = 8
NSA_KV_HEADS = 2
NSA_GROUP = NSA_HEADS // NSA_KV_HEADS
N_BRANCH = 3
CMP_LEN = 32
CMP_STRIDE = 16
CMP_HIDDEN = 2 * HEAD_DIM
SEL_BLOCK = 64
SEL_SHIFT = 6
N_SELECTED = 16
WINDOW = 512
FORCED_SCORE = 1e9

DIFF_HEADS = 4
DIFF_V_DIM = 2 * HEAD_DIM

N_EXPERTS = 32
TOP_K = 4
D_FF = D_MODEL
SWIGLU_ALPHA = 1.702
SWIGLU_LIMIT = 7.0

LANES = 128

HD_Q = 0
HD_KS = 8
HD_KW = 10
HD_QD = 12
HD_KD = 20
N_NORM_HEADS = 28
HD_KC = 28
HD_VC = 30
HD_VS = 32
HD_VW = 34
HD_VD = 36
HD_GATE = 44
N_HEADS_PAD = 48
HP = 4

MASK_VAL = -1e30
M_INIT = -5e29

VMEM_LIMIT = 56 * 1024 * 1024


def _cparams(sem):
    return pltpu.CompilerParams(dimension_semantics=sem, vmem_limit_bytes=VMEM_LIMIT)


def _rope_tables(pos):
    inv = np.power(ROPE_THETA, -np.arange(0, ROT_DIM, 2, dtype=np.float64) / ROT_DIM)
    ang = pos.astype(np.float64)[:, None] * inv[None, :]
    cos, sin = np.cos(ang), np.sin(ang)
    n = pos.shape[0]
    half = ROT_DIM // 2
    c = np.concatenate([cos, cos, np.ones((n, HEAD_DIM - ROT_DIM))], axis=1)
    sa = np.concatenate([-sin, np.zeros((n, HEAD_DIM - half))], axis=1)
    sb = np.concatenate([np.zeros((n, half)), sin, np.zeros((n, HEAD_DIM - ROT_DIM))], axis=1)
    return (jnp.asarray(c, f32), jnp.asarray(sa, f32), jnp.asarray(sb, f32))


def _rope(y, c, sa, sb):
    half = ROT_DIM // 2
    return (y * c + pltpu.roll(y, HEAD_DIM - half, 1) * sa + pltpu.roll(y, half, 1) * sb)


def _rms(y, gain):
    ms = jnp.mean(y * y, axis=-1, keepdims=True)
    return y * lax.rsqrt(ms + NORM_EPS) * gain


def _inproj_kernel(x_ref, g_ref, w_ref, gain_ref, c_ref, sa_ref, sb_ref,
                   o_ref, gate_ref, xn_ref):
    j = pl.program_id(1)

    @pl.when(j == 0)
    def _():
        x = x_ref[...]
        xn_ref[...] = _rms(x, g_ref[...]).astype(bf16)

    acc = jnp.dot(xn_ref[...], w_ref[...], preferred_element_type=f32)

    @pl.when(j < N_NORM_HEADS // HP)
    def _():
        c, sa, sb = c_ref[...], sa_ref[...], sb_ref[...]
        for u in range(HP):
            y = _rms(acc[:, u * HEAD_DIM:(u + 1) * HEAD_DIM], gain_ref[u])
            o_ref[u] = _rope(y, c, sa, sb).astype(bf16)

    @pl.when(j >= N_NORM_HEADS // HP)
    def _():
        for u in range(HP):
            o_ref[u] = acc[:, u * HEAD_DIM:(u + 1) * HEAD_DIM].astype(bf16)

    @pl.when(j == HD_GATE // HP)
    def _():
        gate_ref[0] = acc[:, 0:HEAD_DIM]
        gate_ref[1] = acc[:, HEAD_DIM:2 * HEAD_DIM]


def _inproj(x2, attn_g, w_r, gains, rope_c, rope_sa, rope_sb, S, tm):
    T = x2.shape[0]
    nsb = S // tm
    n_norm_steps = N_NORM_HEADS // HP
    return pl.pallas_call(
        _inproj_kernel,
        grid=(T // tm, N_HEADS_PAD // HP),
        in_specs=[
            pl.BlockSpec((tm, D_MODEL), lambda i, j: (i, 0)),
            pl.BlockSpec((1, D_MODEL), lambda i, j: (0, 0)),
            pl.BlockSpec((D_MODEL, HP * HEAD_DIM), lambda i, j: (0, j)),
            pl.BlockSpec((HP, 1, HEAD_DIM), lambda i, j: (jnp.minimum(j, n_norm_steps - 1), 0, 0)),
            pl.BlockSpec((tm, HEAD_DIM), lambda i, j: (i % nsb, 0)),
            pl.BlockSpec((tm, HEAD_DIM), lambda i, j: (i % nsb, 0)),
            pl.BlockSpec((tm, HEAD_DIM), lambda i, j: (i % nsb, 0)),
        ],
        out_specs=[
            pl.BlockSpec((HP, tm, HEAD_DIM), lambda i, j: (j, i, 0)),
            pl.BlockSpec((2, tm, HEAD_DIM), lambda i, j: (0, i, 0)),
        ],
        out_shape=[
            jax.ShapeDtypeStruct((N_HEADS_PAD, T, HEAD_DIM), bf16),
            jax.ShapeDtypeStruct((2, T, HEAD_DIM), f32),
        ],
        scratch_shapes=[pltpu.VMEM((tm, D_MODEL), bf16)],
        compiler_params=_cparams(("parallel", "arbitrary")),
        name="inproj",
    )(x2, attn_g, w_r, gains, rope_c, rope_sa, rope_sb)


def _gelu_tanh(x):
    return 0.5 * x * (1.0 + jnp.tanh(math.sqrt(2.0 / math.pi) * (x + 0.044715 * (x * x * x))))


def _compress_kernel(c_ref, pe_ref, w1_ref, w2_ref, gain_ref, rc_ref, rsa_ref, rsb_ref,
                     o_ref, *, do_norm):
    half = CMP_STRIDE * HEAD_DIM
    c = c_ref[0]
    nc = c.shape[0]
    w1 = w1_ref[...].astype(bf16)
    a = jnp.dot(c, w1[:half], preferred_element_type=f32)
    b = jnp.dot(c, w1[half:], preferred_element_type=f32)
    peb = jnp.dot(pe_ref[...], w1, preferred_element_type=f32)[0:1]
    hid = a + pltpu.roll(b, nc - 1, 0) + peb
    act = _gelu_tanh(hid)
    out = jnp.dot(act.astype(bf16), w2_ref[...].astype(bf16), preferred_element_type=f32)
    if do_norm:
        out = _rope(_rms(out, gain_ref[...]), rc_ref[...], rsa_ref[...], rsb_ref[...])
    o_ref[0] = out.astype(bf16)


def _compress(chunks, pe, w1, w2, gain, rc, rsa, rsb, do_norm):
    ng, nc, width = chunks.shape
    pe8 = jnp.broadcast_to(pe.reshape(1, CMP_LEN * HEAD_DIM), (8, CMP_LEN * HEAD_DIM)).astype(bf16)
    full = lambda shape: pl.BlockSpec(shape, lambda g: (0,) * len(shape))
    return pl.pallas_call(
        functools.partial(_compress_kernel, do_norm=do_norm),
        grid=(ng,),
        in_specs=[
            pl.BlockSpec((1, nc, width), lambda g: (g, 0, 0)),
            full((8, CMP_LEN * HEAD_DIM)),
            full((CMP_LEN * HEAD_DIM, CMP_HIDDEN)),
            full((CMP_HIDDEN, HEAD_DIM)),
            full((1, HEAD_DIM)),
            full((nc, HEAD_DIM)), full((nc, HEAD_DIM)), full((nc, HEAD_DIM)),
        ],
        out_specs=pl.BlockSpec((1, nc, HEAD_DIM), lambda g: (g, 0, 0)),
        out_shape=jax.ShapeDtypeStruct((ng, nc, HEAD_DIM), bf16),
        compiler_params=_cparams(("parallel",)),
        name="compress_k" if do_norm else "compress_v",
    )(chunks, pe8, w1, w2, gain, rc, rsa, rsb)


def _sigmoid(x):
    return 1.0 / (1.0 + jnp.exp(-x))


def _nsa_cmp_kernel(q_ref, kc_ref, vc_ref, g_ref, smt_ref, oc_ref, bias_ref, *, tq):
    q0 = pl.program_id(2) * tq
    R = NSA_GROUP * tq
    q = q_ref[...].reshape(R, HEAD_DIM)
    kc = kc_ref[0]
    vc = vc_ref[0]
    nc = kc.shape[0]
    s = lax.dot_general(q, kc, (((1,), (1,)), ((), ())), preferred_element_type=f32)
    t = q0 + (lax.broadcasted_iota(i32, (R, nc), 0) & (tq - 1))
    cend = lax.broadcasted_iota(i32, (R, nc), 1) * CMP_STRIDE + (CMP_LEN - 1)
    mask = cend <= t
    s = jnp.where(mask, s, NEG_INF)
    m = jnp.max(s, axis=-1, keepdims=True)
    p = jnp.where(mask, jnp.exp(s - m), 0.0)
    p = p / jnp.maximum(jnp.sum(p, axis=-1, keepdims=True), 1e-30)
    o = jnp.dot(p.astype(bf16), vc, preferred_element_type=f32)

    psum = p[0:tq] + p[tq:2 * tq] + p[2 * tq:3 * tq] + p[3 * tq:4 * tq]
    imp = lax.dot_general(smt_ref[...], psum, (((1,), (1,)), ((), ())),
                          precision=lax.Precision.HIGHEST, preferred_element_type=f32)
    nselp = imp.shape[0]
    jf = lax.broadcasted_iota(i32, (nselp, tq), 0).astype(f32)
    cur = ((q0 + lax.broadcasted_iota(i32, (nselp, tq), 1)) >> SEL_SHIFT).astype(f32)
    forced = (jf == 0.0) | (jf == cur) | (jf == cur - 1.0)
    imp = jnp.where(forced, FORCED_SCORE, imp)
    imp = jnp.where(jf <= cur, imp, NEG_INF)
    sel = jnp.zeros((nselp, tq), f32)
    work = imp
    for _ in range(N_SELECTED):
        mx = jnp.max(work, axis=0, keepdims=True)
        first = jnp.min(jnp.where(work == mx, jf, float(nselp)), axis=0, keepdims=True)
        pick = jf == first
        sel = jnp.where(pick & (mx > -1.0), 1.0, sel)
        work = jnp.where(pick, -3e38, work)
    bias_t = jnp.where(sel > 0.5, 0.0, MASK_VAL)
    bias_ref[0] = bias_t.T.astype(bf16)

    sig = _sigmoid(g_ref[0])
    for gi in range(NSA_GROUP):
        col = gi * N_BRANCH
        oc_ref[:, gi * HEAD_DIM:(gi + 1) * HEAD_DIM] = o[gi * tq:(gi + 1) * tq] * sig[:, col:col + 1]


def _nsa_cmp(P, kcmp, vcmp, gate, smt, B, S, tq):
    T = B * S
    nqb = S // tq
    nc = kcmp.shape[1]
    nselp = smt.shape[0]
    return pl.pallas_call(
        functools.partial(_nsa_cmp_kernel, tq=tq),
        grid=(NSA_KV_HEADS, B, nqb),
        in_specs=[
            pl.BlockSpec((NSA_GROUP, tq, HEAD_DIM), lambda h, b, i: (h, b * nqb + i, 0)),
            pl.BlockSpec((1, nc, HEAD_DIM), lambda h, b, i: (h * B + b, 0, 0)),
            pl.BlockSpec((1, nc, HEAD_DIM), lambda h, b, i: (h * B + b, 0, 0)),
            pl.BlockSpec((1, tq, HEAD_DIM), lambda h, b, i: (h, b * nqb + i, 0)),
            pl.BlockSpec((nselp, nc), lambda h, b, i: (0, 0)),
        ],
        out_specs=[
            pl.BlockSpec((tq, NSA_GROUP * HEAD_DIM), lambda h, b, i: (b * nqb + i, h)),
            pl.BlockSpec((1, tq, nselp), lambda h, b, i: (h, b * nqb + i, 0)),
        ],
        out_shape=[
            jax.ShapeDtypeStruct((T, NSA_HEADS * HEAD_DIM), f32),
            jax.ShapeDtypeStruct((NSA_KV_HEADS, T, nselp), bf16),
        ],
        compiler_params=_cparams(("parallel", "parallel", "parallel")),
        name="nsa_cmp",
    )(P, kcmp, vcmp, gate, smt)


def _lane_tile(x, n):
    return x if n == 1 else jnp.concatenate([x] * n, axis=1)


def _softmax_step(s, v_aug, m_ref, acc_ref):
    tk = s.shape[1]
    m_prev = m_ref[...]
    m_new = jnp.maximum(m_prev, jnp.max(s, axis=1, keepdims=True))
    alpha = jnp.exp(m_prev - m_new)
    p = jnp.exp(s - _lane_tile(m_new, tk // LANES))
    acc_ref[...] = _lane_tile(alpha, 2) * acc_ref[...] + jnp.dot(
        p.astype(bf16), v_aug, preferred_element_type=f32)
    m_ref[...] = m_new


def _nsa_sel_kernel(q_ref, bias_ref, ks_ref, vs_ref, kw_ref, vw_ref, g_ref, oc_ref,
                    o_ref, qaug_ref, m_ref, acc_ref, *, tq, tk, tkw):
    q0 = pl.program_id(2) * tq
    R = NSA_GROUP * tq
    nselp = bias_ref.shape[2]
    q = q_ref[...].reshape(R, HEAD_DIM)
    qaug_ref[:, 0:HEAD_DIM] = q
    bias = bias_ref[0]
    for gi in range(NSA_GROUP):
        qaug_ref[gi * tq:(gi + 1) * tq, HEAD_DIM:HEAD_DIM + nselp] = bias
    sig = _sigmoid(g_ref[0])

    def init():
        m_ref[...] = jnp.full(m_ref.shape, M_INIT, f32)
        acc_ref[...] = jnp.zeros(acc_ref.shape, f32)

    def finish():
        return acc_ref[:, 0:HEAD_DIM] / jnp.maximum(acc_ref[:, HEAD_DIM:2 * HEAD_DIM], 1e-30)

    init()
    ones_k = jnp.ones((tk, HEAD_DIM), bf16)

    def sel_step(kt, carry):
        k0 = pl.multiple_of(kt * tk, tk)
        k = ks_ref[0, pl.ds(k0, tk), :]
        blk = (k0 + lax.broadcasted_iota(i32, (tk, nselp), 0)) >> SEL_SHIFT
        onehot = jnp.where(blk == lax.broadcasted_iota(i32, (tk, nselp), 1), 1.0, 0.0).astype(bf16)
        kaug = jnp.concatenate([k, onehot], axis=1)
        s = lax.dot_general(qaug_ref[...], kaug, (((1,), (1,)), ((), ())),
                            preferred_element_type=f32)
        t = q0 + (lax.broadcasted_iota(i32, (R, tk), 0) & (tq - 1))
        kp = k0 + lax.broadcasted_iota(i32, (R, tk), 1)
        s = jnp.where(kp <= t, s, MASK_VAL)
        vaug = jnp.concatenate([vs_ref[0, pl.ds(k0, tk), :], ones_k], axis=1)
        _softmax_step(s, vaug, m_ref, acc_ref)
        return carry

    lax.fori_loop(0, (q0 + tq - 1) // tk + 1, sel_step, 0)
    o_s = finish()

    init()
    ones_w = jnp.ones((tkw, HEAD_DIM), bf16)

    def win_step(kt, carry):
        k0 = pl.multiple_of(kt * tkw, tkw)
        k = kw_ref[0, pl.ds(k0, tkw), :]
        s = lax.dot_general(qaug_ref[:, 0:HEAD_DIM], k, (((1,), (1,)), ((), ())),
                            preferred_element_type=f32)
        t = q0 + (lax.broadcasted_iota(i32, (R, tkw), 0) & (tq - 1))
        kp = k0 + lax.broadcasted_iota(i32, (R, tkw), 1)
        s = jnp.where((kp <= t) & (t - kp < WINDOW), s, MASK_VAL)
        vaug = jnp.concatenate([vw_ref[0, pl.ds(k0, tkw), :], ones_w], axis=1)
        _softmax_step(s, vaug, m_ref, acc_ref)
        return carry

    lo = jnp.maximum(q0 - WINDOW + 1, 0) // tkw
    lax.fori_loop(lo, (q0 + tq - 1) // tkw + 1, win_step, 0)
    o_w = finish()

    for gi in range(NSA_GROUP):
        rows = slice(gi * tq, (gi + 1) * tq)
        cols = slice(gi * HEAD_DIM, (gi + 1) * HEAD_DIM)
        c = gi * N_BRANCH
        o_ref[:, cols] = (oc_ref[:, cols] + o_s[rows] * sig[:, c + 1:c + 2]
                          + o_w[rows] * sig[:, c + 2:c + 3]).astype(bf16)


def _nsa_sel(P, bias, gate, o_c, B, S, tq, tk, tkw):
    T = B * S
    nqb = S // tq
    nselp = bias.shape[2]
    R = NSA_GROUP * tq
    kv_spec = lambda base: pl.BlockSpec((1, S, HEAD_DIM), lambda h, b, i: (base + h, b, 0))
    return pl.pallas_call(
        functools.partial(_nsa_sel_kernel, tq=tq, tk=tk, tkw=tkw),
        grid=(NSA_KV_HEADS, B, nqb),
        in_specs=[
            pl.BlockSpec((NSA_GROUP, tq, HEAD_DIM), lambda h, b, i: (h, b * nqb + i, 0)),
            pl.BlockSpec((1, tq, nselp), lambda h, b, i: (h, b * nqb + i, 0)),
            kv_spec(HD_KS), kv_spec(HD_VS), kv_spec(HD_KW), kv_spec(HD_VW),
            pl.BlockSpec((1, tq, HEAD_DIM), lambda h, b, i: (h, b * nqb + i, 0)),
            pl.BlockSpec((tq, NSA_GROUP * HEAD_DIM), lambda h, b, i: (b * nqb + i, h)),
        ],
        out_specs=pl.BlockSpec((tq, NSA_GROUP * HEAD_DIM), lambda h, b, i: (b * nqb + i, h)),
        out_shape=jax.ShapeDtypeStruct((T, NSA_HEADS * HEAD_DIM), bf16),
        scratch_shapes=[
            pltpu.VMEM((R, HEAD_DIM + nselp), bf16),
            pltpu.VMEM((R, LANES), f32),
            pltpu.VMEM((R, 2 * HEAD_DIM), f32),
        ],
        compiler_params=_cparams(("parallel", "parallel", "arbitrary")),
        name="nsa_sel_win",
    )(P, bias, P, P, P, P, gate, o_c)


def _diff_kernel(q_ref, k_ref, v_ref, lam_ref, sg_ref, o_ref, m_ref, l_ref, acc_ref,
                 *, tq, tk, lambda_init):
    q0 = pl.program_id(2) * tq
    R = 2 * tq
    q1 = q_ref[0]
    q2 = q_ref[1]
    m_ref[...] = jnp.full(m_ref.shape, M_INIT, f32)
    l_ref[...] = jnp.zeros(l_ref.shape, f32)
    acc_ref[...] = jnp.zeros(acc_ref.shape, f32)

    def step(kt, carry):
        k0 = pl.multiple_of(kt * tk, tk)
        nt = (((1,), (1,)), ((), ()))
        s1 = lax.dot_general(q1, k_ref[0, pl.ds(k0, tk), :], nt, preferred_element_type=f32)
        s2 = lax.dot_general(q2, k_ref[1, pl.ds(k0, tk), :], nt, preferred_element_type=f32)
        s = jnp.concatenate([s1, s2], axis=0)
        t = q0 + (lax.broadcasted_iota(i32, (R, tk), 0) & (tq - 1))
        kp = k0 + lax.broadcasted_iota(i32, (R, tk), 1)
        s = jnp.where(kp <= t, s, MASK_VAL)
        v = jnp.concatenate([v_ref[0, pl.ds(k0, tk), :], v_ref[1, pl.ds(k0, tk), :]], axis=1)
        m_prev = m_ref[...]
        m_new = jnp.maximum(m_prev, jnp.max(s, axis=1, keepdims=True))
        alpha = jnp.exp(m_prev - m_new)
        p = jnp.exp(s - _lane_tile(m_new, tk // LANES))
        l_ref[...] = alpha * l_ref[...] + jnp.sum(p, axis=1, keepdims=True)
        acc_ref[...] = _lane_tile(alpha, 2) * acc_ref[...] + jnp.dot(
            p.astype(bf16), v, preferred_element_type=f32)
        m_ref[...] = m_new
        return carry

    lax.fori_loop(0, (q0 + tq - 1) // tk + 1, step, 0)

    lq = lam_ref[...]
    lam = (jnp.exp(jnp.sum(lq[0:1] * lq[1:2], axis=1, keepdims=True))
           - jnp.exp(jnp.sum(lq[2:3] * lq[3:4], axis=1, keepdims=True)) + lambda_init)
    o = acc_ref[...] / _lane_tile(jnp.maximum(l_ref[...], 1e-30), 2)
    a = o[0:tq] - lam * o[tq:R]
    o_ref[...] = (_rms(a, sg_ref[...]) * (1.0 - lambda_init)).astype(bf16)


def _diff_attn(P, lam_vecs, subln_g, B, S, tq, tk, lambda_init):
    T = B * S
    nqb = S // tq
    return pl.pallas_call(
        functools.partial(_diff_kernel, tq=tq, tk=tk, lambda_init=lambda_init),
        grid=(DIFF_HEADS, B, nqb),
        in_specs=[
            pl.BlockSpec((2, tq, HEAD_DIM), lambda h, b, i: (HD_QD // 2 + h, b * nqb + i, 0)),
            pl.BlockSpec((2, S, HEAD_DIM), lambda h, b, i: (HD_KD // 2 + h, b, 0)),
            pl.BlockSpec((2, S, HEAD_DIM), lambda h, b, i: (HD_VD // 2 + h, b, 0)),
            pl.BlockSpec((8, HEAD_DIM), lambda h, b, i: (0, 0)),
            pl.BlockSpec((1, DIFF_V_DIM), lambda h, b, i: (0, 0)),
        ],
        out_specs=pl.BlockSpec((tq, DIFF_V_DIM), lambda h, b, i: (b * nqb + i, h)),
        out_shape=jax.ShapeDtypeStruct((T, DIFF_HEADS * DIFF_V_DIM), bf16),
        scratch_shapes=[
            pltpu.VMEM((2 * tq, LANES), f32),
            pltpu.VMEM((2 * tq, LANES), f32),
            pltpu.VMEM((2 * tq, DIFF_V_DIM), f32),
        ],
        compiler_params=_cparams(("parallel", "parallel", "arbitrary")),
        name="diff_attn",
    )(P, P, P, lam_vecs, subln_g)


def _outproj_kernel(x_ref, on_ref, od_ref, w_ref, g_ref, wr_ref, br_ref,
                    h_ref, hn_ref, lg_ref):
    half = NSA_HEADS * HEAD_DIM
    h = x_ref[...] + jnp.dot(on_ref[...], w_ref[0:half], preferred_element_type=f32) \
        + jnp.dot(od_ref[...], w_ref[half:], preferred_element_type=f32)
    h_ref[...] = h
    hn = _rms(h, g_ref[...])
    hn_ref[...] = hn
    lg_ref[...] = jnp.dot(hn, wr_ref[...], precision=lax.Precision.HIGHEST,
                          preferred_element_type=f32) + br_ref[...]


def _outproj(x2, o_nsa, o_diff, w_out_b, ffn_g, w_router_p, b_router_p, tm):
    T = x2.shape[0]
    full = lambda shape: pl.BlockSpec(shape, lambda i: (0,) * len(shape))
    return pl.pallas_call(
        _outproj_kernel,
        grid=(T // tm,),
        in_specs=[
            pl.BlockSpec((tm, D_MODEL), lambda i: (i, 0)),
            pl.BlockSpec((tm, NSA_HEADS * HEAD_DIM), lambda i: (i, 0)),
            pl.BlockSpec((tm, DIFF_HEADS * DIFF_V_DIM), lambda i: (i, 0)),
            full((D_MODEL, D_MODEL)),
            full((1, D_MODEL)),
            full((D_MODEL, LANES)),
            full((1, LANES)),
        ],
        out_specs=[
            pl.BlockSpec((tm, D_MODEL), lambda i: (i, 0)),
            pl.BlockSpec((tm, D_MODEL), lambda i: (i, 0)),
            pl.BlockSpec((tm, LANES), lambda i: (i, 0)),
        ],
        out_shape=[
            jax.ShapeDtypeStruct((T, D_MODEL), f32),
            jax.ShapeDtypeStruct((T, D_MODEL), f32),
            jax.ShapeDtypeStruct((T, LANES), f32),
        ],
        compiler_params=_cparams(("parallel",)),
        name="outproj_router",
    )(x2, o_nsa, o_diff, w_out_b, ffn_g, w_router_p, b_router_p)


ISSUE_UNROLL = 8


def _start_row_gathers(idx_fn, n, src_ref, dst_ref, sem):
    def body(c, carry):
        for u in range(ISSUE_UNROLL):
            r = c * ISSUE_UNROLL + u
            pltpu.make_async_copy(src_ref.at[pl.ds(idx_fn(r), 1)], dst_ref.at[pl.ds(r, 1)], sem).start()
        return carry

    lax.fori_loop(0, n // ISSUE_UNROLL, body, 0)


def _wait_row_gathers(n, src_ref, dst_ref, sem):
    for r in range(n):
        pltpu.make_async_copy(src_ref.at[pl.ds(0, 1)], dst_ref.at[pl.ds(r, 1)], sem).wait()


def _gather_kernel(st_ref, sb_ref, src_ref, o_ref, buf_ref, sem, *, tm, n_src):
    b = pl.program_id(0)
    nb = pl.num_programs(0)

    def issue(blk, slot):
        base = sb_ref[blk]
        _start_row_gathers(lambda r: st_ref[jnp.minimum(base + r, n_src - 1)], tm,
                           src_ref, buf_ref.at[slot], sem.at[slot])

    @pl.when(b == 0)
    def _():
        issue(0, 0)

    @pl.when(b + 1 < nb)
    def _():
        issue(b + 1, (b + 1) % 2)

    slot = b % 2
    _wait_row_gathers(tm, src_ref, buf_ref.at[slot], sem.at[slot])
    o_ref[...] = buf_ref[slot].astype(o_ref.dtype)


def _gather_rows(sorted_tok, src_base, src, n_blocks, tm):
    return pl.pallas_call(
        functools.partial(_gather_kernel, tm=tm, n_src=sorted_tok.shape[0]),
        grid_spec=pltpu.PrefetchScalarGridSpec(
            num_scalar_prefetch=2,
            grid=(n_blocks,),
            in_specs=[pl.BlockSpec(memory_space=pl.ANY)],
            out_specs=pl.BlockSpec((tm, src.shape[1]), lambda b, st, sb: (b, 0)),
            scratch_shapes=[pltpu.VMEM((2, tm, src.shape[1]), src.dtype), pltpu.SemaphoreType.DMA((2,))],
        ),
        out_shape=jax.ShapeDtypeStruct((n_blocks * tm, src.shape[1]), bf16),
        compiler_params=_cparams(("arbitrary",)),
        name="moe_gather",
    )(sorted_tok, src_base, src)


def _expert_row_blocks(first, nblk, x_hbm, o_dst, xbuf, obuf, xsem, osem, tm, compute):
    def x_copy(blk, slot):
        return pltpu.make_async_copy(x_hbm.at[pl.ds(pl.multiple_of(blk * tm, tm), tm)],
                                     xbuf.at[slot], xsem.at[slot])

    def o_copy(blk, slot):
        return pltpu.make_async_copy(obuf.at[slot], o_dst(pl.multiple_of(blk * tm, tm)), osem.at[slot])

    @pl.when(nblk > 0)
    def _():
        x_copy(first, 0).start()

    def body(i, carry):
        slot = i % 2

        @pl.when(i + 1 < nblk)
        def _():
            x_copy(first + i + 1, 1 - slot).start()

        x_copy(first + i, slot).wait()

        @pl.when(i >= 2)
        def _():
            o_copy(first + i - 2, slot).wait()

        obuf[slot] = compute(xbuf[slot])
        o_copy(first + i, slot).start()
        return carry

    lax.fori_loop(0, nblk, body, 0)

    @pl.when(nblk >= 2)
    def _():
        o_copy(first + nblk - 2, nblk % 2).wait()

    @pl.when(nblk >= 1)
    def _():
        o_copy(first + nblk - 1, (nblk - 1) % 2).wait()

    return o_copy


def _zero_unused_blocks(o_copy, obuf, lo, hi):
    obuf[0] = jnp.zeros(obuf.shape[1:], obuf.dtype)

    def body(blk, carry):
        cp = o_copy(blk, 0)
        cp.start()
        cp.wait()
        return carry

    lax.fori_loop(lo, hi, body, 0)


def _gateup_kernel(fb_ref, nbk_ref, nu_ref, xs_hbm, wg_ref, wu_ref, bg_ref, bu_ref, hb_hbm,
                   wgb_ref, wub_ref, xbuf, obuf, xsem, osem, *, tm, tn, n_blocks):
    j = pl.program_id(0)
    e = pl.program_id(1)
    wgb_ref[...] = wg_ref[0].astype(bf16)
    wub_ref[...] = wu_ref[0].astype(bf16)
    col0 = pl.multiple_of(j * tn, tn)

    def compute(x):
        gl = jnp.dot(x, wgb_ref[...], preferred_element_type=f32) + bg_ref[0]
        up = jnp.dot(x, wub_ref[...], preferred_element_type=f32) + bu_ref[0]
        gl = jnp.minimum(gl, SWIGLU_LIMIT)
        up = jnp.clip(up, -SWIGLU_LIMIT, SWIGLU_LIMIT)
        return ((up + 1.0) * (gl * _sigmoid(SWIGLU_ALPHA * gl))).astype(bf16)

    o_copy = _expert_row_blocks(fb_ref[e], nbk_ref[e], xs_hbm,
                                lambda r0: hb_hbm.at[pl.ds(r0, tm), pl.ds(col0, tn)],
                                xbuf, obuf, xsem, osem, tm, compute)

    @pl.when(e == N_EXPERTS - 1)
    def _():
        _zero_unused_blocks(o_copy, obuf, nu_ref[0], n_blocks)


def _gateup(first_blk, n_blk, n_used, xs, w_gate_up, b_gate_up3, tm, tn):
    n_rows = xs.shape[0]
    nj = D_FF // tn
    return pl.pallas_call(
        functools.partial(_gateup_kernel, tm=tm, tn=tn, n_blocks=n_rows // tm),
        grid_spec=pltpu.PrefetchScalarGridSpec(
            num_scalar_prefetch=3,
            grid=(nj, N_EXPERTS),
            in_specs=[
                pl.BlockSpec(memory_space=pl.ANY),
                pl.BlockSpec((1, D_MODEL, tn), lambda j, e, *_: (e, 0, j)),
                pl.BlockSpec((1, D_MODEL, tn), lambda j, e, *_: (e, 0, nj + j)),
                pl.BlockSpec((1, 1, tn), lambda j, e, *_: (e, 0, j)),
                pl.BlockSpec((1, 1, tn), lambda j, e, *_: (e, 0, nj + j)),
            ],
            out_specs=pl.BlockSpec(memory_space=pl.ANY),
            scratch_shapes=[
                pltpu.VMEM((D_MODEL, tn), bf16), pltpu.VMEM((D_MODEL, tn), bf16),
                pltpu.VMEM((2, tm, D_MODEL), bf16), pltpu.VMEM((2, tm, tn), bf16),
                pltpu.SemaphoreType.DMA((2,)), pltpu.SemaphoreType.DMA((2,)),
            ],
        ),
        out_shape=jax.ShapeDtypeStruct((n_rows, D_FF), bf16),
        compiler_params=_cparams(("arbitrary", "arbitrary")),
        name="moe_gate_up",
    )(first_blk, n_blk, n_used, xs, w_gate_up, w_gate_up, b_gate_up3, b_gate_up3)


def _down_kernel(fb_ref, nbk_ref, nu_ref, hb_hbm, w_ref, b_ref, y_hbm,
                 wb_ref, xbuf, obuf, xsem, osem, *, tm, n_blocks):
    e = pl.program_id(0)
    wb_ref[...] = w_ref[0].astype(bf16)

    def compute(x):
        return jnp.dot(x, wb_ref[...], preferred_element_type=f32) + b_ref[0]

    o_copy = _expert_row_blocks(fb_ref[e], nbk_ref[e], hb_hbm, lambda r0: y_hbm.at[pl.ds(r0, tm)],
                                xbuf, obuf, xsem, osem, tm, compute)

    @pl.when(e == N_EXPERTS - 1)
    def _():
        _zero_unused_blocks(o_copy, obuf, nu_ref[0], n_blocks)


def _down(first_blk, n_blk, n_used, hb, w_down, b_down3, tm):
    n_rows = hb.shape[0]
    return pl.pallas_call(
        functools.partial(_down_kernel, tm=tm, n_blocks=n_rows // tm),
        grid_spec=pltpu.PrefetchScalarGridSpec(
            num_scalar_prefetch=3,
            grid=(N_EXPERTS,),
            in_specs=[
                pl.BlockSpec(memory_space=pl.ANY),
                pl.BlockSpec((1, D_FF, D_MODEL), lambda e, *_: (e, 0, 0)),
                pl.BlockSpec((1, 1, D_MODEL), lambda e, *_: (e, 0, 0)),
            ],
            out_specs=pl.BlockSpec(memory_space=pl.ANY),
            scratch_shapes=[
                pltpu.VMEM((D_FF, D_MODEL), bf16),
                pltpu.VMEM((2, tm, D_FF), bf16), pltpu.VMEM((2, tm, D_MODEL), f32),
                pltpu.SemaphoreType.DMA((2,)), pltpu.SemaphoreType.DMA((2,)),
            ],
        ),
        out_shape=jax.ShapeDtypeStruct((n_rows, D_MODEL), f32),
        compiler_params=_cparams(("arbitrary",)),
        name="moe_down",
    )(first_blk, n_blk, n_used, hb, w_down, b_down3)


def _combine_kernel(pos_ref, h_ref, gt_ref, y_ref, o_ref, buf_ref, sem, *, tc):
    i = pl.program_id(0)
    n = pl.num_programs(0)
    rows = TOP_K * tc

    def issue(step, slot):
        base = step * rows
        _start_row_gathers(lambda r: pos_ref[base + r], rows, y_ref, buf_ref.at[slot], sem.at[slot])

    @pl.when(i == 0)
    def _():
        issue(0, 0)

    @pl.when(i + 1 < n)
    def _():
        issue(i + 1, (i + 1) % 2)

    slot = i % 2
    _wait_row_gathers(rows, y_ref, buf_ref.at[slot], sem.at[slot])
    gt = gt_ref[...]
    out = h_ref[...]
    for k in range(TOP_K):
        out = out + gt[:, k:k + 1] * buf_ref[slot, k * tc:(k + 1) * tc, :]
    o_ref[...] = out


def _combine(pos_blocks, h, gate_pad, y, tc):
    T = h.shape[0]
    return pl.pallas_call(
        functools.partial(_combine_kernel, tc=tc),
        grid_spec=pltpu.PrefetchScalarGridSpec(
            num_scalar_prefetch=1,
            grid=(T // tc,),
            in_specs=[
                pl.BlockSpec((tc, D_MODEL), lambda i, pos: (i, 0)),
                pl.BlockSpec((tc, LANES), lambda i, pos: (i, 0)),
                pl.BlockSpec(memory_space=pl.ANY),
            ],
            out_specs=pl.BlockSpec((tc, D_MODEL), lambda i, pos: (i, 0)),
            scratch_shapes=[pltpu.VMEM((2, TOP_K * tc, D_MODEL), f32), pltpu.SemaphoreType.DMA((2,))],
        ),
        out_shape=jax.ShapeDtypeStruct((T, D_MODEL), f32),
        compiler_params=_cparams(("arbitrary",)),
        name="moe_combine",
    )(pos_blocks, h, gate_pad, y)


def _rearranged_w_in(w_in):
    splits = np.cumsum([0, 1024, 256, 256, 256, 256, 256, 256, 24, 1024, 1024, 1024])
    q, kc, vc, ks, vs, kw, vw, g, qd, kd, vd = [w_in[:, splits[n]:splits[n + 1]] for n in range(11)]
    gcols = NSA_GROUP * N_BRANCH
    zpad = lambda n: jnp.zeros((D_MODEL, n), w_in.dtype)
    parts = [q, ks, kw, qd, kd, kc, vc, vs, vw, vd,
             g[:, :gcols], zpad(HEAD_DIM - gcols), g[:, gcols:], zpad(HEAD_DIM - gcols),
             zpad((N_HEADS_PAD - HD_GATE - 2) * HEAD_DIM)]
    return jnp.concatenate(parts, axis=1).astype(bf16)


def _sel_map_t(S, nselp):
    n_chunks = S // CMP_STRIDE
    n_cmp = (S - CMP_LEN) // CMP_STRIDE + 1
    n_sel = S // SEL_BLOCK
    c_start = np.arange(n_chunks) * CMP_STRIDE
    s_start = np.arange(nselp) * SEL_BLOCK
    ov = (c_start[None, :] < s_start[:, None] + SEL_BLOCK) & (c_start[None, :] + CMP_LEN > s_start[:, None])
    ov &= (np.arange(n_chunks)[None, :] < n_cmp) & (np.arange(nselp)[:, None] < n_sel)
    return jnp.asarray(ov, f32)


def _attention_block(x2, B, S, attn_norm_g, w_in, nsa_q_norm_g, nsa_k_norm_g, pos_k, pos_v,
                     k_w1, k_w2, v_w1, v_w2, diff_q_norm_g, diff_k_norm_g, lq1, lk1, lq2, lk2,
                     subln_g, lambda_init):
    T = B * S
    scale = HEAD_DIM ** -0.5
    tm = min(1024, S)
    rope_c, rope_sa, rope_sb = _rope_tables(np.arange(S))
    gains = jnp.concatenate([
        jnp.broadcast_to(nsa_q_norm_g * scale, (NSA_HEADS, HEAD_DIM)),
        jnp.broadcast_to(nsa_k_norm_g[1], (NSA_KV_HEADS, HEAD_DIM)),
        jnp.broadcast_to(nsa_k_norm_g[2], (NSA_KV_HEADS, HEAD_DIM)),
        jnp.broadcast_to(diff_q_norm_g * scale, (2 * DIFF_HEADS, HEAD_DIM)),
        jnp.broadcast_to(diff_k_norm_g, (2 * DIFF_HEADS, HEAD_DIM)),
    ], axis=0).reshape(N_NORM_HEADS, 1, HEAD_DIM)
    P, gate = _inproj(x2, attn_norm_g.reshape(1, D_MODEL), _rearranged_w_in(w_in), gains,
                      rope_c, rope_sa, rope_sb, S, tm)

    n_chunks = S // CMP_STRIDE
    cmp_pos = np.arange(n_chunks) * CMP_STRIDE + (CMP_LEN - 1)
    cc, csa, csb = _rope_tables(cmp_pos)
    chunk_view = lambda hd: P[hd:hd + NSA_KV_HEADS].reshape(NSA_KV_HEADS * B, n_chunks, CMP_STRIDE * HEAD_DIM)
    kcmp = _compress(chunk_view(HD_KC), pos_k, k_w1, k_w2, nsa_k_norm_g[0].reshape(1, HEAD_DIM),
                     cc, csa, csb, True)
    vcmp = _compress(chunk_view(HD_VC), pos_v, v_w1, v_w2, nsa_k_norm_g[0].reshape(1, HEAD_DIM),
                     cc, csa, csb, False)

    nselp = -(-(S // SEL_BLOCK) // LANES) * LANES
    tq = min(256, S)
    o_c, bias = _nsa_cmp(P, kcmp, vcmp, gate, _sel_map_t(S, nselp), B, S, tq)
    o_nsa = _nsa_sel(P, bias, gate, o_c, B, S, tq, min(512, S), min(256, S))

    lam_vecs = jnp.concatenate([lq1[None], lk1[None], lq2[None], lk2[None],
                                jnp.zeros((4, HEAD_DIM), f32)], axis=0)
    o_diff = _diff_attn(P, lam_vecs, subln_g.reshape(1, DIFF_V_DIM), B, S,
                        min(512, S), min(512, S), lambda_init)
    return o_nsa, o_diff


def _moe_block(h, hn, logits, w_gate_up, b_gate_up, w_down, b_down, tm=256):
    T = h.shape[0]
    TK = T * TOP_K
    top_v, top_e = lax.top_k(logits[:, :N_EXPERTS], TOP_K)
    gate = jax.nn.softmax(top_v, axis=-1)
    flat_e = top_e.reshape(TK).astype(i32)
    se, order = lax.sort((flat_e, jnp.arange(TK, dtype=i32)), num_keys=1, is_stable=True)
    st = order // TOP_K
    experts = jnp.arange(N_EXPERTS, dtype=i32)
    counts = jnp.sum((flat_e[:, None] == experts[None, :]).astype(i32), axis=0)
    padded = (counts + tm - 1) // tm * tm
    pad_end = jnp.cumsum(padded)
    pad_start = pad_end - padded
    cnt_start = jnp.cumsum(counts) - counts
    dest = pad_start[se] + (jnp.arange(TK, dtype=i32) - cnt_start[se])
    _, pos = lax.sort((order, dest), num_keys=1)
    n_blocks = -(-TK // tm) + N_EXPERTS
    blk_start = jnp.arange(n_blocks, dtype=i32) * tm
    blk_e = jnp.minimum(jnp.sum((pad_end[None, :] <= blk_start[:, None]).astype(i32), axis=1), N_EXPERTS - 1)
    src_base = cnt_start[blk_e] + (blk_start - pad_start[blk_e])
    n_used = (pad_end[-1] // tm).astype(i32).reshape(1)

    xs = _gather_rows(st, src_base.astype(i32), hn, n_blocks, tm)
    first_blk = (pad_start // tm).astype(i32)
    n_blk = (padded // tm).astype(i32)
    hb = _gateup(first_blk, n_blk, n_used, xs, w_gate_up, b_gate_up.reshape(N_EXPERTS, 1, 2 * D_FF), tm, 1024)
    y = _down(first_blk, n_blk, n_used, hb, w_down, b_down.reshape(N_EXPERTS, 1, D_MODEL), tm)

    tc = 64
    pos_blocks = pos.reshape(T // tc, tc, TOP_K).transpose(0, 2, 1).reshape(TK)
    gate_pad = jnp.pad(gate, ((0, 0), (0, LANES - TOP_K)))
    return _combine(pos_blocks, h, gate_pad, y, tc)


def kernel(x, attn_norm_g, w_in, nsa_q_norm_g, nsa_k_norm_g, nsa_cmp_pos_k, nsa_cmp_pos_v, nsa_cmp_k_w1, nsa_cmp_k_w2, nsa_cmp_v_w1, nsa_cmp_v_w2, diff_q_norm_g, diff_k_norm_g, diff_lambda_q1, diff_lambda_k1, diff_lambda_q2, diff_lambda_k2, diff_subln_g, w_out, ffn_norm_g, w_router, b_router, w_gate_up, b_gate_up, w_down, b_down):
    B, S, _ = x.shape
    T = B * S
    depth = attn_norm_g.shape[0]
    h = x.reshape(T, D_MODEL)
    for l in range(depth):
        lambda_init = 0.8 - 0.6 * math.exp(-0.3 * l)
        o_nsa, o_diff = _attention_block(
            h, B, S, attn_norm_g[l], w_in[l], nsa_q_norm_g[l], nsa_k_norm_g[l], nsa_cmp_pos_k[l],
            nsa_cmp_pos_v[l], nsa_cmp_k_w1[l], nsa_cmp_k_w2[l], nsa_cmp_v_w1[l], nsa_cmp_v_w2[l],
            diff_q_norm_g[l], diff_k_norm_g[l], diff_lambda_q1[l], diff_lambda_k1[l],
            diff_lambda_q2[l], diff_lambda_k2[l], diff_subln_g[l], lambda_init)
        w_router_p = jnp.pad(w_router[l], ((0, 0), (0, LANES - N_EXPERTS)))
        b_router_p = jnp.pad(b_router[l], (0, LANES - N_EXPERTS)).reshape(1, LANES)
        h_mid, hn, logits = _outproj(h, o_nsa, o_diff, w_out[l].astype(bf16),
                                     ffn_norm_g[l].reshape(1, D_MODEL), w_router_p, b_router_p,
                                     min(512, T))
        h = _moe_block(h_mid, hn, logits, w_gate_up[l], b_gate_up[l], w_down[l], b_down[l])
    return h.reshape(B, S, D_MODEL)
```

```python
import functools
import math

import numpy as np
import jax
import jax.numpy as jnp
from jax import lax
from jax.experimental import pallas as pl
from jax.experimental.pallas import tpu as pltpu

f32 = jnp.float32
bf16 = jnp.bfloat16
i32 = jnp.int32

D_MODEL = 2048
HEAD_DIM = 128
ROT_DIM = HEAD_DIM // 4
ROPE_THETA = 500000.0
NORM_EPS = 1e-6
NEG_INF = -1e30

NSA_HEADS = 8
NSA_KV_HEADS = 2
NSA_GROUP = NSA_HEADS // NSA_KV_HEADS
N_BRANCH = 3
CMP_LEN = 32
CMP_STRIDE = 16
CMP_HIDDEN = 2 * HEAD_DIM
SEL_BLOCK = 64
SEL_SHIFT = 6
N_SELECTED = 16
WINDOW = 512
FORCED_SCORE = 1e9

DIFF_HEADS = 4
DIFF_V_DIM = 2 * HEAD_DIM

N_EXPERTS = 32
TOP_K = 4
D_FF = D_MODEL
SWIGLU_ALPHA = 1.702
SWIGLU_LIMIT = 7.0

LANES = 128

HD_Q = 0
HD_KS = 8
HD_KW = 10
HD_QD = 12
HD_KD = 20
N_NORM_HEADS = 28
HR_KC = 0
HR_VC = 2
HR_VS = 4
HR_VW = 6
HR_VD = 8
HR_GATE = 16
N_RAW_HEADS = 20
HP = 4

MASK_VAL = -1e30
M_INIT = -5e29

VMEM_LIMIT = 56 * 1024 * 1024


def _cparams(sem):
    return pltpu.CompilerParams(dimension_semantics=sem, vmem_limit_bytes=VMEM_LIMIT)


def _rope_tables(pos):
    inv = np.power(ROPE_THETA, -np.arange(0, ROT_DIM, 2, dtype=np.float64) / ROT_DIM)
    ang = pos.astype(np.float64)[:, None] * inv[None, :]
    cos, sin = np.cos(ang), np.sin(ang)
    n = pos.shape[0]
    half = ROT_DIM // 2
    c = np.concatenate([cos, cos, np.ones((n, HEAD_DIM - ROT_DIM))], axis=1)
    sa = np.concatenate([-sin, np.zeros((n, HEAD_DIM - half))], axis=1)
    sb = np.concatenate([np.zeros((n, half)), sin, np.zeros((n, HEAD_DIM - ROT_DIM))], axis=1)
    return (jnp.asarray(c, f32), jnp.asarray(sa, f32), jnp.asarray(sb, f32))


def _rope(y, c, sa, sb):
    half = ROT_DIM // 2
    return (y * c + pltpu.roll(y, HEAD_DIM - half, 1) * sa + pltpu.roll(y, half, 1) * sb)


def _rms(y, gain):
    ms = jnp.mean(y * y, axis=-1, keepdims=True)
    return y * lax.rsqrt(ms + NORM_EPS) * gain


def _inproj_norm_kernel(x_ref, g_ref, w_ref, gain_ref, c_ref, sa_ref, sb_ref, o_ref, xn_ref):
    @pl.when(pl.program_id(1) == 0)
    def _():
        xn_ref[...] = _rms(x_ref[...], g_ref[...]).astype(bf16)

    tm = x_ref.shape[0]
    rc = min(ROW_CHUNK, tm)
    for r in range(tm // rc):
        rows = slice(r * rc, (r + 1) * rc)
        acc = jnp.dot(xn_ref[rows, :], w_ref[...], preferred_element_type=f32)
        c, sa, sb = c_ref[rows, :], sa_ref[rows, :], sb_ref[rows, :]
        for u in range(HP):
            y = _rms(acc[:, u * HEAD_DIM:(u + 1) * HEAD_DIM], gain_ref[u])
            o_ref[u, rows, :] = _rope(y, c, sa, sb).astype(bf16)


def _inproj_raw_kernel(x_ref, g_ref, w_ref, o_ref, gate_ref, xn_ref):
    j = pl.program_id(1)

    @pl.when(j == 0)
    def _():
        xn_ref[...] = _rms(x_ref[...], g_ref[...]).astype(bf16)

    tm = x_ref.shape[0]
    rc = min(ROW_CHUNK, tm)
    for r in range(tm // rc):
        rows = slice(r * rc, (r + 1) * rc)
        acc = jnp.dot(xn_ref[rows, :], w_ref[...], preferred_element_type=f32)
        for u in range(HP):
            o_ref[u, rows, :] = acc[:, u * HEAD_DIM:(u + 1) * HEAD_DIM].astype(bf16)

    @pl.when(j == HR_GATE // HP)
    def _():
        gate_ref[0] = jnp.dot(xn_ref[...], w_ref[:, 0:HEAD_DIM], preferred_element_type=f32)
        gate_ref[1] = jnp.dot(xn_ref[...], w_ref[:, HEAD_DIM:2 * HEAD_DIM], preferred_element_type=f32)


def _inproj(x2, attn_g, w_norm, w_raw, gains, rope_c, rope_sa, rope_sb, S, tm):
    T = x2.shape[0]
    nsb = S // tm
    x_specs = [
        pl.BlockSpec((tm, D_MODEL), lambda i, j: (i, 0)),
        pl.BlockSpec((1, D_MODEL), lambda i, j: (0, 0)),
        pl.BlockSpec((D_MODEL, HP * HEAD_DIM), lambda i, j: (0, j)),
    ]
    head_spec = pl.BlockSpec((HP, tm, HEAD_DIM), lambda i, j: (j, i, 0))
    pos_spec = pl.BlockSpec((tm, HEAD_DIM), lambda i, j: (i % nsb, 0))
    Pn = pl.pallas_call(
        _inproj_norm_kernel,
        grid=(T // tm, N_NORM_HEADS // HP),
        in_specs=x_specs + [pl.BlockSpec((HP, 1, HEAD_DIM), lambda i, j: (j, 0, 0)),
                            pos_spec, pos_spec, pos_spec],
        out_specs=head_spec,
        out_shape=jax.ShapeDtypeStruct((N_NORM_HEADS, T, HEAD_DIM), bf16),
        scratch_shapes=[pltpu.VMEM((tm, D_MODEL), bf16)],
        compiler_params=_cparams(("parallel", "arbitrary")),
        name="inproj_norm",
    )(x2, attn_g, w_norm, gains, rope_c, rope_sa, rope_sb)
    Pr, gate = pl.pallas_call(
        _inproj_raw_kernel,
        grid=(T // tm, N_RAW_HEADS // HP),
        in_specs=x_specs,
        out_specs=[head_spec, pl.BlockSpec((2, tm, HEAD_DIM), lambda i, j: (0, i, 0))],
        out_shape=[
            jax.ShapeDtypeStruct((N_RAW_HEADS, T, HEAD_DIM), bf16),
            jax.ShapeDtypeStruct((2, T, HEAD_DIM), f32),
        ],
        scratch_shapes=[pltpu.VMEM((tm, D_MODEL), bf16)],
        compiler_params=_cparams(("parallel", "arbitrary")),
        name="inproj_raw",
    )(x2, attn_g, w_raw)
    return Pn, Pr, gate


def _gelu_tanh(x):
    return 0.5 * x * (1.0 + jnp.tanh(math.sqrt(2.0 / math.pi) * (x + 0.044715 * (x * x * x))))


def _compress_kernel(c_ref, pe_ref, w1_ref, w2_ref, gain_ref, rc_ref, rsa_ref, rsb_ref,
                     o_ref, *, do_norm):
    half = CMP_STRIDE * HEAD_DIM
    c = c_ref[0]
    nc = c.shape[0]
    w1 = w1_ref[...].astype(bf16)
    a = jnp.dot(c, w1[:half], preferred_element_type=f32)
    b = jnp.dot(c, w1[half:], preferred_element_type=f32)
    peb = jnp.dot(pe_ref[...], w1, preferred_element_type=f32)[0:1]
    hid = a + pltpu.roll(b, nc - 1, 0) + peb
    act = _gelu_tanh(hid)
    out = jnp.dot(act.astype(bf16), w2_ref[...].astype(bf16), preferred_element_type=f32)
    if do_norm:
        out = _rope(_rms(out, gain_ref[...]), rc_ref[...], rsa_ref[...], rsb_ref[...])
    o_ref[0] = out.astype(bf16)


def _compress(chunks, pe, w1, w2, gain, rc, rsa, rsb, do_norm):
    ng, nc, width = chunks.shape
    pe8 = jnp.broadcast_to(pe.reshape(1, CMP_LEN * HEAD_DIM), (8, CMP_LEN * HEAD_DIM)).astype(bf16)
    full = lambda shape: pl.BlockSpec(shape, lambda g: (0,) * len(shape))
    return pl.pallas_call(
        functools.partial(_compress_kernel, do_norm=do_norm),
        grid=(ng,),
        in_specs=[
            pl.BlockSpec((1, nc, width), lambda g: (g, 0, 0)),
            full((8, CMP_LEN * HEAD_DIM)),
            full((CMP_LEN * HEAD_DIM, CMP_HIDDEN)),
            full((CMP_HIDDEN, HEAD_DIM)),
            full((1, HEAD_DIM)),
            full((nc, HEAD_DIM)), full((nc, HEAD_DIM)), full((nc, HEAD_DIM)),
        ],
        out_specs=pl.BlockSpec((1, nc, HEAD_DIM), lambda g: (g, 0, 0)),
        out_shape=jax.ShapeDtypeStruct((ng, nc, HEAD_DIM), bf16),
        compiler_params=_cparams(("parallel",)),
        name="compress_k" if do_norm else "compress_v",
    )(chunks, pe8, w1, w2, gain, rc, rsa, rsb)


def _sigmoid(x):
    return 1.0 / (1.0 + jnp.exp(-x))


def _nsa_cmp_kernel(q_ref, kc_ref, vc_ref, g_ref, smt_ref, oc_ref, bias_ref, *, tq):
    q0 = pl.program_id(2) * tq
    kc = kc_ref[0]
    vc = vc_ref[0]
    nc = kc.shape[0]
    R = NSA_GROUP * tq
    sig = _sigmoid(g_ref[0])
    q = q_ref[...].reshape(R, HEAD_DIM)
    s = lax.dot_general(q, kc, (((1,), (1,)), ((), ())), preferred_element_type=f32)
    t = q0 + (lax.broadcasted_iota(i32, (R, nc), 0) & (tq - 1))
    cend = lax.broadcasted_iota(i32, (R, nc), 1) * CMP_STRIDE + (CMP_LEN - 1)
    mask = cend <= t
    s = jnp.where(mask, s, NEG_INF)
    m = jnp.max(s, axis=-1, keepdims=True)
    p = jnp.where(mask, jnp.exp(s - m), 0.0)
    p = p / jnp.maximum(jnp.sum(p, axis=-1, keepdims=True), 1e-30)
    o = jnp.dot(p.astype(bf16), vc, preferred_element_type=f32)
    for gi in range(NSA_GROUP):
        col = gi * N_BRANCH
        oc_ref[:, gi * HEAD_DIM:(gi + 1) * HEAD_DIM] = o[gi * tq:(gi + 1) * tq] * sig[:, col:col + 1]

    psum = p[0:tq] + p[tq:2 * tq] + p[2 * tq:3 * tq] + p[3 * tq:4 * tq]
    imp = lax.dot_general(smt_ref[...], psum, (((1,), (1,)), ((), ())),
                          precision=lax.Precision.HIGHEST, preferred_element_type=f32)
    nselp = imp.shape[0]
    jf = lax.broadcasted_iota(i32, (nselp, tq), 0).astype(f32)
    cur = ((q0 + lax.broadcasted_iota(i32, (nselp, tq), 1)) >> SEL_SHIFT).astype(f32)
    forced = (jf == 0.0) | (jf == cur) | (jf == cur - 1.0)
    imp = jnp.where(forced, FORCED_SCORE, imp)
    imp = jnp.where(jf <= cur, imp, NEG_INF)
    PICKED = -3e38
    work = imp
    for _ in range(N_SELECTED):
        mx = jnp.max(work, axis=0, keepdims=True)
        first = jnp.min(jnp.where(work == mx, jf, float(nselp)), axis=0, keepdims=True)
        work = jnp.where(jf == first, PICKED, work)
    bias_t = jnp.where((work < -2e38) & (imp > -1.0), 0.0, MASK_VAL)
    bias_ref[0] = bias_t.T.astype(bf16)


def _nsa_cmp(P, kcmp, vcmp, gate, smt, B, S, tq):
    T = B * S
    nqb = S // tq
    nc = kcmp.shape[1]
    nselp = smt.shape[0]
    return pl.pallas_call(
        functools.partial(_nsa_cmp_kernel, tq=tq),
        grid=(NSA_KV_HEADS, B, nqb),
        in_specs=[
            pl.BlockSpec((NSA_GROUP, tq, HEAD_DIM), lambda h, b, i: (h, b * nqb + i, 0)),
            pl.BlockSpec((1, nc, HEAD_DIM), lambda h, b, i: (h * B + b, 0, 0)),
            pl.BlockSpec((1, nc, HEAD_DIM), lambda h, b, i: (h * B + b, 0, 0)),
            pl.BlockSpec((1, tq, HEAD_DIM), lambda h, b, i: (h, b * nqb + i, 0)),
            pl.BlockSpec((nselp, nc), lambda h, b, i: (0, 0)),
        ],
        out_specs=[
            pl.BlockSpec((tq, NSA_GROUP * HEAD_DIM), lambda h, b, i: (b * nqb + i, h)),
            pl.BlockSpec((1, tq, nselp), lambda h, b, i: (h, b * nqb + i, 0)),
        ],
        out_shape=[
            jax.ShapeDtypeStruct((T, NSA_HEADS * HEAD_DIM), f32),
            jax.ShapeDtypeStruct((NSA_KV_HEADS, T, nselp), bf16),
        ],
        compiler_params=_cparams(("parallel", "parallel", "parallel")),
        name="nsa_cmp",
    )(P, kcmp, vcmp, gate, smt)


def _lane_tile(x, n):
    return x if n == 1 else jnp.concatenate([x] * n, axis=1)


ROW_CHUNK = 256


def _softmax_step(rows, s, v, m_ref, acc_ref, l_ref=None):
    tk = s.shape[1]
    m_prev = m_ref[rows, :]
    m_new = jnp.maximum(m_prev, jnp.max(s, axis=1, keepdims=True))
    alpha = jnp.exp(m_prev - m_new)
    p = jnp.exp(s - _lane_tile(m_new, tk // LANES))
    if l_ref is not None:
        l_ref[rows, :] = alpha * l_ref[rows, :] + jnp.sum(p, axis=1, keepdims=True)
    acc_ref[rows, :] = _lane_tile(alpha, 2) * acc_ref[rows, :] + jnp.dot(
        p.astype(bf16), v, preferred_element_type=f32)
    m_ref[rows, :] = m_new


def _nsa_sel_kernel(q_ref, bias_ref, ks_ref, vs_ref, kw_ref, vw_ref, g_ref, oc_ref,
                    o_ref, qaug_ref, m_ref, acc_ref, *, tq, tk, tkw):
    q0 = pl.program_id(2) * tq
    R = NSA_GROUP * tq
    nselp = bias_ref.shape[2]
    q = q_ref[...].reshape(R, HEAD_DIM)
    qaug_ref[:, 0:HEAD_DIM] = q
    bias = bias_ref[0]
    for gi in range(NSA_GROUP):
        qaug_ref[gi * tq:(gi + 1) * tq, HEAD_DIM:HEAD_DIM + nselp] = bias
    sig = _sigmoid(g_ref[0])
    rc = min(ROW_CHUNK, tq)

    def row_t(c, width):
        return q0 + ((c * rc + lax.broadcasted_iota(i32, (rc, width), 0)) & (tq - 1))

    def init():
        m_ref[...] = jnp.full(m_ref.shape, M_INIT, f32)
        acc_ref[...] = jnp.zeros(acc_ref.shape, f32)

    def finish():
        return acc_ref[:, 0:HEAD_DIM] / jnp.maximum(acc_ref[:, HEAD_DIM:2 * HEAD_DIM], 1e-30)

    init()
    ones_k = jnp.ones((tk, HEAD_DIM), bf16)

    def sel_step(kt, diagonal):
        k0 = pl.multiple_of(kt * tk, tk)
        k = ks_ref[0, pl.ds(k0, tk), :]
        blk = (k0 + lax.broadcasted_iota(i32, (tk, nselp), 0)) >> SEL_SHIFT
        onehot = jnp.where(blk == lax.broadcasted_iota(i32, (tk, nselp), 1), 1.0, 0.0).astype(bf16)
        kaug = jnp.concatenate([k, onehot], axis=1)
        vaug = jnp.concatenate([vs_ref[0, pl.ds(k0, tk), :], ones_k], axis=1)
        for c in range(R // rc):
            rows = slice(c * rc, (c + 1) * rc)
            s = lax.dot_general(qaug_ref[rows, :], kaug, (((1,), (1,)), ((), ())),
                                preferred_element_type=f32)
            if diagonal:
                s = jnp.where(k0 + lax.broadcasted_iota(i32, (rc, tk), 1) <= row_t(c, tk), s, MASK_VAL)
            _softmax_step(rows, s, vaug, m_ref, acc_ref)

    def sel_body(kt, carry):
        sel_step(kt, False)
        return carry

    n_full = q0 // tk
    lax.fori_loop(0, n_full, sel_body, 0)
    sel_step(n_full, True)
    o_s = finish()

    init()
    ones_w = jnp.ones((tkw, HEAD_DIM), bf16)
    qi = pl.program_id(2)

    def win_step(kt, mode):
        k0 = pl.multiple_of(kt * tkw, tkw)
        k = kw_ref[0, pl.ds(k0, tkw), :]
        vaug = jnp.concatenate([vw_ref[0, pl.ds(k0, tkw), :], ones_w], axis=1)
        for c in range(R // rc):
            rows = slice(c * rc, (c + 1) * rc)
            s = lax.dot_general(qaug_ref[rows, 0:HEAD_DIM], k, (((1,), (1,)), ((), ())),
                                preferred_element_type=f32)
            if mode != "full":
                t = row_t(c, tkw)
                kp = k0 + lax.broadcasted_iota(i32, (rc, tkw), 1)
                s = jnp.where(kp <= t if mode == "diagonal" else t - kp < WINDOW, s, MASK_VAL)
            _softmax_step(rows, s, vaug, m_ref, acc_ref)

    @pl.when(qi >= 2)
    def _():
        win_step(qi - 2, "edge")

    @pl.when(qi >= 1)
    def _():
        win_step(qi - 1, "full")

    win_step(qi, "diagonal")
    o_w = finish()

    for gi in range(NSA_GROUP):
        rows = slice(gi * tq, (gi + 1) * tq)
        cols = slice(gi * HEAD_DIM, (gi + 1) * HEAD_DIM)
        c = gi * N_BRANCH
        o_ref[:, cols] = (oc_ref[:, cols] + o_s[rows] * sig[:, c + 1:c + 2]
                          + o_w[rows] * sig[:, c + 2:c + 3]).astype(bf16)


def _nsa_sel(Pn, Pr, bias, gate, o_c, B, S, tq, tk, tkw):
    assert tk % tq == 0 and tkw == tq and WINDOW == 2 * tq
    T = B * S
    nqb = S // tq
    nselp = bias.shape[2]
    R = NSA_GROUP * tq
    kv_spec = lambda base: pl.BlockSpec((1, S, HEAD_DIM), lambda h, b, i: (base + h, b, 0))
    return pl.pallas_call(
        functools.partial(_nsa_sel_kernel, tq=tq, tk=tk, tkw=tkw),
        grid=(NSA_KV_HEADS, B, nqb),
        in_specs=[
            pl.BlockSpec((NSA_GROUP, tq, HEAD_DIM), lambda h, b, i: (h, b * nqb + i, 0)),
            pl.BlockSpec((1, tq, nselp), lambda h, b, i: (h, b * nqb + i, 0)),
            kv_spec(HD_KS), kv_spec(HR_VS), kv_spec(HD_KW), kv_spec(HR_VW),
            pl.BlockSpec((1, tq, HEAD_DIM), lambda h, b, i: (h, b * nqb + i, 0)),
            pl.BlockSpec((tq, NSA_GROUP * HEAD_DIM), lambda h, b, i: (b * nqb + i, h)),
        ],
        out_specs=pl.BlockSpec((tq, NSA_GROUP * HEAD_DIM), lambda h, b, i: (b * nqb + i, h)),
        out_shape=jax.ShapeDtypeStruct((T, NSA_HEADS * HEAD_DIM), bf16),
        scratch_shapes=[
            pltpu.VMEM((R, HEAD_DIM + nselp), bf16),
            pltpu.VMEM((R, LANES), f32),
            pltpu.VMEM((R, 2 * HEAD_DIM), f32),
        ],
        compiler_params=_cparams(("parallel", "parallel", "arbitrary")),
        name="nsa_sel_win",
    )(Pn, bias, Pn, Pr, Pn, Pr, gate, o_c)


def _diff_kernel(q_ref, k_ref, v_ref, lam_ref, sg_ref, o_ref, m_ref, l_ref, acc_ref, sa_ref, sb_ref,
                 *, tq, tk, lambda_init):
    q0 = pl.program_id(2) * tq
    R = 2 * tq
    rc = min(ROW_CHUNK, tq)
    m_ref[...] = jnp.full(m_ref.shape, M_INIT, f32)
    l_ref[...] = jnp.zeros(l_ref.shape, f32)
    acc_ref[...] = jnp.zeros(acc_ref.shape, f32)

    def scores(kt, s_ref):
        k0 = pl.multiple_of(kt * tk, tk)
        for comp in range(2):
            s_ref[comp * tq:(comp + 1) * tq, :] = lax.dot_general(
                q_ref[comp], k_ref[comp, pl.ds(k0, tk), :], (((1,), (1,)), ((), ())),
                preferred_element_type=f32)

    def softmax_pv(kt, s_ref, diagonal):
        k0 = pl.multiple_of(kt * tk, tk)
        v = jnp.concatenate([v_ref[0, pl.ds(k0, tk), :], v_ref[1, pl.ds(k0, tk), :]], axis=1)
        for c in range(R // rc):
            rows = slice(c * rc, (c + 1) * rc)
            s = s_ref[rows, :]
            if diagonal:
                t = q0 + ((c * rc + lax.broadcasted_iota(i32, (rc, tk), 0)) & (tq - 1))
                s = jnp.where(k0 + lax.broadcasted_iota(i32, (rc, tk), 1) <= t, s, MASK_VAL)
            _softmax_step(rows, s, v, m_ref, acc_ref, l_ref)

    def pair(j, carry):
        scores(2 * j + 1, sb_ref)
        softmax_pv(2 * j, sa_ref, False)
        scores(2 * j + 2, sa_ref)
        softmax_pv(2 * j + 1, sb_ref, False)
        return carry

    qi = pl.program_id(2)
    scores(0, sa_ref)
    lax.fori_loop(0, qi // 2, pair, 0)

    @pl.when(qi % 2 == 0)
    def _():
        softmax_pv(qi, sa_ref, True)

    @pl.when(qi % 2 == 1)
    def _():
        scores(qi, sb_ref)
        softmax_pv(qi - 1, sa_ref, False)
        softmax_pv(qi, sb_ref, True)

    lq = lam_ref[...]
    lam = (jnp.exp(jnp.sum(lq[0:1] * lq[1:2], axis=1, keepdims=True))
           - jnp.exp(jnp.sum(lq[2:3] * lq[3:4], axis=1, keepdims=True)) + lambda_init)
    o = acc_ref[...] / _lane_tile(jnp.maximum(l_ref[...], 1e-30), 2)
    a = o[0:tq] - lam * o[tq:R]
    o_ref[...] = (_rms(a, sg_ref[...]) * (1.0 - lambda_init)).astype(bf16)


def _diff_attn(Pn, Pr, lam_vecs, subln_g, B, S, tq, tk, lambda_init):
    assert tq == tk
    T = B * S
    nqb = S // tq
    return pl.pallas_call(
        functools.partial(_diff_kernel, tq=tq, tk=tk, lambda_init=lambda_init),
        grid=(DIFF_HEADS, B, nqb),
        in_specs=[
            pl.BlockSpec((2, tq, HEAD_DIM), lambda h, b, i: (HD_QD // 2 + h, b * nqb + i, 0)),
            pl.BlockSpec((2, S, HEAD_DIM), lambda h, b, i: (HD_KD // 2 + h, b, 0)),
            pl.BlockSpec((2, S, HEAD_DIM), lambda h, b, i: (HR_VD // 2 + h, b, 0)),
            pl.BlockSpec((8, HEAD_DIM), lambda h, b, i: (0, 0)),
            pl.BlockSpec((1, DIFF_V_DIM), lambda h, b, i: (0, 0)),
        ],
        out_specs=pl.BlockSpec((tq, DIFF_V_DIM), lambda h, b, i: (b * nqb + i, h)),
        out_shape=jax.ShapeDtypeStruct((T, DIFF_HEADS * DIFF_V_DIM), bf16),
        scratch_shapes=[
            pltpu.VMEM((2 * tq, LANES), f32),
            pltpu.VMEM((2 * tq, LANES), f32),
            pltpu.VMEM((2 * tq, DIFF_V_DIM), f32),
            pltpu.VMEM((2 * tq, tk), f32),
            pltpu.VMEM((2 * tq, tk), f32),
        ],
        compiler_params=_cparams(("parallel", "parallel", "arbitrary")),
        name="diff_attn",
    )(Pn, Pn, Pr, lam_vecs, subln_g)


def _outproj_kernel(x_ref, on_ref, od_ref, w_ref, g_ref, wr_ref, br_ref,
                    h_ref, hn_ref, lg_ref):
    half = NSA_HEADS * HEAD_DIM
    tm = x_ref.shape[0]
    rc = min(ROW_CHUNK, tm)
    for c in range(tm // rc):
        rows = slice(c * rc, (c + 1) * rc)
        h = x_ref[rows, :] + jnp.dot(on_ref[rows, :], w_ref[0:half], preferred_element_type=f32) \
            + jnp.dot(od_ref[rows, :], w_ref[half:], preferred_element_type=f32)
        h_ref[rows, :] = h
        hn = _rms(h, g_ref[...])
        hn_ref[rows, :] = hn
        hn_hi = hn.astype(bf16)
        hn_lo = (hn - hn_hi.astype(f32)).astype(bf16)
        r = (jnp.dot(hn_hi, wr_ref[...], preferred_element_type=f32)
             + jnp.dot(hn_lo, wr_ref[...], preferred_element_type=f32))
        lg_ref[rows, :] = r[:, 0:LANES] + r[:, LANES:2 * LANES] + br_ref[...]


def _outproj(x2, o_nsa, o_diff, w_out_b, ffn_g, w_router_p, b_router_p, tm):
    T = x2.shape[0]
    full = lambda shape: pl.BlockSpec(shape, lambda i: (0,) * len(shape))
    return pl.pallas_call(
        _outproj_kernel,
        grid=(T // tm,),
        in_specs=[
            pl.BlockSpec((tm, D_MODEL), lambda i: (i, 0)),
            pl.BlockSpec((tm, NSA_HEADS * HEAD_DIM), lambda i: (i, 0)),
            pl.BlockSpec((tm, DIFF_HEADS * DIFF_V_DIM), lambda i: (i, 0)),
            full((D_MODEL, D_MODEL)),
            full((1, D_MODEL)),
            full((D_MODEL, 2 * LANES)),
            full((1, LANES)),
        ],
        out_specs=[
            pl.BlockSpec((tm, D_MODEL), lambda i: (i, 0)),
            pl.BlockSpec((tm, D_MODEL), lambda i: (i, 0)),
            pl.BlockSpec((tm, LANES), lambda i: (i, 0)),
        ],
        out_shape=[
            jax.ShapeDtypeStruct((T, D_MODEL), f32),
            jax.ShapeDtypeStruct((T, D_MODEL), f32),
            jax.ShapeDtypeStruct((T, LANES), f32),
        ],
        compiler_params=_cparams(("parallel",)),
        name="outproj_router",
    )(x2, o_nsa, o_diff, w_out_b, ffn_g, w_router_p, b_router_p)


ISSUE_UNROLL = 8


def _start_row_gathers(idx_fn, n, src_ref, dst_ref, sem):
    def body(c, carry):
        for u in range(ISSUE_UNROLL):
            r = c * ISSUE_UNROLL + u
            pltpu.make_async_copy(src_ref.at[pl.ds(idx_fn(r), 1)], dst_ref.at[pl.ds(r, 1)], sem).start()
        return carry

    lax.fori_loop(0, n // ISSUE_UNROLL, body, 0)


def _wait_row_gathers(n, src_ref, dst_ref, sem):
    for r in range(n):
        pltpu.make_async_copy(src_ref.at[pl.ds(0, 1)], dst_ref.at[pl.ds(r, 1)], sem).wait()


def _gather_kernel(st_ref, sb_ref, src_ref, o_ref, buf_ref, sem, *, tm, n_src):
    b = pl.program_id(0)
    nb = pl.num_programs(0)

    def issue(blk, slot):
        base = sb_ref[blk]
        _start_row_gathers(lambda r: st_ref[jnp.minimum(base + r, n_src - 1)], tm,
                           src_ref, buf_ref.at[slot], sem.at[slot])

    @pl.when(b == 0)
    def _():
        issue(0, 0)

    @pl.when(b + 1 < nb)
    def _():
        issue(b + 1, (b + 1) % 2)

    slot = b % 2
    _wait_row_gathers(tm, src_ref, buf_ref.at[slot], sem.at[slot])
    o_ref[...] = buf_ref[slot].astype(o_ref.dtype)


def _gather_rows(sorted_tok, src_base, src, n_blocks, tm):
    return pl.pallas_call(
        functools.partial(_gather_kernel, tm=tm, n_src=sorted_tok.shape[0]),
        grid_spec=pltpu.PrefetchScalarGridSpec(
            num_scalar_prefetch=2,
            grid=(n_blocks,),
            in_specs=[pl.BlockSpec(memory_space=pl.ANY)],
            out_specs=pl.BlockSpec((tm, src.shape[1]), lambda b, st, sb: (b, 0)),
            scratch_shapes=[pltpu.VMEM((2, tm, src.shape[1]), src.dtype), pltpu.SemaphoreType.DMA((2,))],
        ),
        out_shape=jax.ShapeDtypeStruct((n_blocks * tm, src.shape[1]), bf16),
        compiler_params=_cparams(("arbitrary",)),
        name="moe_gather",
    )(sorted_tok, src_base, src)


def _expert_row_blocks(first, nblk, x_hbm, o_dst, xbuf, obuf, xsem, osem, tm, compute):
    def x_copy(blk, slot):
        return pltpu.make_async_copy(x_hbm.at[pl.ds(pl.multiple_of(blk * tm, tm), tm)],
                                     xbuf.at[slot], xsem.at[slot])

    def o_copy(blk, slot):
        return pltpu.make_async_copy(obuf.at[slot], o_dst(pl.multiple_of(blk * tm, tm)), osem.at[slot])

    @pl.when(nblk > 0)
    def _():
        x_copy(first, 0).start()

    def body(i, carry):
        slot = i % 2

        @pl.when(i + 1 < nblk)
        def _():
            x_copy(first + i + 1, 1 - slot).start()

        x_copy(first + i, slot).wait()

        @pl.when(i >= 2)
        def _():
            o_copy(first + i - 2, slot).wait()

        obuf[slot] = compute(xbuf[slot])
        o_copy(first + i, slot).start()
        return carry

    lax.fori_loop(0, nblk, body, 0)

    @pl.when(nblk >= 2)
    def _():
        o_copy(first + nblk - 2, nblk % 2).wait()

    @pl.when(nblk >= 1)
    def _():
        o_copy(first + nblk - 1, (nblk - 1) % 2).wait()

    return o_copy


def _zero_unused_blocks(o_copy, obuf, lo, hi):
    obuf[0] = jnp.zeros(obuf.shape[1:], obuf.dtype)

    def body(blk, carry):
        cp = o_copy(blk, 0)
        cp.start()
        cp.wait()
        return carry

    lax.fori_loop(lo, hi, body, 0)


def _gateup_kernel(fb_ref, nbk_ref, nu_ref, xs_hbm, wg_ref, wu_ref, bg_ref, bu_ref, hb_hbm,
                   wgb_ref, wub_ref, xbuf, obuf, xsem, osem, *, tm, tn, n_blocks):
    j = pl.program_id(0)
    e = pl.program_id(1)
    wgb_ref[...] = wg_ref[0].astype(bf16)
    wub_ref[...] = wu_ref[0].astype(bf16)
    col0 = pl.multiple_of(j * tn, tn)

    def compute(x):
        gl = jnp.dot(x, wgb_ref[...], preferred_element_type=f32) + bg_ref[0]
        up = jnp.dot(x, wub_ref[...], preferred_element_type=f32) + bu_ref[0]
        gl = jnp.minimum(gl, SWIGLU_LIMIT)
        up = jnp.clip(up, -SWIGLU_LIMIT, SWIGLU_LIMIT)
        return ((up + 1.0) * (gl * _sigmoid(SWIGLU_ALPHA * gl))).astype(bf16)

    o_copy = _expert_row_blocks(fb_ref[e], nbk_ref[e], xs_hbm,
                                lambda r0: hb_hbm.at[pl.ds(r0, tm), pl.ds(col0, tn)],
                                xbuf, obuf, xsem, osem, tm, compute)

    @pl.when(e == N_EXPERTS - 1)
    def _():
        _zero_unused_blocks(o_copy, obuf, nu_ref[0], n_blocks)


def _gateup(first_blk, n_blk, n_used, xs, w_gate_up, b_gate_up3, tm, tn):
    n_rows = xs.shape[0]
    nj = D_FF // tn
    return pl.pallas_call(
        functools.partial(_gateup_kernel, tm=tm, tn=tn, n_blocks=n_rows // tm),
        grid_spec=pltpu.PrefetchScalarGridSpec(
            num_scalar_prefetch=3,
            grid=(nj, N_EXPERTS),
            in_specs=[
                pl.BlockSpec(memory_space=pl.ANY),
                pl.BlockSpec((1, D_MODEL, tn), lambda j, e, *_: (e, 0, j)),
                pl.BlockSpec((1, D_MODEL, tn), lambda j, e, *_: (e, 0, nj + j)),
                pl.BlockSpec((1, 1, tn), lambda j, e, *_: (e, 0, j)),
                pl.BlockSpec((1, 1, tn), lambda j, e, *_: (e, 0, nj + j)),
            ],
            out_specs=pl.BlockSpec(memory_space=pl.ANY),
            scratch_shapes=[
                pltpu.VMEM((D_MODEL, tn), bf16), pltpu.VMEM((D_MODEL, tn), bf16),
                pltpu.VMEM((2, tm, D_MODEL), bf16), pltpu.VMEM((2, tm, tn), bf16),
                pltpu.SemaphoreType.DMA((2,)), pltpu.SemaphoreType.DMA((2,)),
            ],
        ),
        out_shape=jax.ShapeDtypeStruct((n_rows, D_FF), bf16),
        compiler_params=_cparams(("arbitrary", "arbitrary")),
        name="moe_gate_up",
    )(first_blk, n_blk, n_used, xs, w_gate_up, w_gate_up, b_gate_up3, b_gate_up3)


def _down_kernel(fb_ref, nbk_ref, nu_ref, hb_hbm, w_ref, b_ref, y_hbm,
                 wb_ref, xbuf, obuf, xsem, osem, *, tm, n_blocks):
    e = pl.program_id(0)
    wb_ref[...] = w_ref[0].astype(bf16)

    def compute(x):
        return jnp.dot(x, wb_ref[...], preferred_element_type=f32) + b_ref[0]

    o_copy = _expert_row_blocks(fb_ref[e], nbk_ref[e], hb_hbm, lambda r0: y_hbm.at[pl.ds(r0, tm)],
                                xbuf, obuf, xsem, osem, tm, compute)

    @pl.when(e == N_EXPERTS - 1)
    def _():
        _zero_unused_blocks(o_copy, obuf, nu_ref[0], n_blocks)


def _down(first_blk, n_blk, n_used, hb, w_down, b_down3, tm):
    n_rows = hb.shape[0]
    return pl.pallas_call(
        functools.partial(_down_kernel, tm=tm, n_blocks=n_rows // tm),
        grid_spec=pltpu.PrefetchScalarGridSpec(
            num_scalar_prefetch=3,
            grid=(N_EXPERTS,),
            in_specs=[
                pl.BlockSpec(memory_space=pl.ANY),
                pl.BlockSpec((1, D_FF, D_MODEL), lambda e, *_: (e, 0, 0)),
                pl.BlockSpec((1, 1, D_MODEL), lambda e, *_: (e, 0, 0)),
            ],
            out_specs=pl.BlockSpec(memory_space=pl.ANY),
            scratch_shapes=[
                pltpu.VMEM((D_FF, D_MODEL), bf16),
                pltpu.VMEM((2, tm, D_FF), bf16), pltpu.VMEM((2, tm, D_MODEL), f32),
                pltpu.SemaphoreType.DMA((2,)), pltpu.SemaphoreType.DMA((2,)),
            ],
        ),
        out_shape=jax.ShapeDtypeStruct((n_rows, D_MODEL), f32),
        compiler_params=_cparams(("arbitrary",)),
        name="moe_down",
    )(first_blk, n_blk, n_used, hb, w_down, b_down3)


def _combine_kernel(pos_ref, h_ref, gt_ref, y_ref, o_ref, buf_ref, sem, *, tc):
    i = pl.program_id(0)
    n = pl.num_programs(0)
    rows = TOP_K * tc

    def issue(step, slot):
        base = step * rows
        _start_row_gathers(lambda r: pos_ref[base + r], rows, y_ref, buf_ref.at[slot], sem.at[slot])

    @pl.when(i == 0)
    def _():
        issue(0, 0)

    @pl.when(i + 1 < n)
    def _():
        issue(i + 1, (i + 1) % 2)

    slot = i % 2
    _wait_row_gathers(rows, y_ref, buf_ref.at[slot], sem.at[slot])
    gt = gt_ref[...]
    out = h_ref[...]
    for k in range(TOP_K):
        out = out + gt[:, k:k + 1] * buf_ref[slot, k * tc:(k + 1) * tc, :]
    o_ref[...] = out


def _combine(pos_blocks, h, gate_pad, y, tc):
    T = h.shape[0]
    return pl.pallas_call(
        functools.partial(_combine_kernel, tc=tc),
        grid_spec=pltpu.PrefetchScalarGridSpec(
            num_scalar_prefetch=1,
            grid=(T // tc,),
            in_specs=[
                pl.BlockSpec((tc, D_MODEL), lambda i, pos: (i, 0)),
                pl.BlockSpec((tc, LANES), lambda i, pos: (i, 0)),
                pl.BlockSpec(memory_space=pl.ANY),
            ],
            out_specs=pl.BlockSpec((tc, D_MODEL), lambda i, pos: (i, 0)),
            scratch_shapes=[pltpu.VMEM((2, TOP_K * tc, D_MODEL), f32), pltpu.SemaphoreType.DMA((2,))],
        ),
        out_shape=jax.ShapeDtypeStruct((T, D_MODEL), f32),
        compiler_params=_cparams(("arbitrary",)),
        name="moe_combine",
    )(pos_blocks, h, gate_pad, y)


def _rearranged_w_in(w_in):
    splits = np.cumsum([0, 1024, 256, 256, 256, 256, 256, 256, 24, 1024, 1024, 1024])
    q, kc, vc, ks, vs, kw, vw, g, qd, kd, vd = [w_in[:, splits[n]:splits[n + 1]] for n in range(11)]
    gcols = NSA_GROUP * N_BRANCH
    zpad = lambda n: jnp.zeros((D_MODEL, n), w_in.dtype)
    w_norm = jnp.concatenate([q, ks, kw, qd, kd], axis=1)
    w_raw = jnp.concatenate([kc, vc, vs, vw, vd,
                             g[:, :gcols], zpad(HEAD_DIM - gcols), g[:, gcols:], zpad(HEAD_DIM - gcols),
                             zpad((N_RAW_HEADS - HR_GATE - 2) * HEAD_DIM)], axis=1)
    return w_norm.astype(bf16), w_raw.astype(bf16)


def _sel_map_t(S, nselp):
    n_chunks = S // CMP_STRIDE
    n_cmp = (S - CMP_LEN) // CMP_STRIDE + 1
    n_sel = S // SEL_BLOCK
    c_start = np.arange(n_chunks) * CMP_STRIDE
    s_start = np.arange(nselp) * SEL_BLOCK
    ov = (c_start[None, :] < s_start[:, None] + SEL_BLOCK) & (c_start[None, :] + CMP_LEN > s_start[:, None])
    ov &= (np.arange(n_chunks)[None, :] < n_cmp) & (np.arange(nselp)[:, None] < n_sel)
    return jnp.asarray(ov, f32)


def _attention_block(x2, B, S, attn_norm_g, w_in, nsa_q_norm_g, nsa_k_norm_g, pos_k, pos_v,
                     k_w1, k_w2, v_w1, v_w2, diff_q_norm_g, diff_k_norm_g, lq1, lk1, lq2, lk2,
                     subln_g, lambda_init):
    T = B * S
    scale = HEAD_DIM ** -0.5
    tm = min(1024, S)
    rope_c, rope_sa, rope_sb = _rope_tables(np.arange(S))
    gains = jnp.concatenate([
        jnp.broadcast_to(nsa_q_norm_g * scale, (NSA_HEADS, HEAD_DIM)),
        jnp.broadcast_to(nsa_k_norm_g[1], (NSA_KV_HEADS, HEAD_DIM)),
        jnp.broadcast_to(nsa_k_norm_g[2], (NSA_KV_HEADS, HEAD_DIM)),
        jnp.broadcast_to(diff_q_norm_g * scale, (2 * DIFF_HEADS, HEAD_DIM)),
        jnp.broadcast_to(diff_k_norm_g, (2 * DIFF_HEADS, HEAD_DIM)),
    ], axis=0).reshape(N_NORM_HEADS, 1, HEAD_DIM)
    w_norm, w_raw = _rearranged_w_in(w_in)
    Pn, Pr, gate = _inproj(x2, attn_norm_g.reshape(1, D_MODEL), w_norm, w_raw, gains,
                           rope_c, rope_sa, rope_sb, S, tm)

    n_chunks = S // CMP_STRIDE
    cmp_pos = np.arange(n_chunks) * CMP_STRIDE + (CMP_LEN - 1)
    cc, csa, csb = _rope_tables(cmp_pos)
    chunk_view = lambda hd: Pr[hd:hd + NSA_KV_HEADS].reshape(NSA_KV_HEADS * B, n_chunks, CMP_STRIDE * HEAD_DIM)
    kcmp = _compress(chunk_view(HR_KC), pos_k, k_w1, k_w2, nsa_k_norm_g[0].reshape(1, HEAD_DIM),
                     cc, csa, csb, True)
    vcmp = _compress(chunk_view(HR_VC), pos_v, v_w1, v_w2, nsa_k_norm_g[0].reshape(1, HEAD_DIM),
                     cc, csa, csb, False)

    nselp = -(-(S // SEL_BLOCK) // LANES) * LANES
    tq = min(256, S)
    o_c, bias = _nsa_cmp(Pn, kcmp, vcmp, gate, _sel_map_t(S, nselp), B, S, tq)
    o_nsa = _nsa_sel(Pn, Pr, bias, gate, o_c, B, S, tq, min(512, S), min(256, S))

    lam_vecs = jnp.concatenate([lq1[None], lk1[None], lq2[None], lk2[None],
                                jnp.zeros((4, HEAD_DIM), f32)], axis=0)
    o_diff = _diff_attn(Pn, Pr, lam_vecs, subln_g.reshape(1, DIFF_V_DIM), B, S,
                        min(512, S), min(512, S), lambda_init)
    return o_nsa, o_diff


def _moe_block(h, hn, logits, w_gate_up, b_gate_up, w_down, b_down, tm=256):
    T = h.shape[0]
    TK = T * TOP_K
    top_v, top_e = lax.top_k(logits[:, :N_EXPERTS], TOP_K)
    gate = jax.nn.softmax(top_v, axis=-1)
    flat_e = top_e.reshape(TK).astype(i32)
    se, order = lax.sort((flat_e, jnp.arange(TK, dtype=i32)), num_keys=1, is_stable=True)
    st = order // TOP_K
    experts = jnp.arange(N_EXPERTS, dtype=i32)
    counts = jnp.sum((flat_e[:, None] == experts[None, :]).astype(i32), axis=0)
    padded = (counts + tm - 1) // tm * tm
    pad_end = jnp.cumsum(padded)
    pad_start = pad_end - padded
    cnt_start = jnp.cumsum(counts) - counts
    dest = pad_start[se] + (jnp.arange(TK, dtype=i32) - cnt_start[se])
    _, pos = lax.sort((order, dest), num_keys=1)
    n_blocks = -(-TK // tm) + N_EXPERTS
    blk_start = jnp.arange(n_blocks, dtype=i32) * tm
    blk_e = jnp.minimum(jnp.sum((pad_end[None, :] <= blk_start[:, None]).astype(i32), axis=1), N_EXPERTS - 1)
    src_base = cnt_start[blk_e] + (blk_start - pad_start[blk_e])
    n_used = (pad_end[-1] // tm).astype(i32).reshape(1)

    xs = _gather_rows(st, src_base.astype(i32), hn, n_blocks, tm)
    first_blk = (pad_start // tm).astype(i32)
    n_blk = (padded // tm).astype(i32)
    hb = _gateup(first_blk, n_blk, n_used, xs, w_gate_up, b_gate_up.reshape(N_EXPERTS, 1, 2 * D_FF), tm, 1024)
    y = _down(first_blk, n_blk, n_used, hb, w_down, b_down.reshape(N_EXPERTS, 1, D_MODEL), tm)

    tc = 64
    pos_blocks = pos.reshape(T // tc, tc, TOP_K).transpose(0, 2, 1).reshape(TK)
    gate_pad = jnp.pad(gate, ((0, 0), (0, LANES - TOP_K)))
    return _combine(pos_blocks, h, gate_pad, y, tc)


def kernel(x, attn_norm_g, w_in, nsa_q_norm_g, nsa_k_norm_g, nsa_cmp_pos_k, nsa_cmp_pos_v, nsa_cmp_k_w1, nsa_cmp_k_w2, nsa_cmp_v_w1, nsa_cmp_v_w2, diff_q_norm_g, diff_k_norm_g, diff_lambda_q1, diff_lambda_k1, diff_lambda_q2, diff_lambda_k2, diff_subln_g, w_out, ffn_norm_g, w_router, b_router, w_gate_up, b_gate_up, w_down, b_down):
    B, S, _ = x.shape
    T = B * S
    depth = attn_norm_g.shape[0]
    h = x.reshape(T, D_MODEL)
    for l in range(depth):
        lambda_init = 0.8 - 0.6 * math.exp(-0.3 * l)
        o_nsa, o_diff = _attention_block(
            h, B, S, attn_norm_g[l], w_in[l], nsa_q_norm_g[l], nsa_k_norm_g[l], nsa_cmp_pos_k[l],
            nsa_cmp_pos_v[l], nsa_cmp_k_w1[l], nsa_cmp_k_w2[l], nsa_cmp_v_w1[l], nsa_cmp_v_w2[l],
            diff_q_norm_g[l], diff_k_norm_g[l], diff_lambda_q1[l], diff_lambda_k1[l],
            diff_lambda_q2[l], diff_lambda_k2[l], diff_subln_g[l], lambda_init)
        w_router_f = jnp.pad(w_router[l], ((0, 0), (0, LANES - N_EXPERTS)))
        w_router_hi = w_router_f.astype(bf16)
        w_router_lo = (w_router_f - w_router_hi.astype(f32)).astype(bf16)
        w_router_p = jnp.concatenate([w_router_hi, w_router_lo], axis=1)
        b_router_p = jnp.pad(b_router[l], (0, LANES - N_EXPERTS)).reshape(1, LANES)
        h_mid, hn, logits = _outproj(h, o_nsa, o_diff, w_out[l].astype(bf16),
                                     ffn_norm_g[l].reshape(1, D_MODEL), w_router_p, b_router_p,
                                     min(512, T))
        h = _moe_block(h_mid, hn, logits, w_gate_up[l], b_gate_up[l], w_down[l], b_down[l])
    return h.reshape(B, S, D_MODEL)
```

```python
import functools
import math

import numpy as np
import jax
import jax.numpy as jnp
from jax import lax
from jax.experimental import pallas as pl
from jax.experimental.pallas import tpu as pltpu

f32 = jnp.float32
bf16 = jnp.bfloat16
i32 = jnp.int32

D_MODEL = 2048
HEAD_DIM = 128
ROT_DIM = HEAD_DIM // 4
ROPE_THETA = 500000.0
NORM_EPS = 1e-6
NEG_INF = -1e30

NSA_HEADS = 8
NSA_KV_HEADS = 2
NSA_GROUP = NSA_HEADS // NSA_KV_HEADS
N_BRANCH = 3
CMP_LEN = 32
CMP_STRIDE = 16
CMP_HIDDEN = 2 * HEAD_DIM
SEL_BLOCK = 64
SEL_SHIFT = 6
N_SELECTED = 16
WINDOW = 512
FORCED_SCORE = 1e9

DIFF_HEADS = 4
DIFF_V_DIM = 2 * HEAD_DIM

N_EXPERTS = 32
TOP_K = 4
D_FF = D_MODEL
SWIGLU_ALPHA = 1.702
SWIGLU_LIMIT = 7.0

LANES = 128

HD_Q = 0
HD_KS = 8
HD_KW = 10
HD_QD = 12
HD_KD = 20
N_NORM_HEADS = 28
HR_KC = 0
HR_VC = 2
HR_VS = 4
HR_VW = 6
HR_VD = 8
HR_GATE = 16
N_RAW_HEADS = 20
HP = 4

MASK_VAL = -1e30
M_INIT = -5e29

VMEM_LIMIT = 56 * 1024 * 1024


def _cparams(sem):
    return pltpu.CompilerParams(dimension_semantics=sem, vmem_limit_bytes=VMEM_LIMIT)


def _rope_tables(pos):
    inv = np.power(ROPE_THETA, -np.arange(0, ROT_DIM, 2, dtype=np.float64) / ROT_DIM)
    ang = pos.astype(np.float64)[:, None] * inv[None, :]
    cos, sin = np.cos(ang), np.sin(ang)
    n = pos.shape[0]
    half = ROT_DIM // 2
    c = np.concatenate([cos, cos, np.ones((n, HEAD_DIM - ROT_DIM))], axis=1)
    sa = np.concatenate([-sin, np.zeros((n, HEAD_DIM - half))], axis=1)
    sb = np.concatenate([np.zeros((n, half)), sin, np.zeros((n, HEAD_DIM - ROT_DIM))], axis=1)
    return (jnp.asarray(c, f32), jnp.asarray(sa, f32), jnp.asarray(sb, f32))


def _rope(y, c, sa, sb):
    half = ROT_DIM // 2
    return (y * c + pltpu.roll(y, HEAD_DIM - half, 1) * sa + pltpu.roll(y, half, 1) * sb)


def _rms(y, gain):
    ms = jnp.mean(y * y, axis=-1, keepdims=True)
    return y * lax.rsqrt(ms + NORM_EPS) * gain


def _inproj_norm_kernel(x_ref, g_ref, w_ref, gain_ref, c_ref, sa_ref, sb_ref, o_ref, xn_ref):
    @pl.when(pl.program_id(1) == 0)
    def _():
        xn_ref[...] = _rms(x_ref[...], g_ref[...]).astype(bf16)

    tm = x_ref.shape[0]
    rc = min(ROW_CHUNK, tm)
    for r in range(tm // rc):
        rows = slice(r * rc, (r + 1) * rc)
        acc = jnp.dot(xn_ref[rows, :], w_ref[...], preferred_element_type=f32)
        c, sa, sb = c_ref[rows, :], sa_ref[rows, :], sb_ref[rows, :]
        for u in range(HP):
            y = _rms(acc[:, u * HEAD_DIM:(u + 1) * HEAD_DIM], gain_ref[u])
            o_ref[u, rows, :] = _rope(y, c, sa, sb).astype(bf16)


def _inproj_raw_kernel(x_ref, g_ref, w_ref, o_ref, gate_ref, xn_ref):
    j = pl.program_id(1)

    @pl.when(j == 0)
    def _():
        xn_ref[...] = _rms(x_ref[...], g_ref[...]).astype(bf16)

    tm = x_ref.shape[0]
    rc = min(ROW_CHUNK, tm)
    for r in range(tm // rc):
        rows = slice(r * rc, (r + 1) * rc)
        acc = jnp.dot(xn_ref[rows, :], w_ref[...], preferred_element_type=f32)
        for u in range(HP):
            o_ref[u, rows, :] = acc[:, u * HEAD_DIM:(u + 1) * HEAD_DIM].astype(bf16)

    @pl.when(j == HR_GATE // HP)
    def _():
        gate_ref[0] = jnp.dot(xn_ref[...], w_ref[:, 0:HEAD_DIM], preferred_element_type=f32)
        gate_ref[1] = jnp.dot(xn_ref[...], w_ref[:, HEAD_DIM:2 * HEAD_DIM], preferred_element_type=f32)


def _inproj(x2, attn_g, w_norm, w_raw, gains, rope_c, rope_sa, rope_sb, S, tm):
    T = x2.shape[0]
    nsb = S // tm
    x_specs = [
        pl.BlockSpec((tm, D_MODEL), lambda i, j: (i, 0)),
        pl.BlockSpec((1, D_MODEL), lambda i, j: (0, 0)),
        pl.BlockSpec((D_MODEL, HP * HEAD_DIM), lambda i, j: (0, j)),
    ]
    head_spec = pl.BlockSpec((HP, tm, HEAD_DIM), lambda i, j: (j, i, 0))
    pos_spec = pl.BlockSpec((tm, HEAD_DIM), lambda i, j: (i % nsb, 0))
    Pn = pl.pallas_call(
        _inproj_norm_kernel,
        grid=(T // tm, N_NORM_HEADS // HP),
        in_specs=x_specs + [pl.BlockSpec((HP, 1, HEAD_DIM), lambda i, j: (j, 0, 0)),
                            pos_spec, pos_spec, pos_spec],
        out_specs=head_spec,
        out_shape=jax.ShapeDtypeStruct((N_NORM_HEADS, T, HEAD_DIM), bf16),
        scratch_shapes=[pltpu.VMEM((tm, D_MODEL), bf16)],
        compiler_params=_cparams(("parallel", "arbitrary")),
        name="inproj_norm",
    )(x2, attn_g, w_norm, gains, rope_c, rope_sa, rope_sb)
    Pr, gate = pl.pallas_call(
        _inproj_raw_kernel,
        grid=(T // tm, N_RAW_HEADS // HP),
        in_specs=x_specs,
        out_specs=[head_spec, pl.BlockSpec((2, tm, HEAD_DIM), lambda i, j: (0, i, 0))],
        out_shape=[
            jax.ShapeDtypeStruct((N_RAW_HEADS, T, HEAD_DIM), bf16),
            jax.ShapeDtypeStruct((2, T, HEAD_DIM), f32),
        ],
        scratch_shapes=[pltpu.VMEM((tm, D_MODEL), bf16)],
        compiler_params=_cparams(("parallel", "arbitrary")),
        name="inproj_raw",
    )(x2, attn_g, w_raw)
    return Pn, Pr, gate


def _gelu_tanh(x):
    return 0.5 * x * (1.0 + jnp.tanh(math.sqrt(2.0 / math.pi) * (x + 0.044715 * (x * x * x))))


def _compress_kernel(c_ref, pe_ref, w1_ref, w2_ref, gain_ref, rc_ref, rsa_ref, rsb_ref,
                     o_ref, *, do_norm):
    half = CMP_STRIDE * HEAD_DIM
    c = c_ref[0]
    nc = c.shape[0]
    w1 = w1_ref[...].astype(bf16)
    a = jnp.dot(c, w1[:half], preferred_element_type=f32)
    b = jnp.dot(c, w1[half:], preferred_element_type=f32)
    peb = jnp.dot(pe_ref[...], w1, preferred_element_type=f32)[0:1]
    hid = a + pltpu.roll(b, nc - 1, 0) + peb
    act = _gelu_tanh(hid)
    out = jnp.dot(act.astype(bf16), w2_ref[...].astype(bf16), preferred_element_type=f32)
    if do_norm:
        out = _rope(_rms(out, gain_ref[...]), rc_ref[...], rsa_ref[...], rsb_ref[...])
    o_ref[0] = out.astype(bf16)


def _compress(chunks, pe, w1, w2, gain, rc, rsa, rsb, do_norm):
    ng, nc, width = chunks.shape
    pe8 = jnp.broadcast_to(pe.reshape(1, CMP_LEN * HEAD_DIM), (8, CMP_LEN * HEAD_DIM)).astype(bf16)
    full = lambda shape: pl.BlockSpec(shape, lambda g: (0,) * len(shape))
    return pl.pallas_call(
        functools.partial(_compress_kernel, do_norm=do_norm),
        grid=(ng,),
        in_specs=[
            pl.BlockSpec((1, nc, width), lambda g: (g, 0, 0)),
            full((8, CMP_LEN * HEAD_DIM)),
            full((CMP_LEN * HEAD_DIM, CMP_HIDDEN)),
            full((CMP_HIDDEN, HEAD_DIM)),
            full((1, HEAD_DIM)),
            full((nc, HEAD_DIM)), full((nc, HEAD_DIM)), full((nc, HEAD_DIM)),
        ],
        out_specs=pl.BlockSpec((1, nc, HEAD_DIM), lambda g: (g, 0, 0)),
        out_shape=jax.ShapeDtypeStruct((ng, nc, HEAD_DIM), bf16),
        compiler_params=_cparams(("parallel",)),
        name="compress_k" if do_norm else "compress_v",
    )(chunks, pe8, w1, w2, gain, rc, rsa, rsb)


def _sigmoid(x):
    return 1.0 / (1.0 + jnp.exp(-x))


def _nsa_cmp_kernel(q_ref, kc_ref, vc_ref, g_ref, smt_ref, oc_ref, bias_ref, *, tq):
    q0 = pl.program_id(2) * tq
    kc = kc_ref[0]
    vc = vc_ref[0]
    nc = kc.shape[0]
    R = NSA_GROUP * tq
    sig = _sigmoid(g_ref[0])
    q = q_ref[...].reshape(R, HEAD_DIM)
    s = lax.dot_general(q, kc, (((1,), (1,)), ((), ())), preferred_element_type=f32)
    t = q0 + (lax.broadcasted_iota(i32, (R, nc), 0) & (tq - 1))
    cend = lax.broadcasted_iota(i32, (R, nc), 1) * CMP_STRIDE + (CMP_LEN - 1)
    mask = cend <= t
    s = jnp.where(mask, s, NEG_INF)
    m = jnp.max(s, axis=-1, keepdims=True)
    p = jnp.where(mask, jnp.exp(s - m), 0.0)
    p = p / jnp.maximum(jnp.sum(p, axis=-1, keepdims=True), 1e-30)
    o = jnp.dot(p.astype(bf16), vc, preferred_element_type=f32)
    for gi in range(NSA_GROUP):
        col = gi * N_BRANCH
        oc_ref[:, gi * HEAD_DIM:(gi + 1) * HEAD_DIM] = o[gi * tq:(gi + 1) * tq] * sig[:, col:col + 1]

    psum = p[0:tq] + p[tq:2 * tq] + p[2 * tq:3 * tq] + p[3 * tq:4 * tq]
    imp = lax.dot_general(smt_ref[...], psum, (((1,), (1,)), ((), ())),
                          precision=lax.Precision.HIGHEST, preferred_element_type=f32)
    nselp = imp.shape[0]
    jf = lax.broadcasted_iota(i32, (nselp, tq), 0).astype(f32)
    cur = ((q0 + lax.broadcasted_iota(i32, (nselp, tq), 1)) >> SEL_SHIFT).astype(f32)
    forced = (jf == 0.0) | (jf == cur) | (jf == cur - 1.0)
    imp = jnp.where(forced, FORCED_SCORE, imp)
    imp = jnp.where(jf <= cur, imp, NEG_INF)
    PICKED = -3e38
    work = imp
    for _ in range(N_SELECTED):
        mx = jnp.max(work, axis=0, keepdims=True)
        first = jnp.min(jnp.where(work == mx, jf, float(nselp)), axis=0, keepdims=True)
        work = jnp.where(jf == first, PICKED, work)
    bias_t = jnp.where((work < -2e38) & (imp > -1.0), 0.0, MASK_VAL)
    bias_ref[0] = bias_t.T.astype(bf16)


def _nsa_cmp(P, kcmp, vcmp, gate, smt, B, S, tq):
    T = B * S
    nqb = S // tq
    nc = kcmp.shape[1]
    nselp = smt.shape[0]
    return pl.pallas_call(
        functools.partial(_nsa_cmp_kernel, tq=tq),
        grid=(NSA_KV_HEADS, B, nqb),
        in_specs=[
            pl.BlockSpec((NSA_GROUP, tq, HEAD_DIM), lambda h, b, i: (h, b * nqb + i, 0)),
            pl.BlockSpec((1, nc, HEAD_DIM), lambda h, b, i: (h * B + b, 0, 0)),
            pl.BlockSpec((1, nc, HEAD_DIM), lambda h, b, i: (h * B + b, 0, 0)),
            pl.BlockSpec((1, tq, HEAD_DIM), lambda h, b, i: (h, b * nqb + i, 0)),
            pl.BlockSpec((nselp, nc), lambda h, b, i: (0, 0)),
        ],
        out_specs=[
            pl.BlockSpec((tq, NSA_GROUP * HEAD_DIM), lambda h, b, i: (b * nqb + i, h)),
            pl.BlockSpec((1, tq, nselp), lambda h, b, i: (h, b * nqb + i, 0)),
        ],
        out_shape=[
            jax.ShapeDtypeStruct((T, NSA_HEADS * HEAD_DIM), f32),
            jax.ShapeDtypeStruct((NSA_KV_HEADS, T, nselp), bf16),
        ],
        compiler_params=_cparams(("parallel", "parallel", "parallel")),
        name="nsa_cmp",
    )(P, kcmp, vcmp, gate, smt)


def _lane_tile(x, n):
    return x if n == 1 else jnp.concatenate([x] * n, axis=1)


ROW_CHUNK = 256


def _softmax_step(rows, s, v, m_ref, acc_ref, l_ref=None):
    tk = s.shape[1]
    m_prev = m_ref[rows, :]
    m_new = jnp.maximum(m_prev, jnp.max(s, axis=1, keepdims=True))
    alpha = jnp.exp(m_prev - m_new)
    p = jnp.exp(s - _lane_tile(m_new, tk // LANES))
    if l_ref is not None:
        l_ref[rows, :] = alpha * l_ref[rows, :] + jnp.sum(p, axis=1, keepdims=True)
    acc_ref[rows, :] = _lane_tile(alpha, 2) * acc_ref[rows, :] + jnp.dot(
        p.astype(bf16), v, preferred_element_type=f32)
    m_ref[rows, :] = m_new


def _nsa_sel_kernel(q_ref, bias_ref, ks_ref, vs_ref, kw_ref, vw_ref, g_ref, oc_ref,
                    o_ref, qaug_ref, m_ref, acc_ref, *, tq, tk, tkw):
    q0 = pl.program_id(2) * tq
    R = NSA_GROUP * tq
    nselp = bias_ref.shape[2]
    q = q_ref[...].reshape(R, HEAD_DIM)
    qaug_ref[:, 0:HEAD_DIM] = q
    bias = bias_ref[0]
    for gi in range(NSA_GROUP):
        qaug_ref[gi * tq:(gi + 1) * tq, HEAD_DIM:HEAD_DIM + nselp] = bias
    sig = _sigmoid(g_ref[0])
    rc = min(ROW_CHUNK, tq)

    def row_t(c, width):
        return q0 + ((c * rc + lax.broadcasted_iota(i32, (rc, width), 0)) & (tq - 1))

    def init():
        m_ref[...] = jnp.full(m_ref.shape, M_INIT, f32)
        acc_ref[...] = jnp.zeros(acc_ref.shape, f32)

    def finish():
        return acc_ref[:, 0:HEAD_DIM] / jnp.maximum(acc_ref[:, HEAD_DIM:2 * HEAD_DIM], 1e-30)

    init()
    ones_k = jnp.ones((tk, HEAD_DIM), bf16)

    def sel_step(kt, diagonal):
        k0 = pl.multiple_of(kt * tk, tk)
        k = ks_ref[0, pl.ds(k0, tk), :]
        blk = (k0 + lax.broadcasted_iota(i32, (tk, nselp), 0)) >> SEL_SHIFT
        onehot = jnp.where(blk == lax.broadcasted_iota(i32, (tk, nselp), 1), 1.0, 0.0).astype(bf16)
        kaug = jnp.concatenate([k, onehot], axis=1)
        vaug = jnp.concatenate([vs_ref[0, pl.ds(k0, tk), :], ones_k], axis=1)
        for c in range(R // rc):
            rows = slice(c * rc, (c + 1) * rc)
            s = lax.dot_general(qaug_ref[rows, :], kaug, (((1,), (1,)), ((), ())),
                                preferred_element_type=f32)
            if diagonal:
                s = jnp.where(k0 + lax.broadcasted_iota(i32, (rc, tk), 1) <= row_t(c, tk), s, MASK_VAL)
            _softmax_step(rows, s, vaug, m_ref, acc_ref)

    def sel_body(kt, carry):
        sel_step(kt, False)
        return carry

    n_full = q0 // tk
    lax.fori_loop(0, n_full, sel_body, 0)
    sel_step(n_full, True)
    o_s = finish()

    init()
    ones_w = jnp.ones((tkw, HEAD_DIM), bf16)
    qi = pl.program_id(2)

    def win_step(kt, mode):
        k0 = pl.multiple_of(kt * tkw, tkw)
        k = kw_ref[0, pl.ds(k0, tkw), :]
        vaug = jnp.concatenate([vw_ref[0, pl.ds(k0, tkw), :], ones_w], axis=1)
        for c in range(R // rc):
            rows = slice(c * rc, (c + 1) * rc)
            s = lax.dot_general(qaug_ref[rows, 0:HEAD_DIM], k, (((1,), (1,)), ((), ())),
                                preferred_element_type=f32)
            if mode != "full":
                t = row_t(c, tkw)
                kp = k0 + lax.broadcasted_iota(i32, (rc, tkw), 1)
                s = jnp.where(kp <= t if mode == "diagonal" else t - kp < WINDOW, s, MASK_VAL)
            _softmax_step(rows, s, vaug, m_ref, acc_ref)

    @pl.when(qi >= 2)
    def _():
        win_step(qi - 2, "edge")

    @pl.when(qi >= 1)
    def _():
        win_step(qi - 1, "full")

    win_step(qi, "diagonal")
    o_w = finish()

    for gi in range(NSA_GROUP):
        rows = slice(gi * tq, (gi + 1) * tq)
        cols = slice(gi * HEAD_DIM, (gi + 1) * HEAD_DIM)
        c = gi * N_BRANCH
        o_ref[:, cols] = (oc_ref[:, cols] + o_s[rows] * sig[:, c + 1:c + 2]
                          + o_w[rows] * sig[:, c + 2:c + 3]).astype(bf16)


def _nsa_sel(Pn, Pr, bias, gate, o_c, B, S, tq, tk, tkw):
    assert tk % tq == 0 and tkw == tq and WINDOW == 2 * tq
    T = B * S
    nqb = S // tq
    nselp = bias.shape[2]
    R = NSA_GROUP * tq
    kv_spec = lambda base: pl.BlockSpec((1, S, HEAD_DIM), lambda h, b, i: (base + h, b, 0))
    return pl.pallas_call(
        functools.partial(_nsa_sel_kernel, tq=tq, tk=tk, tkw=tkw),
        grid=(NSA_KV_HEADS, B, nqb),
        in_specs=[
            pl.BlockSpec((NSA_GROUP, tq, HEAD_DIM), lambda h, b, i: (h, b * nqb + i, 0)),
            pl.BlockSpec((1, tq, nselp), lambda h, b, i: (h, b * nqb + i, 0)),
            kv_spec(HD_KS), kv_spec(HR_VS), kv_spec(HD_KW), kv_spec(HR_VW),
            pl.BlockSpec((1, tq, HEAD_DIM), lambda h, b, i: (h, b * nqb + i, 0)),
            pl.BlockSpec((tq, NSA_GROUP * HEAD_DIM), lambda h, b, i: (b * nqb + i, h)),
        ],
        out_specs=pl.BlockSpec((tq, NSA_GROUP * HEAD_DIM), lambda h, b, i: (b * nqb + i, h)),
        out_shape=jax.ShapeDtypeStruct((T, NSA_HEADS * HEAD_DIM), bf16),
        scratch_shapes=[
            pltpu.VMEM((R, HEAD_DIM + nselp), bf16),
            pltpu.VMEM((R, LANES), f32),
            pltpu.VMEM((R, 2 * HEAD_DIM), f32),
        ],
        compiler_params=_cparams(("parallel", "parallel", "arbitrary")),
        name="nsa_sel_win",
    )(Pn, bias, Pn, Pr, Pn, Pr, gate, o_c)


def _diff_kernel(q_ref, k_ref, v_ref, lam_ref, sg_ref, o_ref, m_ref, l_ref, acc_ref, sa_ref, sb_ref,
                 *, tq, tk, lambda_init):
    q0 = pl.program_id(2) * tq
    R = 2 * tq
    rc = min(ROW_CHUNK, tq)
    m_ref[...] = jnp.full(m_ref.shape, M_INIT, f32)
    l_ref[...] = jnp.zeros(l_ref.shape, f32)
    acc_ref[...] = jnp.zeros(acc_ref.shape, f32)

    def scores(kt, s_ref):
        k0 = pl.multiple_of(kt * tk, tk)
        for comp in range(2):
            s_ref[comp * tq:(comp + 1) * tq, :] = lax.dot_general(
                q_ref[comp], k_ref[comp, pl.ds(k0, tk), :], (((1,), (1,)), ((), ())),
                preferred_element_type=f32)

    def softmax_pv(kt, s_ref, diagonal):
        k0 = pl.multiple_of(kt * tk, tk)
        v = jnp.concatenate([v_ref[0, pl.ds(k0, tk), :], v_ref[1, pl.ds(k0, tk), :]], axis=1)
        for c in range(R // rc):
            rows = slice(c * rc, (c + 1) * rc)
            s = s_ref[rows, :]
            if diagonal:
                t = q0 + ((c * rc + lax.broadcasted_iota(i32, (rc, tk), 0)) & (tq - 1))
                s = jnp.where(k0 + lax.broadcasted_iota(i32, (rc, tk), 1) <= t, s, MASK_VAL)
            _softmax_step(rows, s, v, m_ref, acc_ref, l_ref)

    def pair(j, carry):
        scores(2 * j + 1, sb_ref)
        softmax_pv(2 * j, sa_ref, False)
        scores(2 * j + 2, sa_ref)
        softmax_pv(2 * j + 1, sb_ref, False)
        return carry

    qi = pl.program_id(2)
    scores(0, sa_ref)
    lax.fori_loop(0, qi // 2, pair, 0)

    @pl.when(qi % 2 == 0)
    def _():
        softmax_pv(qi, sa_ref, True)

    @pl.when(qi % 2 == 1)
    def _():
        scores(qi, sb_ref)
        softmax_pv(qi - 1, sa_ref, False)
        softmax_pv(qi, sb_ref, True)

    lq = lam_ref[...]
    lam = (jnp.exp(jnp.sum(lq[0:1] * lq[1:2], axis=1, keepdims=True))
           - jnp.exp(jnp.sum(lq[2:3] * lq[3:4], axis=1, keepdims=True)) + lambda_init)
    o = acc_ref[...] / _lane_tile(jnp.maximum(l_ref[...], 1e-30), 2)
    a = o[0:tq] - lam * o[tq:R]
    o_ref[...] = (_rms(a, sg_ref[...]) * (1.0 - lambda_init)).astype(bf16)


def _diff_attn(Pn, Pr, lam_vecs, subln_g, B, S, tq, tk, lambda_init):
    assert tq == tk
    T = B * S
    nqb = S // tq
    return pl.pallas_call(
        functools.partial(_diff_kernel, tq=tq, tk=tk, lambda_init=lambda_init),
        grid=(DIFF_HEADS, B, nqb),
        in_specs=[
            pl.BlockSpec((2, tq, HEAD_DIM), lambda h, b, i: (HD_QD // 2 + h, b * nqb + i, 0)),
            pl.BlockSpec((2, S, HEAD_DIM), lambda h, b, i: (HD_KD // 2 + h, b, 0)),
            pl.BlockSpec((2, S, HEAD_DIM), lambda h, b, i: (HR_VD // 2 + h, b, 0)),
            pl.BlockSpec((8, HEAD_DIM), lambda h, b, i: (0, 0)),
            pl.BlockSpec((1, DIFF_V_DIM), lambda h, b, i: (0, 0)),
        ],
        out_specs=pl.BlockSpec((tq, DIFF_V_DIM), lambda h, b, i: (b * nqb + i, h)),
        out_shape=jax.ShapeDtypeStruct((T, DIFF_HEADS * DIFF_V_DIM), bf16),
        scratch_shapes=[
            pltpu.VMEM((2 * tq, LANES), f32),
            pltpu.VMEM((2 * tq, LANES), f32),
            pltpu.VMEM((2 * tq, DIFF_V_DIM), f32),
            pltpu.VMEM((2 * tq, tk), f32),
            pltpu.VMEM((2 * tq, tk), f32),
        ],
        compiler_params=_cparams(("parallel", "parallel", "arbitrary")),
        name="diff_attn",
    )(Pn, Pn, Pr, lam_vecs, subln_g)


def _outproj_kernel(x_ref, on_ref, od_ref, w_ref, g_ref, wr_ref, br_ref,
                    h_ref, hn_ref, lg_ref):
    half = NSA_HEADS * HEAD_DIM
    tm = x_ref.shape[0]
    rc = min(ROW_CHUNK, tm)
    for c in range(tm // rc):
        rows = slice(c * rc, (c + 1) * rc)
        h = x_ref[rows, :] + jnp.dot(on_ref[rows, :], w_ref[0:half], preferred_element_type=f32) \
            + jnp.dot(od_ref[rows, :], w_ref[half:], preferred_element_type=f32)
        h_ref[rows, :] = h
        hn = _rms(h, g_ref[...])
        hn_ref[rows, :] = hn
        hn_hi = hn.astype(bf16)
        hn_lo = (hn - hn_hi.astype(f32)).astype(bf16)
        r = (jnp.dot(hn_hi, wr_ref[...], preferred_element_type=f32)
             + jnp.dot(hn_lo, wr_ref[...], preferred_element_type=f32))
        lg_ref[rows, :] = r[:, 0:LANES] + r[:, LANES:2 * LANES] + br_ref[...]


def _outproj(x2, o_nsa, o_diff, w_out_b, ffn_g, w_router_p, b_router_p, tm):
    T = x2.shape[0]
    full = lambda shape: pl.BlockSpec(shape, lambda i: (0,) * len(shape))
    return pl.pallas_call(
        _outproj_kernel,
        grid=(T // tm,),
        in_specs=[
            pl.BlockSpec((tm, D_MODEL), lambda i: (i, 0)),
            pl.BlockSpec((tm, NSA_HEADS * HEAD_DIM), lambda i: (i, 0)),
            pl.BlockSpec((tm, DIFF_HEADS * DIFF_V_DIM), lambda i: (i, 0)),
            full((D_MODEL, D_MODEL)),
            full((1, D_MODEL)),
            full((D_MODEL, 2 * LANES)),
            full((1, LANES)),
        ],
        out_specs=[
            pl.BlockSpec((tm, D_MODEL), lambda i: (i, 0)),
            pl.BlockSpec((tm, D_MODEL), lambda i: (i, 0)),
            pl.BlockSpec((tm, LANES), lambda i: (i, 0)),
        ],
        out_shape=[
            jax.ShapeDtypeStruct((T, D_MODEL), f32),
            jax.ShapeDtypeStruct((T, D_MODEL), f32),
            jax.ShapeDtypeStruct((T, LANES), f32),
        ],
        compiler_params=_cparams(("parallel",)),
        name="outproj_router",
    )(x2, o_nsa, o_diff, w_out_b, ffn_g, w_router_p, b_router_p)


ISSUE_UNROLL = 8


def _start_row_gathers(idx_fn, n, src_ref, dst_ref, sem):
    def body(c, carry):
        for u in range(ISSUE_UNROLL):
            r = c * ISSUE_UNROLL + u
            pltpu.make_async_copy(src_ref.at[pl.ds(idx_fn(r), 1)], dst_ref.at[pl.ds(r, 1)], sem).start()
        return carry

    lax.fori_loop(0, n // ISSUE_UNROLL, body, 0)


def _wait_row_gathers(n, src_ref, dst_ref, sem):
    for r in range(n):
        pltpu.make_async_copy(src_ref.at[pl.ds(0, 1)], dst_ref.at[pl.ds(r, 1)], sem).wait()


def _gather_kernel(st_ref, sb_ref, src_ref, o_ref, buf_ref, sem, *, tm, n_src):
    b = pl.program_id(0)
    nb = pl.num_programs(0)

    def issue(blk, slot):
        base = sb_ref[blk]
        _start_row_gathers(lambda r: st_ref[jnp.minimum(base + r, n_src - 1)], tm,
                           src_ref, buf_ref.at[slot], sem.at[slot])

    @pl.when(b == 0)
    def _():
        issue(0, 0)

    @pl.when(b + 1 < nb)
    def _():
        issue(b + 1, (b + 1) % 2)

    slot = b % 2
    _wait_row_gathers(tm, src_ref, buf_ref.at[slot], sem.at[slot])
    o_ref[...] = buf_ref[slot].astype(o_ref.dtype)


def _gather_rows(sorted_tok, src_base, src, n_blocks, tm):
    return pl.pallas_call(
        functools.partial(_gather_kernel, tm=tm, n_src=sorted_tok.shape[0]),
        grid_spec=pltpu.PrefetchScalarGridSpec(
            num_scalar_prefetch=2,
            grid=(n_blocks,),
            in_specs=[pl.BlockSpec(memory_space=pl.ANY)],
            out_specs=pl.BlockSpec((tm, src.shape[1]), lambda b, st, sb: (b, 0)),
            scratch_shapes=[pltpu.VMEM((2, tm, src.shape[1]), src.dtype), pltpu.SemaphoreType.DMA((2,))],
        ),
        out_shape=jax.ShapeDtypeStruct((n_blocks * tm, src.shape[1]), bf16),
        compiler_params=_cparams(("arbitrary",)),
        name="moe_gather",
    )(sorted_tok, src_base, src)


def _expert_row_blocks(first, nblk, x_hbm, o_dst, xbuf, obuf, xsem, osem, tm, compute):
    def x_copy(blk, slot):
        return pltpu.make_async_copy(x_hbm.at[pl.ds(pl.multiple_of(blk * tm, tm), tm)],
                                     xbuf.at[slot], xsem.at[slot])

    def o_copy(blk, slot):
        return pltpu.make_async_copy(obuf.at[slot], o_dst(pl.multiple_of(blk * tm, tm)), osem.at[slot])

    @pl.when(nblk > 0)
    def _():
        x_copy(first, 0).start()

    def body(i, carry):
        slot = i % 2

        @pl.when(i + 1 < nblk)
        def _():
            x_copy(first + i + 1, 1 - slot).start()

        x_copy(first + i, slot).wait()

        @pl.when(i >= 2)
        def _():
            o_copy(first + i - 2, slot).wait()

        obuf[slot] = compute(xbuf[slot])
        o_copy(first + i, slot).start()
        return carry

    lax.fori_loop(0, nblk, body, 0)

    @pl.when(nblk >= 2)
    def _():
        o_copy(first + nblk - 2, nblk % 2).wait()

    @pl.when(nblk >= 1)
    def _():
        o_copy(first + nblk - 1, (nblk - 1) % 2).wait()

    return o_copy


def _zero_unused_blocks(o_copy, obuf, lo, hi):
    obuf[0] = jnp.zeros(obuf.shape[1:], obuf.dtype)

    def body(blk, carry):
        cp = o_copy(blk, 0)
        cp.start()
        cp.wait()
        return carry

    lax.fori_loop(lo, hi, body, 0)


def _gateup_kernel(fb_ref, nbk_ref, nu_ref, st_ref, sb_ref, hn_hbm, wg_ref, wu_ref, bg_ref, bu_ref, hb_hbm,
                   wgb_ref, wub_ref, xa_ref, xb_ref, obuf, xsem, osem, *, tm, tn, n_blocks, n_src):
    j = pl.program_id(0)
    e = pl.program_id(1)
    wgb_ref[...] = wg_ref[0].astype(bf16)
    wub_ref[...] = wu_ref[0].astype(bf16)
    col0 = pl.multiple_of(j * tn, tn)
    first = fb_ref[e]
    nblk = nbk_ref[e]
    xbufs = (xa_ref, xb_ref)

    def compute(x):
        gl = jnp.dot(x, wgb_ref[...], preferred_element_type=f32) + bg_ref[0]
        up = jnp.dot(x, wub_ref[...], preferred_element_type=f32) + bu_ref[0]
        gl = jnp.minimum(gl, SWIGLU_LIMIT)
        up = jnp.clip(up, -SWIGLU_LIMIT, SWIGLU_LIMIT)
        return ((up + 1.0) * (gl * _sigmoid(SWIGLU_ALPHA * gl))).astype(bf16)

    def issue(g, p):
        base = sb_ref[g]
        for r in range(tm):
            tok = st_ref[jnp.minimum(base + r, n_src - 1)]
            pltpu.make_async_copy(hn_hbm.at[pl.ds(tok, 1)], xbufs[p].at[pl.ds(r, 1)], xsem.at[p]).start()

    def wait_x(p):
        _wait_row_gathers(tm, hn_hbm, xbufs[p], xsem.at[p])

    def o_copy(g, slot):
        return pltpu.make_async_copy(
            obuf.at[slot], hb_hbm.at[pl.ds(pl.multiple_of(g * tm, tm), tm), pl.ds(col0, tn)], osem.at[slot])

    @pl.when(e == 0)
    def _():
        issue(0, 0)

    def body(i, carry):
        g = first + i
        oslot = i % 2

        @pl.when(i >= 2)
        def _():
            o_copy(g - 2, oslot).wait()

        for p in range(2):
            @pl.when(g % 2 == p)
            def _():
                wait_x(p)
                issue(g + 1, 1 - p)
                obuf[oslot] = compute(xbufs[p][...].astype(bf16))

        o_copy(g, oslot).start()
        return carry

    lax.fori_loop(0, nblk, body, 0)

    @pl.when(nblk >= 2)
    def _():
        o_copy(first + nblk - 2, nblk % 2).wait()

    @pl.when(nblk >= 1)
    def _():
        o_copy(first + nblk - 1, (nblk - 1) % 2).wait()

    @pl.when(e == N_EXPERTS - 1)
    def _():
        nu = nu_ref[0]
        for p in range(2):
            @pl.when(nu % 2 == p)
            def _():
                wait_x(p)
        _zero_unused_blocks(o_copy, obuf, nu, n_blocks)


def _gateup(first_blk, n_blk, n_used, sorted_tok, src_base, hn, w_gate_up, b_gate_up3, n_blocks, tm, tn):
    nj = D_FF // tn
    return pl.pallas_call(
        functools.partial(_gateup_kernel, tm=tm, tn=tn, n_blocks=n_blocks, n_src=sorted_tok.shape[0]),
        grid_spec=pltpu.PrefetchScalarGridSpec(
            num_scalar_prefetch=5,
            grid=(nj, N_EXPERTS),
            in_specs=[
                pl.BlockSpec(memory_space=pl.ANY),
                pl.BlockSpec((1, D_MODEL, tn), lambda j, e, *_: (e, 0, j)),
                pl.BlockSpec((1, D_MODEL, tn), lambda j, e, *_: (e, 0, nj + j)),
                pl.BlockSpec((1, 1, tn), lambda j, e, *_: (e, 0, j)),
                pl.BlockSpec((1, 1, tn), lambda j, e, *_: (e, 0, nj + j)),
            ],
            out_specs=pl.BlockSpec(memory_space=pl.ANY),
            scratch_shapes=[
                pltpu.VMEM((D_MODEL, tn), bf16), pltpu.VMEM((D_MODEL, tn), bf16),
                pltpu.VMEM((tm, D_MODEL), hn.dtype), pltpu.VMEM((tm, D_MODEL), hn.dtype),
                pltpu.VMEM((2, tm, tn), bf16),
                pltpu.SemaphoreType.DMA((2,)), pltpu.SemaphoreType.DMA((2,)),
            ],
        ),
        out_shape=jax.ShapeDtypeStruct((n_blocks * tm, D_FF), bf16),
        compiler_params=_cparams(("arbitrary", "arbitrary")),
        name="moe_gate_up",
    )(first_blk, n_blk, n_used, sorted_tok, src_base, hn, w_gate_up, w_gate_up, b_gate_up3, b_gate_up3)


def _down_kernel(fb_ref, nbk_ref, nu_ref, hb_hbm, w_ref, b_ref, y_hbm,
                 wb_ref, xbuf, obuf, xsem, osem, *, tm, n_blocks):
    e = pl.program_id(0)
    wb_ref[...] = w_ref[0].astype(bf16)

    def compute(x):
        return jnp.dot(x, wb_ref[...], preferred_element_type=f32) + b_ref[0]

    o_copy = _expert_row_blocks(fb_ref[e], nbk_ref[e], hb_hbm, lambda r0: y_hbm.at[pl.ds(r0, tm)],
                                xbuf, obuf, xsem, osem, tm, compute)

    @pl.when(e == N_EXPERTS - 1)
    def _():
        _zero_unused_blocks(o_copy, obuf, nu_ref[0], n_blocks)


def _down(first_blk, n_blk, n_used, hb, w_down, b_down3, tm):
    n_rows = hb.shape[0]
    return pl.pallas_call(
        functools.partial(_down_kernel, tm=tm, n_blocks=n_rows // tm),
        grid_spec=pltpu.PrefetchScalarGridSpec(
            num_scalar_prefetch=3,
            grid=(N_EXPERTS,),
            in_specs=[
                pl.BlockSpec(memory_space=pl.ANY),
                pl.BlockSpec((1, D_FF, D_MODEL), lambda e, *_: (e, 0, 0)),
                pl.BlockSpec((1, 1, D_MODEL), lambda e, *_: (e, 0, 0)),
            ],
            out_specs=pl.BlockSpec(memory_space=pl.ANY),
            scratch_shapes=[
                pltpu.VMEM((D_FF, D_MODEL), bf16),
                pltpu.VMEM((2, tm, D_FF), bf16), pltpu.VMEM((2, tm, D_MODEL), f32),
                pltpu.SemaphoreType.DMA((2,)), pltpu.SemaphoreType.DMA((2,)),
            ],
        ),
        out_shape=jax.ShapeDtypeStruct((n_rows, D_MODEL), f32),
        compiler_params=_cparams(("arbitrary",)),
        name="moe_down",
    )(first_blk, n_blk, n_used, hb, w_down, b_down3)


def _combine_kernel(pos_ref, h_ref, gt_ref, y_ref, o_ref, buf_ref, sem, *, tc):
    i = pl.program_id(0)
    n = pl.num_programs(0)
    rows = TOP_K * tc

    def issue(step, slot):
        base = step * rows
        _start_row_gathers(lambda r: pos_ref[base + r], rows, y_ref, buf_ref.at[slot], sem.at[slot])

    @pl.when(i == 0)
    def _():
        issue(0, 0)

    @pl.when(i + 1 < n)
    def _():
        issue(i + 1, (i + 1) % 2)

    slot = i % 2
    _wait_row_gathers(rows, y_ref, buf_ref.at[slot], sem.at[slot])
    gt = gt_ref[...]
    out = h_ref[...]
    for k in range(TOP_K):
        out = out + gt[:, k:k + 1] * buf_ref[slot, k * tc:(k + 1) * tc, :]
    o_ref[...] = out


def _combine(pos_blocks, h, gate_pad, y, tc):
    T = h.shape[0]
    return pl.pallas_call(
        functools.partial(_combine_kernel, tc=tc),
        grid_spec=pltpu.PrefetchScalarGridSpec(
            num_scalar_prefetch=1,
            grid=(T // tc,),
            in_specs=[
                pl.BlockSpec((tc, D_MODEL), lambda i, pos: (i, 0)),
                pl.BlockSpec((tc, LANES), lambda i, pos: (i, 0)),
                pl.BlockSpec(memory_space=pl.ANY),
            ],
            out_specs=pl.BlockSpec((tc, D_MODEL), lambda i, pos: (i, 0)),
            scratch_shapes=[pltpu.VMEM((2, TOP_K * tc, D_MODEL), f32), pltpu.SemaphoreType.DMA((2,))],
        ),
        out_shape=jax.ShapeDtypeStruct((T, D_MODEL), f32),
        compiler_params=_cparams(("arbitrary",)),
        name="moe_combine",
    )(pos_blocks, h, gate_pad, y)


def _rearranged_w_in(w_in):
    splits = np.cumsum([0, 1024, 256, 256, 256, 256, 256, 256, 24, 1024, 1024, 1024])
    q, kc, vc, ks, vs, kw, vw, g, qd, kd, vd = [w_in[:, splits[n]:splits[n + 1]] for n in range(11)]
    gcols = NSA_GROUP * N_BRANCH
    zpad = lambda n: jnp.zeros((D_MODEL, n), w_in.dtype)
    w_norm = jnp.concatenate([q, ks, kw, qd, kd], axis=1)
    w_raw = jnp.concatenate([kc, vc, vs, vw, vd,
                             g[:, :gcols], zpad(HEAD_DIM - gcols), g[:, gcols:], zpad(HEAD_DIM - gcols),
                             zpad((N_RAW_HEADS - HR_GATE - 2) * HEAD_DIM)], axis=1)
    return w_norm.astype(bf16), w_raw.astype(bf16)


def _sel_map_t(S, nselp):
    n_chunks = S // CMP_STRIDE
    n_cmp = (S - CMP_LEN) // CMP_STRIDE + 1
    n_sel = S // SEL_BLOCK
    c_start = np.arange(n_chunks) * CMP_STRIDE
    s_start = np.arange(nselp) * SEL_BLOCK
    ov = (c_start[None, :] < s_start[:, None] + SEL_BLOCK) & (c_start[None, :] + CMP_LEN > s_start[:, None])
    ov &= (np.arange(n_chunks)[None, :] < n_cmp) & (np.arange(nselp)[:, None] < n_sel)
    return jnp.asarray(ov, f32)


def _attention_block(x2, B, S, attn_norm_g, w_in, nsa_q_norm_g, nsa_k_norm_g, pos_k, pos_v,
                     k_w1, k_w2, v_w1, v_w2, diff_q_norm_g, diff_k_norm_g, lq1, lk1, lq2, lk2,
                     subln_g, lambda_init):
    T = B * S
    scale = HEAD_DIM ** -0.5
    tm = min(1024, S)
    rope_c, rope_sa, rope_sb = _rope_tables(np.arange(S))
    gains = jnp.concatenate([
        jnp.broadcast_to(nsa_q_norm_g * scale, (NSA_HEADS, HEAD_DIM)),
        jnp.broadcast_to(nsa_k_norm_g[1], (NSA_KV_HEADS, HEAD_DIM)),
        jnp.broadcast_to(nsa_k_norm_g[2], (NSA_KV_HEADS, HEAD_DIM)),
        jnp.broadcast_to(diff_q_norm_g * scale, (2 * DIFF_HEADS, HEAD_DIM)),
        jnp.broadcast_to(diff_k_norm_g, (2 * DIFF_HEADS, HEAD_DIM)),
    ], axis=0).reshape(N_NORM_HEADS, 1, HEAD_DIM)
    w_norm, w_raw = _rearranged_w_in(w_in)
    Pn, Pr, gate = _inproj(x2, attn_norm_g.reshape(1, D_MODEL), w_norm, w_raw, gains,
                           rope_c, rope_sa, rope_sb, S, tm)

    n_chunks = S // CMP_STRIDE
    cmp_pos = np.arange(n_chunks) * CMP_STRIDE + (CMP_LEN - 1)
    cc, csa, csb = _rope_tables(cmp_pos)
    chunk_view = lambda hd: Pr[hd:hd + NSA_KV_HEADS].reshape(NSA_KV_HEADS * B, n_chunks, CMP_STRIDE * HEAD_DIM)
    kcmp = _compress(chunk_view(HR_KC), pos_k, k_w1, k_w2, nsa_k_norm_g[0].reshape(1, HEAD_DIM),
                     cc, csa, csb, True)
    vcmp = _compress(chunk_view(HR_VC), pos_v, v_w1, v_w2, nsa_k_norm_g[0].reshape(1, HEAD_DIM),
                     cc, csa, csb, False)

    nselp = -(-(S // SEL_BLOCK) // LANES) * LANES
    tq = min(256, S)
    o_c, bias = _nsa_cmp(Pn, kcmp, vcmp, gate, _sel_map_t(S, nselp), B, S, tq)
    o_nsa = _nsa_sel(Pn, Pr, bias, gate, o_c, B, S, tq, min(512, S), min(256, S))

    lam_vecs = jnp.concatenate([lq1[None], lk1[None], lq2[None], lk2[None],
                                jnp.zeros((4, HEAD_DIM), f32)], axis=0)
    o_diff = _diff_attn(Pn, Pr, lam_vecs, subln_g.reshape(1, DIFF_V_DIM), B, S,
                        min(512, S), min(512, S), lambda_init)
    return o_nsa, o_diff


def _moe_block(h, hn, logits, w_gate_up, b_gate_up, w_down, b_down, tm=256):
    T = h.shape[0]
    TK = T * TOP_K
    top_v, top_e = lax.top_k(logits[:, :N_EXPERTS], TOP_K)
    gate = jax.nn.softmax(top_v, axis=-1)
    flat_e = top_e.reshape(TK).astype(i32)
    se, order = lax.sort((flat_e, jnp.arange(TK, dtype=i32)), num_keys=1, is_stable=True)
    st = order // TOP_K
    experts = jnp.arange(N_EXPERTS, dtype=i32)
    counts = jnp.sum((flat_e[:, None] == experts[None, :]).astype(i32), axis=0)
    padded = (counts + tm - 1) // tm * tm
    pad_end = jnp.cumsum(padded)
    pad_start = pad_end - padded
    cnt_start = jnp.cumsum(counts) - counts
    dest = pad_start[se] + (jnp.arange(TK, dtype=i32) - cnt_start[se])
    _, pos = lax.sort((order, dest), num_keys=1)
    n_blocks = -(-TK // tm) + N_EXPERTS
    blk_start = jnp.arange(n_blocks, dtype=i32) * tm
    blk_e = jnp.minimum(jnp.sum((pad_end[None, :] <= blk_start[:, None]).astype(i32), axis=1), N_EXPERTS - 1)
    src_base = cnt_start[blk_e] + (blk_start - pad_start[blk_e])
    n_used = (pad_end[-1] // tm).astype(i32).reshape(1)

    src_base = jnp.concatenate([src_base, src_base[-1:]]).astype(i32)
    first_blk = (pad_start // tm).astype(i32)
    n_blk = (padded // tm).astype(i32)
    hb = _gateup(first_blk, n_blk, n_used, st, src_base, hn, w_gate_up,
                 b_gate_up.reshape(N_EXPERTS, 1, 2 * D_FF), n_blocks, tm, 1024)
    y = _down(first_blk, n_blk, n_used, hb, w_down, b_down.reshape(N_EXPERTS, 1, D_MODEL), tm)

    tc = 64
    pos_blocks = pos.reshape(T // tc, tc, TOP_K).transpose(0, 2, 1).reshape(TK)
    gate_pad = jnp.pad(gate, ((0, 0), (0, LANES - TOP_K)))
    return _combine(pos_blocks, h, gate_pad, y, tc)


def kernel(x, attn_norm_g, w_in, nsa_q_norm_g, nsa_k_norm_g, nsa_cmp_pos_k, nsa_cmp_pos_v, nsa_cmp_k_w1, nsa_cmp_k_w2, nsa_cmp_v_w1, nsa_cmp_v_w2, diff_q_norm_g, diff_k_norm_g, diff_lambda_q1, diff_lambda_k1, diff_lambda_q2, diff_lambda_k2, diff_subln_g, w_out, ffn_norm_g, w_router, b_router, w_gate_up, b_gate_up, w_down, b_down):
    B, S, _ = x.shape
    T = B * S
    depth = attn_norm_g.shape[0]
    h = x.reshape(T, D_MODEL)
    for l in range(depth):
        lambda_init = 0.8 - 0.6 * math.exp(-0.3 * l)
        o_nsa, o_diff = _attention_block(
            h, B, S, attn_norm_g[l], w_in[l], nsa_q_norm_g[l], nsa_k_norm_g[l], nsa_cmp_pos_k[l],
            nsa_cmp_pos_v[l], nsa_cmp_k_w1[l], nsa_cmp_k_w2[l], nsa_cmp_v_w1[l], nsa_cmp_v_w2[l],
            diff_q_norm_g[l], diff_k_norm_g[l], diff_lambda_q1[l], diff_lambda_k1[l],
            diff_lambda_q2[l], diff_lambda_k2[l], diff_subln_g[l], lambda_init)
        w_router_f = jnp.pad(w_router[l], ((0, 0), (0, LANES - N_EXPERTS)))
        w_router_hi = w_router_f.astype(bf16)
        w_router_lo = (w_router_f - w_router_hi.astype(f32)).astype(bf16)
        w_router_p = jnp.concatenate([w_router_hi, w_router_lo], axis=1)
        b_router_p = jnp.pad(b_router[l], (0, LANES - N_EXPERTS)).reshape(1, LANES)
        h_mid, hn, logits = _outproj(h, o_nsa, o_diff, w_out[l].astype(bf16),
                                     ffn_norm_g[l].reshape(1, D_MODEL), w_router_p, b_router_p,
                                     min(512, T))
        h = _moe_block(h_mid, hn, logits, w_gate_up[l], b_gate_up[l], w_down[l], b_down[l])
    return h.reshape(B, S, D_MODEL)
```

```python
import functools
import math

import numpy as np
import jax
import jax.numpy as jnp
from jax import lax
from jax.experimental import pallas as pl
from jax.experimental.pallas import tpu as pltpu

f32 = jnp.float32
bf16 = jnp.bfloat16
i32 = jnp.int32

D_MODEL = 2048
HEAD_DIM = 128
ROT_DIM = HEAD_DIM // 4
ROPE_THETA = 500000.0
NORM_EPS = 1e-6
NEG_INF = -1e30

NSA_HEADS = 8
NSA_KV_HEADS = 2
NSA_GROUP = NSA_HEADS // NSA_KV_HEADS
N_BRANCH = 3
CMP_LEN = 32
CMP_STRIDE = 16
CMP_HIDDEN = 2 * HEAD_DIM
SEL_BLOCK = 64
SEL_SHIFT = 6
N_SELECTED = 16
WINDOW = 512
FORCED_SCORE = 1e9

DIFF_HEADS = 4
DIFF_V_DIM = 2 * HEAD_DIM

N_EXPERTS = 32
TOP_K = 4
D_FF = D_MODEL
SWIGLU_ALPHA = 1.702
SWIGLU_LIMIT = 7.0

LANES = 128

HD_Q = 0
HD_KS = 8
HD_KW = 10
HD_QD = 12
HD_KD = 20
N_NORM_HEADS = 28
HR_KC = 0
HR_VC = 2
HR_VS = 4
HR_VW = 6
HR_VD = 8
HR_GATE = 16
N_RAW_HEADS = 20
HP = 4

MASK_VAL = -1e30
M_INIT = -5e29

VMEM_LIMIT = 56 * 1024 * 1024


def _tiles(S):
    return dict(
        proj_rows=min(1024, S),
        attn_q=min(256, S),
        sel_keys=min(1024, S),
        diff_q=min(512, S),
        out_rows=512,
        moe_rows=256,
        moe_cols=1024,
        combine_rows=64,
    )


def _cparams(sem):
    return pltpu.CompilerParams(dimension_semantics=sem, vmem_limit_bytes=VMEM_LIMIT)


def _rope_tables(pos):
    inv = np.power(ROPE_THETA, -np.arange(0, ROT_DIM, 2, dtype=np.float64) / ROT_DIM)
    ang = pos.astype(np.float64)[:, None] * inv[None, :]
    cos, sin = np.cos(ang), np.sin(ang)
    n = pos.shape[0]
    half = ROT_DIM // 2
    c = np.concatenate([cos, cos, np.ones((n, HEAD_DIM - ROT_DIM))], axis=1)
    sa = np.concatenate([-sin, np.zeros((n, HEAD_DIM - half))], axis=1)
    sb = np.concatenate([np.zeros((n, half)), sin, np.zeros((n, HEAD_DIM - ROT_DIM))], axis=1)
    return (jnp.asarray(c, f32), jnp.asarray(sa, f32), jnp.asarray(sb, f32))


def _rope(y, c, sa, sb):
    half = ROT_DIM // 2
    return (y * c + pltpu.roll(y, HEAD_DIM - half, 1) * sa + pltpu.roll(y, half, 1) * sb)


def _rms(y, gain):
    ms = jnp.mean(y * y, axis=-1, keepdims=True)
    return y * lax.rsqrt(ms + NORM_EPS) * gain


def _inproj_norm_kernel(x_ref, g_ref, w_ref, gain_ref, c_ref, sa_ref, sb_ref, o_ref, xn_ref):
    @pl.when(pl.program_id(1) == 0)
    def _():
        xn_ref[...] = _rms(x_ref[...], g_ref[...]).astype(bf16)

    tm = x_ref.shape[0]
    rc = min(ROW_CHUNK, tm)
    for r in range(tm // rc):
        rows = slice(r * rc, (r + 1) * rc)
        acc = jnp.dot(xn_ref[rows, :], w_ref[...], preferred_element_type=f32)
        c, sa, sb = c_ref[rows, :], sa_ref[rows, :], sb_ref[rows, :]
        for u in range(HP):
            y = _rms(acc[:, u * HEAD_DIM:(u + 1) * HEAD_DIM], gain_ref[u])
            o_ref[u, rows, :] = _rope(y, c, sa, sb).astype(bf16)


def _inproj_raw_kernel(x_ref, g_ref, w_ref, o_ref, gate_ref, xn_ref):
    j = pl.program_id(1)

    @pl.when(j == 0)
    def _():
        xn_ref[...] = _rms(x_ref[...], g_ref[...]).astype(bf16)

    tm = x_ref.shape[0]
    rc = min(ROW_CHUNK, tm)
    for r in range(tm // rc):
        rows = slice(r * rc, (r + 1) * rc)
        acc = jnp.dot(xn_ref[rows, :], w_ref[...], preferred_element_type=f32)
        for u in range(HP):
            o_ref[u, rows, :] = acc[:, u * HEAD_DIM:(u + 1) * HEAD_DIM].astype(bf16)

    @pl.when(j == HR_GATE // HP)
    def _():
        gate_ref[0] = jnp.dot(xn_ref[...], w_ref[:, 0:HEAD_DIM], preferred_element_type=f32)
        gate_ref[1] = jnp.dot(xn_ref[...], w_ref[:, HEAD_DIM:2 * HEAD_DIM], preferred_element_type=f32)


def _inproj(x2, attn_g, w_norm, w_raw, gains, rope_c, rope_sa, rope_sb, S, tm):
    T = x2.shape[0]
    nsb = S // tm
    x_specs = [
        pl.BlockSpec((tm, D_MODEL), lambda i, j: (i, 0)),
        pl.BlockSpec((1, D_MODEL), lambda i, j: (0, 0)),
        pl.BlockSpec((D_MODEL, HP * HEAD_DIM), lambda i, j: (0, j)),
    ]
    head_spec = pl.BlockSpec((HP, tm, HEAD_DIM), lambda i, j: (j, i, 0))
    pos_spec = pl.BlockSpec((tm, HEAD_DIM), lambda i, j: (i % nsb, 0))
    Pn = pl.pallas_call(
        _inproj_norm_kernel,
        grid=(T // tm, N_NORM_HEADS // HP),
        in_specs=x_specs + [pl.BlockSpec((HP, 1, HEAD_DIM), lambda i, j: (j, 0, 0)),
                            pos_spec, pos_spec, pos_spec],
        out_specs=head_spec,
        out_shape=jax.ShapeDtypeStruct((N_NORM_HEADS, T, HEAD_DIM), bf16),
        scratch_shapes=[pltpu.VMEM((tm, D_MODEL), bf16)],
        compiler_params=_cparams(("parallel", "arbitrary")),
        name="inproj_norm",
    )(x2, attn_g, w_norm, gains, rope_c, rope_sa, rope_sb)
    Pr, gate = pl.pallas_call(
        _inproj_raw_kernel,
        grid=(T // tm, N_RAW_HEADS // HP),
        in_specs=x_specs,
        out_specs=[head_spec, pl.BlockSpec((2, tm, HEAD_DIM), lambda i, j: (0, i, 0))],
        out_shape=[
            jax.ShapeDtypeStruct((N_RAW_HEADS, T, HEAD_DIM), bf16),
            jax.ShapeDtypeStruct((2, T, HEAD_DIM), f32),
        ],
        scratch_shapes=[pltpu.VMEM((tm, D_MODEL), bf16)],
        compiler_params=_cparams(("parallel", "arbitrary")),
        name="inproj_raw",
    )(x2, attn_g, w_raw)
    return Pn, Pr, gate


def _gelu_tanh(x):
    return 0.5 * x * (1.0 + jnp.tanh(math.sqrt(2.0 / math.pi) * (x + 0.044715 * (x * x * x))))


def _compress_kernel(c_ref, pe_ref, w1_ref, w2_ref, gain_ref, rc_ref, rsa_ref, rsb_ref,
                     o_ref, *, do_norm):
    half = CMP_STRIDE * HEAD_DIM
    c = c_ref[0]
    nc = c.shape[0]
    w1 = w1_ref[...].astype(bf16)
    a = jnp.dot(c, w1[:half], preferred_element_type=f32)
    b = jnp.dot(c, w1[half:], preferred_element_type=f32)
    peb = jnp.dot(pe_ref[...], w1, preferred_element_type=f32)[0:1]
    hid = a + pltpu.roll(b, nc - 1, 0) + peb
    act = _gelu_tanh(hid)
    out = jnp.dot(act.astype(bf16), w2_ref[...].astype(bf16), preferred_element_type=f32)
    if do_norm:
        out = _rope(_rms(out, gain_ref[...]), rc_ref[...], rsa_ref[...], rsb_ref[...])
    o_ref[0] = out.astype(bf16)


def _compress(chunks, pe, w1, w2, gain, rc, rsa, rsb, do_norm):
    ng, nc, width = chunks.shape
    pe8 = jnp.broadcast_to(pe.reshape(1, CMP_LEN * HEAD_DIM), (8, CMP_LEN * HEAD_DIM)).astype(bf16)
    full = lambda shape: pl.BlockSpec(shape, lambda g: (0,) * len(shape))
    return pl.pallas_call(
        functools.partial(_compress_kernel, do_norm=do_norm),
        grid=(ng,),
        in_specs=[
            pl.BlockSpec((1, nc, width), lambda g: (g, 0, 0)),
            full((8, CMP_LEN * HEAD_DIM)),
            full((CMP_LEN * HEAD_DIM, CMP_HIDDEN)),
            full((CMP_HIDDEN, HEAD_DIM)),
            full((1, HEAD_DIM)),
            full((nc, HEAD_DIM)), full((nc, HEAD_DIM)), full((nc, HEAD_DIM)),
        ],
        out_specs=pl.BlockSpec((1, nc, HEAD_DIM), lambda g: (g, 0, 0)),
        out_shape=jax.ShapeDtypeStruct((ng, nc, HEAD_DIM), bf16),
        compiler_params=_cparams(("parallel",)),
        name="compress_k" if do_norm else "compress_v",
    )(chunks, pe8, w1, w2, gain, rc, rsa, rsb)


def _sigmoid(x):
    return 1.0 / (1.0 + jnp.exp(-x))


def _nsa_cmp_kernel(q_ref, kc_ref, vc_ref, g_ref, smt_ref, oc_ref, bias_ref, *, tq):
    q0 = pl.program_id(2) * tq
    kc = kc_ref[0]
    vc = vc_ref[0]
    nc = kc.shape[0]
    R = NSA_GROUP * tq
    sig = _sigmoid(g_ref[0])
    q = q_ref[...].reshape(R, HEAD_DIM)
    s = lax.dot_general(q, kc, (((1,), (1,)), ((), ())), preferred_element_type=f32)
    t = q0 + (lax.broadcasted_iota(i32, (R, nc), 0) & (tq - 1))
    cend = lax.broadcasted_iota(i32, (R, nc), 1) * CMP_STRIDE + (CMP_LEN - 1)
    mask = cend <= t
    s = jnp.where(mask, s, NEG_INF)
    m = jnp.max(s, axis=-1, keepdims=True)
    p = jnp.where(mask, jnp.exp(s - m), 0.0)
    p = p / jnp.maximum(jnp.sum(p, axis=-1, keepdims=True), 1e-30)
    o = jnp.dot(p.astype(bf16), vc, preferred_element_type=f32)
    for gi in range(NSA_GROUP):
        col = gi * N_BRANCH
        oc_ref[:, gi * HEAD_DIM:(gi + 1) * HEAD_DIM] = o[gi * tq:(gi + 1) * tq] * sig[:, col:col + 1]

    psum = p[0:tq] + p[tq:2 * tq] + p[2 * tq:3 * tq] + p[3 * tq:4 * tq]
    ps_hi = psum.astype(bf16)
    ps_lo = (psum - ps_hi.astype(f32)).astype(bf16)
    nt = (((1,), (1,)), ((), ()))
    imp = (lax.dot_general(smt_ref[...], ps_hi, nt, preferred_element_type=f32)
           + lax.dot_general(smt_ref[...], ps_lo, nt, preferred_element_type=f32))
    nselp = imp.shape[0]
    jf = lax.broadcasted_iota(i32, (nselp, tq), 0).astype(f32)
    cur = ((q0 + lax.broadcasted_iota(i32, (nselp, tq), 1)) >> SEL_SHIFT).astype(f32)
    forced = (jf == 0.0) | (jf == cur) | (jf == cur - 1.0)
    imp = jnp.where(forced, FORCED_SCORE, imp)
    imp = jnp.where(jf <= cur, imp, NEG_INF)
    PICKED = -3e38
    work = imp
    for _ in range(N_SELECTED):
        mx = jnp.max(work, axis=0, keepdims=True)
        first = jnp.min(jnp.where(work == mx, jf, float(nselp)), axis=0, keepdims=True)
        work = jnp.where(jf == first, PICKED, work)
    bias_t = jnp.where((work < -2e38) & (imp > -1.0), 0.0, MASK_VAL)
    bias_ref[0] = bias_t.T.astype(bf16)


def _nsa_cmp(P, kcmp, vcmp, gate, smt, B, S, tq):
    T = B * S
    nqb = S // tq
    nc = kcmp.shape[1]
    nselp = smt.shape[0]
    return pl.pallas_call(
        functools.partial(_nsa_cmp_kernel, tq=tq),
        grid=(NSA_KV_HEADS, B, nqb),
        in_specs=[
            pl.BlockSpec((NSA_GROUP, tq, HEAD_DIM), lambda h, b, i: (h, b * nqb + i, 0)),
            pl.BlockSpec((1, nc, HEAD_DIM), lambda h, b, i: (h * B + b, 0, 0)),
            pl.BlockSpec((1, nc, HEAD_DIM), lambda h, b, i: (h * B + b, 0, 0)),
            pl.BlockSpec((1, tq, HEAD_DIM), lambda h, b, i: (h, b * nqb + i, 0)),
            pl.BlockSpec((nselp, nc), lambda h, b, i: (0, 0)),
        ],
        out_specs=[
            pl.BlockSpec((tq, NSA_GROUP * HEAD_DIM), lambda h, b, i: (b * nqb + i, h)),
            pl.BlockSpec((1, tq, nselp), lambda h, b, i: (h, b * nqb + i, 0)),
        ],
        out_shape=[
            jax.ShapeDtypeStruct((T, NSA_HEADS * HEAD_DIM), f32),
            jax.ShapeDtypeStruct((NSA_KV_HEADS, T, nselp), bf16),
        ],
        compiler_params=_cparams(("parallel", "parallel", "parallel")),
        name="nsa_cmp",
    )(P, kcmp, vcmp, gate, smt)


def _lane_tile(x, n):
    return x if n == 1 else jnp.concatenate([x] * n, axis=1)


ROW_CHUNK = 256


def _softmax_step(rows, s, v, m_ref, acc_ref, l_ref=None):
    tk = s.shape[1]
    m_prev = m_ref[rows, :]
    m_new = jnp.maximum(m_prev, jnp.max(s, axis=1, keepdims=True))
    alpha = jnp.exp(m_prev - m_new)
    p = jnp.exp(s - _lane_tile(m_new, tk // LANES))
    if l_ref is not None:
        l_ref[rows, :] = alpha * l_ref[rows, :] + jnp.sum(p, axis=1, keepdims=True)
    acc_ref[rows, :] = _lane_tile(alpha, 2) * acc_ref[rows, :] + jnp.dot(
        p.astype(bf16), v, preferred_element_type=f32)
    m_ref[rows, :] = m_new


def _nsa_sel_kernel(q_ref, bias_ref, ks_ref, vs_ref, kw_ref, vw_ref, g_ref, oc_ref,
                    o_ref, qaug_ref, m_ref, acc_ref, *, tq, tk, tkw):
    q0 = pl.program_id(2) * tq
    R = NSA_GROUP * tq
    nselp = bias_ref.shape[2]
    q = q_ref[...].reshape(R, HEAD_DIM)
    qaug_ref[:, 0:HEAD_DIM] = q
    bias = bias_ref[0]
    for gi in range(NSA_GROUP):
        qaug_ref[gi * tq:(gi + 1) * tq, HEAD_DIM:HEAD_DIM + nselp] = bias
    sig = _sigmoid(g_ref[0])
    rc = min(ROW_CHUNK, tq)

    def row_t(c, width):
        return q0 + ((c * rc + lax.broadcasted_iota(i32, (rc, width), 0)) & (tq - 1))

    def init():
        m_ref[...] = jnp.full(m_ref.shape, M_INIT, f32)
        acc_ref[...] = jnp.zeros(acc_ref.shape, f32)

    def finish():
        return acc_ref[:, 0:HEAD_DIM] / jnp.maximum(acc_ref[:, HEAD_DIM:2 * HEAD_DIM], 1e-30)

    init()
    ones_k = jnp.ones((tk, HEAD_DIM), bf16)

    def sel_step(kt, diagonal):
        k0 = pl.multiple_of(kt * tk, tk)
        k = ks_ref[0, pl.ds(k0, tk), :]
        blk = (k0 + lax.broadcasted_iota(i32, (tk, nselp), 0)) >> SEL_SHIFT
        onehot = jnp.where(blk == lax.broadcasted_iota(i32, (tk, nselp), 1), 1.0, 0.0).astype(bf16)
        kaug = jnp.concatenate([k, onehot], axis=1)
        vaug = jnp.concatenate([vs_ref[0, pl.ds(k0, tk), :], ones_k], axis=1)
        for c in range(R // rc):
            rows = slice(c * rc, (c + 1) * rc)
            s = lax.dot_general(qaug_ref[rows, :], kaug, (((1,), (1,)), ((), ())),
                                preferred_element_type=f32)
            if diagonal:
                s = jnp.where(k0 + lax.broadcasted_iota(i32, (rc, tk), 1) <= row_t(c, tk), s, MASK_VAL)
            _softmax_step(rows, s, vaug, m_ref, acc_ref)

    def sel_body(kt, carry):
        sel_step(kt, False)
        return carry

    n_full = q0 // tk
    lax.fori_loop(0, n_full, sel_body, 0)
    sel_step(n_full, True)
    o_s = finish()

    init()
    ones_w = jnp.ones((tkw, HEAD_DIM), bf16)
    qi = pl.program_id(2)

    def win_step(kt, mode):
        k0 = pl.multiple_of(kt * tkw, tkw)
        k = kw_ref[0, pl.ds(k0, tkw), :]
        vaug = jnp.concatenate([vw_ref[0, pl.ds(k0, tkw), :], ones_w], axis=1)
        for c in range(R // rc):
            rows = slice(c * rc, (c + 1) * rc)
            s = lax.dot_general(qaug_ref[rows, 0:HEAD_DIM], k, (((1,), (1,)), ((), ())),
                                preferred_element_type=f32)
            if mode != "full":
                t = row_t(c, tkw)
                kp = k0 + lax.broadcasted_iota(i32, (rc, tkw), 1)
                s = jnp.where(kp <= t if mode == "diagonal" else t - kp < WINDOW, s, MASK_VAL)
            _softmax_step(rows, s, vaug, m_ref, acc_ref)

    @pl.when(qi >= 2)
    def _():
        win_step(qi - 2, "edge")

    @pl.when(qi >= 1)
    def _():
        win_step(qi - 1, "full")

    win_step(qi, "diagonal")
    o_w = finish()

    for gi in range(NSA_GROUP):
        rows = slice(gi * tq, (gi + 1) * tq)
        cols = slice(gi * HEAD_DIM, (gi + 1) * HEAD_DIM)
        c = gi * N_BRANCH
        o_ref[:, cols] = (oc_ref[:, cols] + o_s[rows] * sig[:, c + 1:c + 2]
                          + o_w[rows] * sig[:, c + 2:c + 3]).astype(bf16)


def _nsa_sel(Pn, Pr, bias, gate, o_c, B, S, tq, tk, tkw):
    assert tk % tq == 0 and tkw == tq and WINDOW == 2 * tq
    T = B * S
    nqb = S // tq
    nselp = bias.shape[2]
    R = NSA_GROUP * tq
    kv_spec = lambda base: pl.BlockSpec((1, S, HEAD_DIM), lambda h, b, i: (base + h, b, 0))
    return pl.pallas_call(
        functools.partial(_nsa_sel_kernel, tq=tq, tk=tk, tkw=tkw),
        grid=(NSA_KV_HEADS, B, nqb),
        in_specs=[
            pl.BlockSpec((NSA_GROUP, tq, HEAD_DIM), lambda h, b, i: (h, b * nqb + i, 0)),
            pl.BlockSpec((1, tq, nselp), lambda h, b, i: (h, b * nqb + i, 0)),
            kv_spec(HD_KS), kv_spec(HR_VS), kv_spec(HD_KW), kv_spec(HR_VW),
            pl.BlockSpec((1, tq, HEAD_DIM), lambda h, b, i: (h, b * nqb + i, 0)),
            pl.BlockSpec((tq, NSA_GROUP * HEAD_DIM), lambda h, b, i: (b * nqb + i, h)),
        ],
        out_specs=pl.BlockSpec((tq, NSA_GROUP * HEAD_DIM), lambda h, b, i: (b * nqb + i, h)),
        out_shape=jax.ShapeDtypeStruct((T, NSA_HEADS * HEAD_DIM), bf16),
        scratch_shapes=[
            pltpu.VMEM((R, HEAD_DIM + nselp), bf16),
            pltpu.VMEM((R, LANES), f32),
            pltpu.VMEM((R, 2 * HEAD_DIM), f32),
        ],
        compiler_params=_cparams(("parallel", "parallel", "arbitrary")),
        name="nsa_sel_win",
    )(Pn, bias, Pn, Pr, Pn, Pr, gate, o_c)


def _diff_kernel(q_ref, k_ref, v_ref, lam_ref, sg_ref, o_ref, m_ref, l_ref, acc_ref, sa_ref, sb_ref,
                 *, tq, tk, lambda_init):
    q0 = pl.program_id(2) * tq
    R = 2 * tq
    rc = min(ROW_CHUNK, tq)
    m_ref[...] = jnp.full(m_ref.shape, M_INIT, f32)
    l_ref[...] = jnp.zeros(l_ref.shape, f32)
    acc_ref[...] = jnp.zeros(acc_ref.shape, f32)

    def scores(kt, s_ref):
        k0 = pl.multiple_of(kt * tk, tk)
        for comp in range(2):
            s_ref[comp * tq:(comp + 1) * tq, :] = lax.dot_general(
                q_ref[comp], k_ref[comp, pl.ds(k0, tk), :], (((1,), (1,)), ((), ())),
                preferred_element_type=f32)

    def softmax_pv(kt, s_ref, diagonal):
        k0 = pl.multiple_of(kt * tk, tk)
        v = jnp.concatenate([v_ref[0, pl.ds(k0, tk), :], v_ref[1, pl.ds(k0, tk), :]], axis=1)
        for c in range(R // rc):
            rows = slice(c * rc, (c + 1) * rc)
            s = s_ref[rows, :]
            if diagonal:
                t = q0 + ((c * rc + lax.broadcasted_iota(i32, (rc, tk), 0)) & (tq - 1))
                s = jnp.where(k0 + lax.broadcasted_iota(i32, (rc, tk), 1) <= t, s, MASK_VAL)
            _softmax_step(rows, s, v, m_ref, acc_ref, l_ref)

    def pair(j, carry):
        scores(2 * j + 1, sb_ref)
        softmax_pv(2 * j, sa_ref, False)
        scores(2 * j + 2, sa_ref)
        softmax_pv(2 * j + 1, sb_ref, False)
        return carry

    qi = pl.program_id(2)
    scores(0, sa_ref)
    lax.fori_loop(0, qi // 2, pair, 0)

    @pl.when(qi % 2 == 0)
    def _():
        softmax_pv(qi, sa_ref, True)

    @pl.when(qi % 2 == 1)
    def _():
        scores(qi, sb_ref)
        softmax_pv(qi - 1, sa_ref, False)
        softmax_pv(qi, sb_ref, True)

    lq = lam_ref[...]
    lam = (jnp.exp(jnp.sum(lq[0:1] * lq[1:2], axis=1, keepdims=True))
           - jnp.exp(jnp.sum(lq[2:3] * lq[3:4], axis=1, keepdims=True)) + lambda_init)
    o = acc_ref[...] / _lane_tile(jnp.maximum(l_ref[...], 1e-30), 2)
    a = o[0:tq] - lam * o[tq:R]
    o_ref[...] = (_rms(a, sg_ref[...]) * (1.0 - lambda_init)).astype(bf16)


def _diff_attn(Pn, Pr, lam_vecs, subln_g, B, S, tq, tk, lambda_init):
    assert tq == tk
    T = B * S
    nqb = S // tq
    return pl.pallas_call(
        functools.partial(_diff_kernel, tq=tq, tk=tk, lambda_init=lambda_init),
        grid=(DIFF_HEADS, B, nqb),
        in_specs=[
            pl.BlockSpec((2, tq, HEAD_DIM), lambda h, b, i: (HD_QD // 2 + h, b * nqb + i, 0)),
            pl.BlockSpec((2, S, HEAD_DIM), lambda h, b, i: (HD_KD // 2 + h, b, 0)),
            pl.BlockSpec((2, S, HEAD_DIM), lambda h, b, i: (HR_VD // 2 + h, b, 0)),
            pl.BlockSpec((8, HEAD_DIM), lambda h, b, i: (0, 0)),
            pl.BlockSpec((1, DIFF_V_DIM), lambda h, b, i: (0, 0)),
        ],
        out_specs=pl.BlockSpec((tq, DIFF_V_DIM), lambda h, b, i: (b * nqb + i, h)),
        out_shape=jax.ShapeDtypeStruct((T, DIFF_HEADS * DIFF_V_DIM), bf16),
        scratch_shapes=[
            pltpu.VMEM((2 * tq, LANES), f32),
            pltpu.VMEM((2 * tq, LANES), f32),
            pltpu.VMEM((2 * tq, DIFF_V_DIM), f32),
            pltpu.VMEM((2 * tq, tk), f32),
            pltpu.VMEM((2 * tq, tk), f32),
        ],
        compiler_params=_cparams(("parallel", "parallel", "arbitrary")),
        name="diff_attn",
    )(Pn, Pn, Pr, lam_vecs, subln_g)


def _outproj_kernel(x_ref, on_ref, od_ref, w_ref, g_ref, wr_ref, br_ref,
                    h_ref, hn_ref, lg_ref):
    half = NSA_HEADS * HEAD_DIM
    tm = x_ref.shape[0]
    rc = min(ROW_CHUNK, tm)
    for c in range(tm // rc):
        rows = slice(c * rc, (c + 1) * rc)
        h = x_ref[rows, :] + jnp.dot(on_ref[rows, :], w_ref[0:half], preferred_element_type=f32) \
            + jnp.dot(od_ref[rows, :], w_ref[half:], preferred_element_type=f32)
        h_ref[rows, :] = h
        hn = _rms(h, g_ref[...])
        hn_ref[rows, :] = hn
        hn_hi = hn.astype(bf16)
        hn_lo = (hn - hn_hi.astype(f32)).astype(bf16)
        r = (jnp.dot(hn_hi, wr_ref[...], preferred_element_type=f32)
             + jnp.dot(hn_lo, wr_ref[...], preferred_element_type=f32))
        lg_ref[rows, :] = r[:, 0:LANES] + r[:, LANES:2 * LANES] + br_ref[...]


def _outproj(x2, o_nsa, o_diff, w_out_b, ffn_g, w_router_p, b_router_p, tm):
    T = x2.shape[0]
    full = lambda shape: pl.BlockSpec(shape, lambda i: (0,) * len(shape))
    return pl.pallas_call(
        _outproj_kernel,
        grid=(T // tm,),
        in_specs=[
            pl.BlockSpec((tm, D_MODEL), lambda i: (i, 0)),
            pl.BlockSpec((tm, NSA_HEADS * HEAD_DIM), lambda i: (i, 0)),
            pl.BlockSpec((tm, DIFF_HEADS * DIFF_V_DIM), lambda i: (i, 0)),
            full((D_MODEL, D_MODEL)),
            full((1, D_MODEL)),
            full((D_MODEL, 2 * LANES)),
            full((1, LANES)),
        ],
        out_specs=[
            pl.BlockSpec((tm, D_MODEL), lambda i: (i, 0)),
            pl.BlockSpec((tm, D_MODEL), lambda i: (i, 0)),
            pl.BlockSpec((tm, LANES), lambda i: (i, 0)),
        ],
        out_shape=[
            jax.ShapeDtypeStruct((T, D_MODEL), f32),
            jax.ShapeDtypeStruct((T, D_MODEL), f32),
            jax.ShapeDtypeStruct((T, LANES), f32),
        ],
        compiler_params=_cparams(("parallel",)),
        name="outproj_router",
    )(x2, o_nsa, o_diff, w_out_b, ffn_g, w_router_p, b_router_p)


def _start_row_gathers(idx_fn, n, src_ref, dst_ref, sem):
    for r in range(n):
        pltpu.make_async_copy(src_ref.at[pl.ds(idx_fn(r), 1)], dst_ref.at[pl.ds(r, 1)], sem).start()


def _wait_row_gathers(n, src_ref, dst_ref, sem):
    for r in range(n):
        pltpu.make_async_copy(src_ref.at[pl.ds(0, 1)], dst_ref.at[pl.ds(r, 1)], sem).wait()


def _expert_row_blocks(first, nblk, x_hbm, o_dst, xbuf, obuf, xsem, osem, tm, compute):
    def x_copy(blk, slot):
        return pltpu.make_async_copy(x_hbm.at[pl.ds(pl.multiple_of(blk * tm, tm), tm)],
                                     xbuf.at[slot], xsem.at[slot])

    def o_copy(blk, slot):
        return pltpu.make_async_copy(obuf.at[slot], o_dst(pl.multiple_of(blk * tm, tm)), osem.at[slot])

    @pl.when(nblk > 0)
    def _():
        x_copy(first, 0).start()

    def body(i, carry):
        slot = i % 2

        @pl.when(i + 1 < nblk)
        def _():
            x_copy(first + i + 1, 1 - slot).start()

        x_copy(first + i, slot).wait()

        @pl.when(i >= 2)
        def _():
            o_copy(first + i - 2, slot).wait()

        obuf[slot] = compute(xbuf[slot])
        o_copy(first + i, slot).start()
        return carry

    lax.fori_loop(0, nblk, body, 0)

    @pl.when(nblk >= 2)
    def _():
        o_copy(first + nblk - 2, nblk % 2).wait()

    @pl.when(nblk >= 1)
    def _():
        o_copy(first + nblk - 1, (nblk - 1) % 2).wait()

    return o_copy


def _zero_unused_blocks(o_copy, obuf, lo, hi):
    obuf[0] = jnp.zeros(obuf.shape[1:], obuf.dtype)

    def body(blk, carry):
        cp = o_copy(blk, 0)
        cp.start()
        cp.wait()
        return carry

    lax.fori_loop(lo, hi, body, 0)


def _gateup_kernel(fb_ref, nbk_ref, nu_ref, st_ref, sb_ref, hn_hbm, wg_ref, wu_ref, bg_ref, bu_ref, hb_hbm,
                   wgb_ref, wub_ref, xa_ref, xb_ref, obuf, xsem, osem, *, tm, tn, n_blocks):
    j = pl.program_id(0)
    e = pl.program_id(1)
    wgb_ref[...] = wg_ref[0].astype(bf16)
    wub_ref[...] = wu_ref[0].astype(bf16)
    col0 = pl.multiple_of(j * tn, tn)
    first = fb_ref[e]
    nblk = nbk_ref[e]
    xbufs = (xa_ref, xb_ref)

    def compute(x):
        gl = jnp.dot(x, wgb_ref[...], preferred_element_type=f32) + bg_ref[0]
        up = jnp.dot(x, wub_ref[...], preferred_element_type=f32) + bu_ref[0]
        gl = jnp.minimum(gl, SWIGLU_LIMIT)
        up = jnp.clip(up, -SWIGLU_LIMIT, SWIGLU_LIMIT)
        return ((up + 1.0) * (gl * _sigmoid(SWIGLU_ALPHA * gl))).astype(bf16)

    def issue(g, p):
        base = sb_ref[g]
        _start_row_gathers(lambda r: st_ref[base + r], tm, hn_hbm, xbufs[p], xsem.at[p])

    def wait_x(p):
        _wait_row_gathers(tm, hn_hbm, xbufs[p], xsem.at[p])

    def o_copy(g, slot):
        return pltpu.make_async_copy(
            obuf.at[slot], hb_hbm.at[pl.ds(pl.multiple_of(g * tm, tm), tm), pl.ds(col0, tn)], osem.at[slot])

    @pl.when(e == 0)
    def _():
        issue(0, 0)

    def body(i, carry):
        g = first + i
        oslot = i % 2

        @pl.when(i >= 2)
        def _():
            o_copy(g - 2, oslot).wait()

        for p in range(2):
            @pl.when(g % 2 == p)
            def _():
                wait_x(p)
                issue(g + 1, 1 - p)
                obuf[oslot] = compute(xbufs[p][...].astype(bf16))

        o_copy(g, oslot).start()
        return carry

    lax.fori_loop(0, nblk, body, 0)

    @pl.when(nblk >= 2)
    def _():
        o_copy(first + nblk - 2, nblk % 2).wait()

    @pl.when(nblk >= 1)
    def _():
        o_copy(first + nblk - 1, (nblk - 1) % 2).wait()

    @pl.when(e == N_EXPERTS - 1)
    def _():
        nu = nu_ref[0]
        for p in range(2):
            @pl.when(nu % 2 == p)
            def _():
                wait_x(p)
        _zero_unused_blocks(o_copy, obuf, nu, n_blocks)


def _gateup(first_blk, n_blk, n_used, sorted_tok, src_base, hn, w_gate_up, b_gate_up3, n_blocks, tm, tn):
    nj = D_FF // tn
    return pl.pallas_call(
        functools.partial(_gateup_kernel, tm=tm, tn=tn, n_blocks=n_blocks),
        grid_spec=pltpu.PrefetchScalarGridSpec(
            num_scalar_prefetch=5,
            grid=(nj, N_EXPERTS),
            in_specs=[
                pl.BlockSpec(memory_space=pl.ANY),
                pl.BlockSpec((1, D_MODEL, tn), lambda j, e, *_: (e, 0, j)),
                pl.BlockSpec((1, D_MODEL, tn), lambda j, e, *_: (e, 0, nj + j)),
                pl.BlockSpec((1, 1, tn), lambda j, e, *_: (e, 0, j)),
                pl.BlockSpec((1, 1, tn), lambda j, e, *_: (e, 0, nj + j)),
            ],
            out_specs=pl.BlockSpec(memory_space=pl.ANY),
            scratch_shapes=[
                pltpu.VMEM((D_MODEL, tn), bf16), pltpu.VMEM((D_MODEL, tn), bf16),
                pltpu.VMEM((tm, D_MODEL), hn.dtype), pltpu.VMEM((tm, D_MODEL), hn.dtype),
                pltpu.VMEM((2, tm, tn), bf16),
                pltpu.SemaphoreType.DMA((2,)), pltpu.SemaphoreType.DMA((2,)),
            ],
        ),
        out_shape=jax.ShapeDtypeStruct((n_blocks * tm, D_FF), bf16),
        compiler_params=_cparams(("arbitrary", "arbitrary")),
        name="moe_gate_up",
    )(first_blk, n_blk, n_used, sorted_tok, src_base, hn, w_gate_up, w_gate_up, b_gate_up3, b_gate_up3)


def _down_kernel(fb_ref, nbk_ref, nu_ref, hb_hbm, w_ref, b_ref, y_hbm,
                 wb_ref, xbuf, obuf, xsem, osem, *, tm, n_blocks):
    e = pl.program_id(0)
    wb_ref[...] = w_ref[0].astype(bf16)

    def compute(x):
        return jnp.dot(x, wb_ref[...], preferred_element_type=f32) + b_ref[0]

    o_copy = _expert_row_blocks(fb_ref[e], nbk_ref[e], hb_hbm, lambda r0: y_hbm.at[pl.ds(r0, tm)],
                                xbuf, obuf, xsem, osem, tm, compute)

    @pl.when(e == N_EXPERTS - 1)
    def _():
        _zero_unused_blocks(o_copy, obuf, nu_ref[0], n_blocks)


def _down(first_blk, n_blk, n_used, hb, w_down, b_down3, tm):
    n_rows = hb.shape[0]
    return pl.pallas_call(
        functools.partial(_down_kernel, tm=tm, n_blocks=n_rows // tm),
        grid_spec=pltpu.PrefetchScalarGridSpec(
            num_scalar_prefetch=3,
            grid=(N_EXPERTS,),
            in_specs=[
                pl.BlockSpec(memory_space=pl.ANY),
                pl.BlockSpec((1, D_FF, D_MODEL), lambda e, *_: (e, 0, 0)),
                pl.BlockSpec((1, 1, D_MODEL), lambda e, *_: (e, 0, 0)),
            ],
            out_specs=pl.BlockSpec(memory_space=pl.ANY),
            scratch_shapes=[
                pltpu.VMEM((D_FF, D_MODEL), bf16),
                pltpu.VMEM((2, tm, D_FF), bf16), pltpu.VMEM((2, tm, D_MODEL), f32),
                pltpu.SemaphoreType.DMA((2,)), pltpu.SemaphoreType.DMA((2,)),
            ],
        ),
        out_shape=jax.ShapeDtypeStruct((n_rows, D_MODEL), f32),
        compiler_params=_cparams(("arbitrary",)),
        name="moe_down",
    )(first_blk, n_blk, n_used, hb, w_down, b_down3)


def _combine_kernel(pos_ref, h_ref, gt_ref, y_ref, o_ref, buf_ref, sem, *, tc):
    i = pl.program_id(0)
    n = pl.num_programs(0)
    rows = TOP_K * tc

    def issue(step, slot):
        base = step * rows
        _start_row_gathers(lambda r: pos_ref[base + r], rows, y_ref, buf_ref.at[slot], sem.at[slot])

    @pl.when(i == 0)
    def _():
        issue(0, 0)

    @pl.when(i + 1 < n)
    def _():
        issue(i + 1, (i + 1) % 2)

    slot = i % 2
    _wait_row_gathers(rows, y_ref, buf_ref.at[slot], sem.at[slot])
    gt = gt_ref[...]
    out = h_ref[...]
    for k in range(TOP_K):
        out = out + gt[:, k:k + 1] * buf_ref[slot, k * tc:(k + 1) * tc, :]
    o_ref[...] = out


def _combine(pos_blocks, h, gate_pad, y, tc):
    T = h.shape[0]
    return pl.pallas_call(
        functools.partial(_combine_kernel, tc=tc),
        grid_spec=pltpu.PrefetchScalarGridSpec(
            num_scalar_prefetch=1,
            grid=(T // tc,),
            in_specs=[
                pl.BlockSpec((tc, D_MODEL), lambda i, pos: (i, 0)),
                pl.BlockSpec((tc, LANES), lambda i, pos: (i, 0)),
                pl.BlockSpec(memory_space=pl.ANY),
            ],
            out_specs=pl.BlockSpec((tc, D_MODEL), lambda i, pos: (i, 0)),
            scratch_shapes=[pltpu.VMEM((2, TOP_K * tc, D_MODEL), f32), pltpu.SemaphoreType.DMA((2,))],
        ),
        out_shape=jax.ShapeDtypeStruct((T, D_MODEL), f32),
        compiler_params=_cparams(("arbitrary",)),
        name="moe_combine",
    )(pos_blocks, h, gate_pad, y)


def _rearranged_w_in(w_in):
    splits = np.cumsum([0, 1024, 256, 256, 256, 256, 256, 256, 24, 1024, 1024, 1024])
    q, kc, vc, ks, vs, kw, vw, g, qd, kd, vd = [w_in[:, splits[n]:splits[n + 1]] for n in range(11)]
    gcols = NSA_GROUP * N_BRANCH
    zpad = lambda n: jnp.zeros((D_MODEL, n), w_in.dtype)
    w_norm = jnp.concatenate([q, ks, kw, qd, kd], axis=1)
    w_raw = jnp.concatenate([kc, vc, vs, vw, vd,
                             g[:, :gcols], zpad(HEAD_DIM - gcols), g[:, gcols:], zpad(HEAD_DIM - gcols),
                             zpad((N_RAW_HEADS - HR_GATE - 2) * HEAD_DIM)], axis=1)
    return w_norm.astype(bf16), w_raw.astype(bf16)


def _sel_map_t(S, nselp):
    n_chunks = S // CMP_STRIDE
    n_cmp = (S - CMP_LEN) // CMP_STRIDE + 1
    n_sel = S // SEL_BLOCK
    c_start = np.arange(n_chunks) * CMP_STRIDE
    s_start = np.arange(nselp) * SEL_BLOCK
    ov = (c_start[None, :] < s_start[:, None] + SEL_BLOCK) & (c_start[None, :] + CMP_LEN > s_start[:, None])
    ov &= (np.arange(n_chunks)[None, :] < n_cmp) & (np.arange(nselp)[:, None] < n_sel)
    return jnp.asarray(ov, bf16)


def _attention_block(x2, B, S, attn_norm_g, w_in, nsa_q_norm_g, nsa_k_norm_g, pos_k, pos_v,
                     k_w1, k_w2, v_w1, v_w2, diff_q_norm_g, diff_k_norm_g, lq1, lk1, lq2, lk2,
                     subln_g, lambda_init):
    T = B * S
    scale = HEAD_DIM ** -0.5
    tiles = _tiles(S)
    tm = tiles["proj_rows"]
    rope_c, rope_sa, rope_sb = _rope_tables(np.arange(S))
    gains = jnp.concatenate([
        jnp.broadcast_to(nsa_q_norm_g * scale, (NSA_HEADS, HEAD_DIM)),
        jnp.broadcast_to(nsa_k_norm_g[1], (NSA_KV_HEADS, HEAD_DIM)),
        jnp.broadcast_to(nsa_k_norm_g[2], (NSA_KV_HEADS, HEAD_DIM)),
        jnp.broadcast_to(diff_q_norm_g * scale, (2 * DIFF_HEADS, HEAD_DIM)),
        jnp.broadcast_to(diff_k_norm_g, (2 * DIFF_HEADS, HEAD_DIM)),
    ], axis=0).reshape(N_NORM_HEADS, 1, HEAD_DIM)
    w_norm, w_raw = _rearranged_w_in(w_in)
    Pn, Pr, gate = _inproj(x2, attn_norm_g.reshape(1, D_MODEL), w_norm, w_raw, gains,
                           rope_c, rope_sa, rope_sb, S, tm)

    n_chunks = S // CMP_STRIDE
    cmp_pos = np.arange(n_chunks) * CMP_STRIDE + (CMP_LEN - 1)
    cc, csa, csb = _rope_tables(cmp_pos)
    chunk_view = lambda hd: Pr[hd:hd + NSA_KV_HEADS].reshape(NSA_KV_HEADS * B, n_chunks, CMP_STRIDE * HEAD_DIM)
    kcmp = _compress(chunk_view(HR_KC), pos_k, k_w1, k_w2, nsa_k_norm_g[0].reshape(1, HEAD_DIM),
                     cc, csa, csb, True)
    vcmp = _compress(chunk_view(HR_VC), pos_v, v_w1, v_w2, nsa_k_norm_g[0].reshape(1, HEAD_DIM),
                     cc, csa, csb, False)

    nselp = -(-(S // SEL_BLOCK) // LANES) * LANES
    tq = tiles["attn_q"]
    o_c, bias = _nsa_cmp(Pn, kcmp, vcmp, gate, _sel_map_t(S, nselp), B, S, tq)
    o_nsa = _nsa_sel(Pn, Pr, bias, gate, o_c, B, S, tq, tiles["sel_keys"], tq)

    lam_vecs = jnp.concatenate([lq1[None], lk1[None], lq2[None], lk2[None],
                                jnp.zeros((4, HEAD_DIM), f32)], axis=0)
    o_diff = _diff_attn(Pn, Pr, lam_vecs, subln_g.reshape(1, DIFF_V_DIM), B, S,
                        tiles["diff_q"], tiles["diff_q"], lambda_init)
    return o_nsa, o_diff


def _moe_block(h, hn, logits, w_gate_up, b_gate_up, w_down, b_down, tiles):
    T = h.shape[0]
    tm = tiles["moe_rows"]
    TK = T * TOP_K
    top_v, top_e = lax.top_k(logits[:, :N_EXPERTS], TOP_K)
    gate = jax.nn.softmax(top_v, axis=-1)
    flat_e = top_e.reshape(TK).astype(i32)
    se, order = lax.sort((flat_e, jnp.arange(TK, dtype=i32)), num_keys=1, is_stable=True)
    st = order // TOP_K
    experts = jnp.arange(N_EXPERTS, dtype=i32)
    counts = jnp.sum((flat_e[:, None] == experts[None, :]).astype(i32), axis=0)
    padded = (counts + tm - 1) // tm * tm
    pad_end = jnp.cumsum(padded)
    pad_start = pad_end - padded
    cnt_start = jnp.cumsum(counts) - counts
    dest = pad_start[se] + (jnp.arange(TK, dtype=i32) - cnt_start[se])
    _, pos = lax.sort((order, dest), num_keys=1)
    n_blocks = -(-TK // tm) + N_EXPERTS
    blk_start = jnp.arange(n_blocks, dtype=i32) * tm
    blk_e = jnp.minimum(jnp.sum((pad_end[None, :] <= blk_start[:, None]).astype(i32), axis=1), N_EXPERTS - 1)
    src_base = cnt_start[blk_e] + (blk_start - pad_start[blk_e])
    n_used = (pad_end[-1] // tm).astype(i32).reshape(1)

    src_base = jnp.minimum(jnp.concatenate([src_base, src_base[-1:]]), TK).astype(i32)
    st = jnp.concatenate([st, jnp.zeros((tm,), i32)])
    first_blk = (pad_start // tm).astype(i32)
    n_blk = (padded // tm).astype(i32)
    hb = _gateup(first_blk, n_blk, n_used, st, src_base, hn, w_gate_up,
                 b_gate_up.reshape(N_EXPERTS, 1, 2 * D_FF), n_blocks, tm, tiles["moe_cols"])
    y = _down(first_blk, n_blk, n_used, hb, w_down, b_down.reshape(N_EXPERTS, 1, D_MODEL), tm)

    tc = tiles["combine_rows"]
    pos_blocks = pos.reshape(T // tc, tc, TOP_K).transpose(0, 2, 1).reshape(TK)
    gate_pad = jnp.pad(gate, ((0, 0), (0, LANES - TOP_K)))
    return _combine(pos_blocks, h, gate_pad, y, tc)


def kernel(x, attn_norm_g, w_in, nsa_q_norm_g, nsa_k_norm_g, nsa_cmp_pos_k, nsa_cmp_pos_v, nsa_cmp_k_w1, nsa_cmp_k_w2, nsa_cmp_v_w1, nsa_cmp_v_w2, diff_q_norm_g, diff_k_norm_g, diff_lambda_q1, diff_lambda_k1, diff_lambda_q2, diff_lambda_k2, diff_subln_g, w_out, ffn_norm_g, w_router, b_router, w_gate_up, b_gate_up, w_down, b_down):
    B, S, _ = x.shape
    T = B * S
    depth = attn_norm_g.shape[0]
    h = x.reshape(T, D_MODEL)
    for l in range(depth):
        lambda_init = 0.8 - 0.6 * math.exp(-0.3 * l)
        o_nsa, o_diff = _attention_block(
            h, B, S, attn_norm_g[l], w_in[l], nsa_q_norm_g[l], nsa_k_norm_g[l], nsa_cmp_pos_k[l],
            nsa_cmp_pos_v[l], nsa_cmp_k_w1[l], nsa_cmp_k_w2[l], nsa_cmp_v_w1[l], nsa_cmp_v_w2[l],
            diff_q_norm_g[l], diff_k_norm_g[l], diff_lambda_q1[l], diff_lambda_k1[l],
            diff_lambda_q2[l], diff_lambda_k2[l], diff_subln_g[l], lambda_init)
        w_router_f = jnp.pad(w_router[l], ((0, 0), (0, LANES - N_EXPERTS)))
        w_router_hi = w_router_f.astype(bf16)
        w_router_lo = (w_router_f - w_router_hi.astype(f32)).astype(bf16)
        w_router_p = jnp.concatenate([w_router_hi, w_router_lo], axis=1)
        b_router_p = jnp.pad(b_router[l], (0, LANES - N_EXPERTS)).reshape(1, LANES)
        h_mid, hn, logits = _outproj(h, o_nsa, o_diff, w_out[l].astype(bf16),
                                     ffn_norm_g[l].reshape(1, D_MODEL), w_router_p, b_router_p,
                                     _tiles(S)["out_rows"])
        h = _moe_block(h_mid, hn, logits, w_gate_up[l], b_gate_up[l], w_down[l], b_down[l], _tiles(S))
    return h.reshape(B, S, D_MODEL)
```

```python
import functools
import math

import numpy as np
import jax
import jax.numpy as jnp
from jax import lax
from jax.experimental import pallas as pl
from jax.experimental.pallas import tpu as pltpu

f32 = jnp.float32
bf16 = jnp.bfloat16
i32 = jnp.int32

D_MODEL = 2048
HEAD_DIM = 128
ROT_DIM = HEAD_DIM // 4
ROPE_THETA = 500000.0
NORM_EPS = 1e-6
NEG_INF = -1e30

NSA_HEADS = 8
NSA_KV_HEADS = 2
NSA_GROUP = NSA_HEADS // NSA_KV_HEADS
N_BRANCH = 3
CMP_LEN = 32
CMP_STRIDE = 16
CMP_HIDDEN = 2 * HEAD_DIM
SEL_BLOCK = 64
SEL_SHIFT = 6
N_SELECTED = 16
WINDOW = 512
FORCED_SCORE = 1e9

DIFF_HEADS = 4
DIFF_V_DIM = 2 * HEAD_DIM

N_EXPERTS = 32
TOP_K = 4
D_FF = D_MODEL
SWIGLU_ALPHA = 1.702
SWIGLU_LIMIT = 7.0

LANES = 128

HD_Q = 0
HD_KS = 8
HD_KW = 10
HD_QD = 12
HD_KD = 20
N_NORM_HEADS = 28
HR_KC = 0
HR_VC = 2
HR_VS = 4
HR_VW = 6
HR_VD = 8
HR_GATE = 16
N_RAW_HEADS = 20
HP = 4

MASK_VAL = -1e30
M_INIT = -5e29

VMEM_LIMIT = 56 * 1024 * 1024


def _tiles(S):
    return dict(
        proj_rows=min(1024, S),
        attn_q=min(256, S),
        sel_keys=min(1024, S),
        diff_q=min(512, S),
        out_rows=512,
        moe_rows=256,
        moe_cols=1024,
        combine_rows=64,
    )


def _cparams(sem):
    return pltpu.CompilerParams(dimension_semantics=sem, vmem_limit_bytes=VMEM_LIMIT)


def _rope_tables(pos):
    inv = np.power(ROPE_THETA, -np.arange(0, ROT_DIM, 2, dtype=np.float64) / ROT_DIM)
    ang = pos.astype(np.float64)[:, None] * inv[None, :]
    cos, sin = np.cos(ang), np.sin(ang)
    n = pos.shape[0]
    half = ROT_DIM // 2
    c = np.concatenate([cos, cos, np.ones((n, HEAD_DIM - ROT_DIM))], axis=1)
    sa = np.concatenate([-sin, np.zeros((n, HEAD_DIM - half))], axis=1)
    sb = np.concatenate([np.zeros((n, half)), sin, np.zeros((n, HEAD_DIM - ROT_DIM))], axis=1)
    return (jnp.asarray(c, f32), jnp.asarray(sa, f32), jnp.asarray(sb, f32))


def _rope(y, c, sa, sb):
    half = ROT_DIM // 2
    return (y * c + pltpu.roll(y, HEAD_DIM - half, 1) * sa + pltpu.roll(y, half, 1) * sb)


def _rms(y, gain):
    ms = jnp.mean(y * y, axis=-1, keepdims=True)
    return y * lax.rsqrt(ms + NORM_EPS) * gain


def _inproj_norm_kernel(x_ref, g_ref, w_ref, gain_ref, c_ref, sa_ref, sb_ref, o_ref, xn_ref):
    @pl.when(pl.program_id(1) == 0)
    def _():
        xn_ref[...] = _rms(x_ref[...], g_ref[...]).astype(bf16)

    tm = x_ref.shape[0]
    rc = min(ROW_CHUNK, tm)
    for r in range(tm // rc):
        rows = slice(r * rc, (r + 1) * rc)
        acc = jnp.dot(xn_ref[rows, :], w_ref[...], preferred_element_type=f32)
        c, sa, sb = c_ref[rows, :], sa_ref[rows, :], sb_ref[rows, :]
        for u in range(HP):
            y = _rms(acc[:, u * HEAD_DIM:(u + 1) * HEAD_DIM], gain_ref[u])
            o_ref[u, rows, :] = _rope(y, c, sa, sb).astype(bf16)


def _inproj_raw_kernel(x_ref, g_ref, w_ref, o_ref, gate_ref, xn_ref):
    j = pl.program_id(1)

    @pl.when(j == 0)
    def _():
        xn_ref[...] = _rms(x_ref[...], g_ref[...]).astype(bf16)

    tm = x_ref.shape[0]
    rc = min(ROW_CHUNK, tm)
    for r in range(tm // rc):
        rows = slice(r * rc, (r + 1) * rc)
        acc = jnp.dot(xn_ref[rows, :], w_ref[...], preferred_element_type=f32)
        for u in range(HP):
            o_ref[u, rows, :] = acc[:, u * HEAD_DIM:(u + 1) * HEAD_DIM].astype(bf16)

    @pl.when(j == HR_GATE // HP)
    def _():
        gate_ref[0] = jnp.dot(xn_ref[...], w_ref[:, 0:HEAD_DIM], preferred_element_type=f32)
        gate_ref[1] = jnp.dot(xn_ref[...], w_ref[:, HEAD_DIM:2 * HEAD_DIM], preferred_element_type=f32)


def _inproj(x2, attn_g, w_norm, w_raw, gains, rope_c, rope_sa, rope_sb, S, tm):
    T = x2.shape[0]
    nsb = S // tm
    x_specs = [
        pl.BlockSpec((tm, D_MODEL), lambda i, j: (i, 0)),
        pl.BlockSpec((1, D_MODEL), lambda i, j: (0, 0)),
        pl.BlockSpec((D_MODEL, HP * HEAD_DIM), lambda i, j: (0, j)),
    ]
    head_spec = pl.BlockSpec((HP, tm, HEAD_DIM), lambda i, j: (j, i, 0))
    pos_spec = pl.BlockSpec((tm, HEAD_DIM), lambda i, j: (i % nsb, 0))
    Pn = pl.pallas_call(
        _inproj_norm_kernel,
        grid=(T // tm, N_NORM_HEADS // HP),
        in_specs=x_specs + [pl.BlockSpec((HP, 1, HEAD_DIM), lambda i, j: (j, 0, 0)),
                            pos_spec, pos_spec, pos_spec],
        out_specs=head_spec,
        out_shape=jax.ShapeDtypeStruct((N_NORM_HEADS, T, HEAD_DIM), bf16),
        scratch_shapes=[pltpu.VMEM((tm, D_MODEL), bf16)],
        compiler_params=_cparams(("parallel", "arbitrary")),
        name="inproj_norm",
    )(x2, attn_g, w_norm, gains, rope_c, rope_sa, rope_sb)
    Pr, gate = pl.pallas_call(
        _inproj_raw_kernel,
        grid=(T // tm, N_RAW_HEADS // HP),
        in_specs=x_specs,
        out_specs=[head_spec, pl.BlockSpec((2, tm, HEAD_DIM), lambda i, j: (0, i, 0))],
        out_shape=[
            jax.ShapeDtypeStruct((N_RAW_HEADS, T, HEAD_DIM), bf16),
            jax.ShapeDtypeStruct((2, T, HEAD_DIM), f32),
        ],
        scratch_shapes=[pltpu.VMEM((tm, D_MODEL), bf16)],
        compiler_params=_cparams(("parallel", "arbitrary")),
        name="inproj_raw",
    )(x2, attn_g, w_raw)
    return Pn, Pr, gate


def _gelu_tanh(x):
    return 0.5 * x * (1.0 + jnp.tanh(math.sqrt(2.0 / math.pi) * (x + 0.044715 * (x * x * x))))


def _compress_kernel(c_ref, pe_ref, w1_ref, w2_ref, gain_ref, rc_ref, rsa_ref, rsb_ref,
                     o_ref, *, do_norm):
    half = CMP_STRIDE * HEAD_DIM
    c = c_ref[0]
    nc = c.shape[0]
    w1 = w1_ref[...].astype(bf16)
    a = jnp.dot(c, w1[:half], preferred_element_type=f32)
    b = jnp.dot(c, w1[half:], preferred_element_type=f32)
    peb = jnp.dot(pe_ref[...], w1, preferred_element_type=f32)[0:1]
    hid = a + pltpu.roll(b, nc - 1, 0) + peb
    act = _gelu_tanh(hid)
    out = jnp.dot(act.astype(bf16), w2_ref[...].astype(bf16), preferred_element_type=f32)
    if do_norm:
        out = _rope(_rms(out, gain_ref[...]), rc_ref[...], rsa_ref[...], rsb_ref[...])
    o_ref[0] = out.astype(bf16)


def _compress(chunks, pe, w1, w2, gain, rc, rsa, rsb, do_norm):
    ng, nc, width = chunks.shape
    pe8 = jnp.broadcast_to(pe.reshape(1, CMP_LEN * HEAD_DIM), (8, CMP_LEN * HEAD_DIM)).astype(bf16)
    full = lambda shape: pl.BlockSpec(shape, lambda g: (0,) * len(shape))
    return pl.pallas_call(
        functools.partial(_compress_kernel, do_norm=do_norm),
        grid=(ng,),
        in_specs=[
            pl.BlockSpec((1, nc, width), lambda g: (g, 0, 0)),
            full((8, CMP_LEN * HEAD_DIM)),
            full((CMP_LEN * HEAD_DIM, CMP_HIDDEN)),
            full((CMP_HIDDEN, HEAD_DIM)),
            full((1, HEAD_DIM)),
            full((nc, HEAD_DIM)), full((nc, HEAD_DIM)), full((nc, HEAD_DIM)),
        ],
        out_specs=pl.BlockSpec((1, nc, HEAD_DIM), lambda g: (g, 0, 0)),
        out_shape=jax.ShapeDtypeStruct((ng, nc, HEAD_DIM), bf16),
        compiler_params=_cparams(("parallel",)),
        name="compress_k" if do_norm else "compress_v",
    )(chunks, pe8, w1, w2, gain, rc, rsa, rsb)


def _sigmoid(x):
    return 1.0 / (1.0 + jnp.exp(-x))


def _nsa_cmp_kernel(q_ref, kc_ref, vc_ref, g_ref, smt_ref, oc_ref, bias_ref, *, tq):
    q0 = pl.program_id(2) * tq
    kc = kc_ref[0]
    vc = vc_ref[0]
    nc = kc.shape[0]
    R = NSA_GROUP * tq
    sig = _sigmoid(g_ref[0])
    q = q_ref[...].reshape(R, HEAD_DIM)
    s = lax.dot_general(q, kc, (((1,), (1,)), ((), ())), preferred_element_type=f32)
    t = q0 + (lax.broadcasted_iota(i32, (R, nc), 0) & (tq - 1))
    cend = lax.broadcasted_iota(i32, (R, nc), 1) * CMP_STRIDE + (CMP_LEN - 1)
    mask = cend <= t
    s = jnp.where(mask, s, NEG_INF)
    m = jnp.max(s, axis=-1, keepdims=True)
    p = jnp.where(mask, jnp.exp(s - m), 0.0)
    p = p / jnp.maximum(jnp.sum(p, axis=-1, keepdims=True), 1e-30)
    o = jnp.dot(p.astype(bf16), vc, preferred_element_type=f32)
    for gi in range(NSA_GROUP):
        col = gi * N_BRANCH
        oc_ref[:, gi * HEAD_DIM:(gi + 1) * HEAD_DIM] = o[gi * tq:(gi + 1) * tq] * sig[:, col:col + 1]

    psum = p[0:tq] + p[tq:2 * tq] + p[2 * tq:3 * tq] + p[3 * tq:4 * tq]
    ps_hi = psum.astype(bf16)
    ps_lo = (psum - ps_hi.astype(f32)).astype(bf16)
    nt = (((1,), (1,)), ((), ()))
    imp = (lax.dot_general(smt_ref[...], ps_hi, nt, preferred_element_type=f32)
           + lax.dot_general(smt_ref[...], ps_lo, nt, preferred_element_type=f32))
    nselp = imp.shape[0]
    jf = lax.broadcasted_iota(i32, (nselp, tq), 0).astype(f32)
    cur = ((q0 + lax.broadcasted_iota(i32, (nselp, tq), 1)) >> SEL_SHIFT).astype(f32)
    forced = (jf == 0.0) | (jf == cur) | (jf == cur - 1.0)
    imp = jnp.where(forced, FORCED_SCORE, imp)
    imp = jnp.where(jf <= cur, imp, NEG_INF)
    PICKED = -3e38
    work = imp
    for _ in range(N_SELECTED):
        mx = jnp.max(work, axis=0, keepdims=True)
        first = jnp.min(jnp.where(work == mx, jf, float(nselp)), axis=0, keepdims=True)
        work = jnp.where(jf == first, PICKED, work)
    bias_t = jnp.where((work < -2e38) & (imp > -1.0), 0.0, MASK_VAL)
    bias_ref[0] = bias_t.T.astype(bf16)


def _nsa_cmp(P, kcmp, vcmp, gate, smt, B, S, tq):
    T = B * S
    nqb = S // tq
    nc = kcmp.shape[1]
    nselp = smt.shape[0]
    return pl.pallas_call(
        functools.partial(_nsa_cmp_kernel, tq=tq),
        grid=(NSA_KV_HEADS, B, nqb),
        in_specs=[
            pl.BlockSpec((NSA_GROUP, tq, HEAD_DIM), lambda h, b, i: (h, b * nqb + i, 0)),
            pl.BlockSpec((1, nc, HEAD_DIM), lambda h, b, i: (h * B + b, 0, 0)),
            pl.BlockSpec((1, nc, HEAD_DIM), lambda h, b, i: (h * B + b, 0, 0)),
            pl.BlockSpec((1, tq, HEAD_DIM), lambda h, b, i: (h, b * nqb + i, 0)),
            pl.BlockSpec((nselp, nc), lambda h, b, i: (0, 0)),
        ],
        out_specs=[
            pl.BlockSpec((tq, NSA_GROUP * HEAD_DIM), lambda h, b, i: (b * nqb + i, h)),
            pl.BlockSpec((1, tq, nselp), lambda h, b, i: (h, b * nqb + i, 0)),
        ],
        out_shape=[
            jax.ShapeDtypeStruct((T, NSA_HEADS * HEAD_DIM), f32),
            jax.ShapeDtypeStruct((NSA_KV_HEADS, T, nselp), bf16),
        ],
        compiler_params=_cparams(("parallel", "parallel", "parallel")),
        name="nsa_cmp",
    )(P, kcmp, vcmp, gate, smt)


def _lane_tile(x, n):
    return x if n == 1 else jnp.concatenate([x] * n, axis=1)


ROW_CHUNK = 256


def _softmax_step(rows, s, v, m_ref, acc_ref, l_ref=None):
    tk = s.shape[1]
    m_prev = m_ref[rows, :]
    m_new = jnp.maximum(m_prev, jnp.max(s, axis=1, keepdims=True))
    alpha = jnp.exp(m_prev - m_new)
    p = jnp.exp(s - _lane_tile(m_new, tk // LANES))
    if l_ref is not None:
        l_ref[rows, :] = alpha * l_ref[rows, :] + jnp.sum(p, axis=1, keepdims=True)
    acc_ref[rows, :] = _lane_tile(alpha, 2) * acc_ref[rows, :] + jnp.dot(
        p.astype(bf16), v, preferred_element_type=f32)
    m_ref[rows, :] = m_new


def _pipelined_key_tiles(n_full, scores, softmax_pv, sa_ref, sb_ref):
    def pair(j, carry):
        scores(2 * j + 1, sb_ref)
        softmax_pv(2 * j, sa_ref, False)
        scores(2 * j + 2, sa_ref)
        softmax_pv(2 * j + 1, sb_ref, False)
        return carry

    scores(0, sa_ref)
    lax.fori_loop(0, n_full // 2, pair, 0)

    @pl.when(n_full % 2 == 0)
    def _():
        softmax_pv(n_full, sa_ref, True)

    @pl.when(n_full % 2 == 1)
    def _():
        scores(n_full, sb_ref)
        softmax_pv(n_full - 1, sa_ref, False)
        softmax_pv(n_full, sb_ref, True)


def _nsa_sel_kernel(q_ref, bias_ref, ks_ref, vs_ref, kw_ref, vw_ref, g_ref, oc_ref,
                    o_ref, qaug_ref, m_ref, acc_ref, sa_ref, sb_ref, *, tq, tk, tkw):
    q0 = pl.program_id(2) * tq
    R = NSA_GROUP * tq
    nselp = bias_ref.shape[2]
    q = q_ref[...].reshape(R, HEAD_DIM)
    qaug_ref[:, 0:HEAD_DIM] = q
    bias = bias_ref[0]
    for gi in range(NSA_GROUP):
        qaug_ref[gi * tq:(gi + 1) * tq, HEAD_DIM:HEAD_DIM + nselp] = bias
    sig = _sigmoid(g_ref[0])
    rc = min(ROW_CHUNK, tq)

    def row_t(c, width):
        return q0 + ((c * rc + lax.broadcasted_iota(i32, (rc, width), 0)) & (tq - 1))

    def init():
        m_ref[...] = jnp.full(m_ref.shape, M_INIT, f32)
        acc_ref[...] = jnp.zeros(acc_ref.shape, f32)

    def finish():
        return acc_ref[:, 0:HEAD_DIM] / jnp.maximum(acc_ref[:, HEAD_DIM:2 * HEAD_DIM], 1e-30)

    init()
    ones_k = jnp.ones((tk, HEAD_DIM), bf16)

    def sel_scores(kt, s_ref):
        k0 = pl.multiple_of(kt * tk, tk)
        k = ks_ref[0, pl.ds(k0, tk), :]
        blk = (k0 + lax.broadcasted_iota(i32, (tk, nselp), 0)) >> SEL_SHIFT
        onehot = jnp.where(blk == lax.broadcasted_iota(i32, (tk, nselp), 1), 1.0, 0.0).astype(bf16)
        kaug = jnp.concatenate([k, onehot], axis=1)
        for c in range(R // rc):
            rows = slice(c * rc, (c + 1) * rc)
            s_ref[rows, :] = lax.dot_general(qaug_ref[rows, :], kaug, (((1,), (1,)), ((), ())),
                                             preferred_element_type=f32)

    def sel_softmax_pv(kt, s_ref, diagonal):
        k0 = pl.multiple_of(kt * tk, tk)
        vaug = jnp.concatenate([vs_ref[0, pl.ds(k0, tk), :], ones_k], axis=1)
        for c in range(R // rc):
            rows = slice(c * rc, (c + 1) * rc)
            s = s_ref[rows, :]
            if diagonal:
                s = jnp.where(k0 + lax.broadcasted_iota(i32, (rc, tk), 1) <= row_t(c, tk), s, MASK_VAL)
            _softmax_step(rows, s, vaug, m_ref, acc_ref)

    _pipelined_key_tiles(q0 // tk, sel_scores, sel_softmax_pv, sa_ref, sb_ref)
    o_s = finish()

    init()
    ones_w = jnp.ones((tkw, HEAD_DIM), bf16)
    qi = pl.program_id(2)

    def win_step(kt, mode):
        k0 = pl.multiple_of(kt * tkw, tkw)
        k = kw_ref[0, pl.ds(k0, tkw), :]
        vaug = jnp.concatenate([vw_ref[0, pl.ds(k0, tkw), :], ones_w], axis=1)
        for c in range(R // rc):
            rows = slice(c * rc, (c + 1) * rc)
            s = lax.dot_general(qaug_ref[rows, 0:HEAD_DIM], k, (((1,), (1,)), ((), ())),
                                preferred_element_type=f32)
            if mode != "full":
                t = row_t(c, tkw)
                kp = k0 + lax.broadcasted_iota(i32, (rc, tkw), 1)
                s = jnp.where(kp <= t if mode == "diagonal" else t - kp < WINDOW, s, MASK_VAL)
            _softmax_step(rows, s, vaug, m_ref, acc_ref)

    @pl.when(qi >= 2)
    def _():
        win_step(qi - 2, "edge")

    @pl.when(qi >= 1)
    def _():
        win_step(qi - 1, "full")

    win_step(qi, "diagonal")
    o_w = finish()

    for gi in range(NSA_GROUP):
        rows = slice(gi * tq, (gi + 1) * tq)
        cols = slice(gi * HEAD_DIM, (gi + 1) * HEAD_DIM)
        c = gi * N_BRANCH
        o_ref[:, cols] = (oc_ref[:, cols] + o_s[rows] * sig[:, c + 1:c + 2]
                          + o_w[rows] * sig[:, c + 2:c + 3]).astype(bf16)


def _nsa_sel(Pn, Pr, bias, gate, o_c, B, S, tq, tk, tkw):
    assert tk % tq == 0 and tkw == tq and WINDOW == 2 * tq
    T = B * S
    nqb = S // tq
    nselp = bias.shape[2]
    R = NSA_GROUP * tq
    kv_spec = lambda base: pl.BlockSpec((1, S, HEAD_DIM), lambda h, b, i: (base + h, b, 0))
    return pl.pallas_call(
        functools.partial(_nsa_sel_kernel, tq=tq, tk=tk, tkw=tkw),
        grid=(NSA_KV_HEADS, B, nqb),
        in_specs=[
            pl.BlockSpec((NSA_GROUP, tq, HEAD_DIM), lambda h, b, i: (h, b * nqb + i, 0)),
            pl.BlockSpec((1, tq, nselp), lambda h, b, i: (h, b * nqb + i, 0)),
            kv_spec(HD_KS), kv_spec(HR_VS), kv_spec(HD_KW), kv_spec(HR_VW),
            pl.BlockSpec((1, tq, HEAD_DIM), lambda h, b, i: (h, b * nqb + i, 0)),
            pl.BlockSpec((tq, NSA_GROUP * HEAD_DIM), lambda h, b, i: (b * nqb + i, h)),
        ],
        out_specs=pl.BlockSpec((tq, NSA_GROUP * HEAD_DIM), lambda h, b, i: (b * nqb + i, h)),
        out_shape=jax.ShapeDtypeStruct((T, NSA_HEADS * HEAD_DIM), bf16),
        scratch_shapes=[
            pltpu.VMEM((R, HEAD_DIM + nselp), bf16),
            pltpu.VMEM((R, LANES), f32),
            pltpu.VMEM((R, 2 * HEAD_DIM), f32),
            pltpu.VMEM((R, tk), f32),
            pltpu.VMEM((R, tk), f32),
        ],
        compiler_params=_cparams(("parallel", "parallel", "arbitrary")),
        name="nsa_sel_win",
    )(Pn, bias, Pn, Pr, Pn, Pr, gate, o_c)


def _diff_kernel(q_ref, k_ref, v_ref, lam_ref, sg_ref, o_ref, m_ref, l_ref, acc_ref, sa_ref, sb_ref,
                 *, tq, tk, lambda_init):
    q0 = pl.program_id(2) * tq
    R = 2 * tq
    rc = min(ROW_CHUNK, tq)
    m_ref[...] = jnp.full(m_ref.shape, M_INIT, f32)
    l_ref[...] = jnp.zeros(l_ref.shape, f32)
    acc_ref[...] = jnp.zeros(acc_ref.shape, f32)

    def scores(kt, s_ref):
        k0 = pl.multiple_of(kt * tk, tk)
        for comp in range(2):
            s_ref[comp * tq:(comp + 1) * tq, :] = lax.dot_general(
                q_ref[comp], k_ref[comp, pl.ds(k0, tk), :], (((1,), (1,)), ((), ())),
                preferred_element_type=f32)

    def softmax_pv(kt, s_ref, diagonal):
        k0 = pl.multiple_of(kt * tk, tk)
        v = jnp.concatenate([v_ref[0, pl.ds(k0, tk), :], v_ref[1, pl.ds(k0, tk), :]], axis=1)
        for c in range(R // rc):
            rows = slice(c * rc, (c + 1) * rc)
            s = s_ref[rows, :]
            if diagonal:
                t = q0 + ((c * rc + lax.broadcasted_iota(i32, (rc, tk), 0)) & (tq - 1))
                s = jnp.where(k0 + lax.broadcasted_iota(i32, (rc, tk), 1) <= t, s, MASK_VAL)
            _softmax_step(rows, s, v, m_ref, acc_ref, l_ref)

    _pipelined_key_tiles(pl.program_id(2), scores, softmax_pv, sa_ref, sb_ref)

    lq = lam_ref[...]
    lam = (jnp.exp(jnp.sum(lq[0:1] * lq[1:2], axis=1, keepdims=True))
           - jnp.exp(jnp.sum(lq[2:3] * lq[3:4], axis=1, keepdims=True)) + lambda_init)
    o = acc_ref[...] / _lane_tile(jnp.maximum(l_ref[...], 1e-30), 2)
    a = o[0:tq] - lam * o[tq:R]
    o_ref[...] = (_rms(a, sg_ref[...]) * (1.0 - lambda_init)).astype(bf16)


def _diff_attn(Pn, Pr, lam_vecs, subln_g, B, S, tq, tk, lambda_init):
    assert tq == tk
    T = B * S
    nqb = S // tq
    return pl.pallas_call(
        functools.partial(_diff_kernel, tq=tq, tk=tk, lambda_init=lambda_init),
        grid=(DIFF_HEADS, B, nqb),
        in_specs=[
            pl.BlockSpec((2, tq, HEAD_DIM), lambda h, b, i: (HD_QD // 2 + h, b * nqb + i, 0)),
            pl.BlockSpec((2, S, HEAD_DIM), lambda h, b, i: (HD_KD // 2 + h, b, 0)),
            pl.BlockSpec((2, S, HEAD_DIM), lambda h, b, i: (HR_VD // 2 + h, b, 0)),
            pl.BlockSpec((8, HEAD_DIM), lambda h, b, i: (0, 0)),
            pl.BlockSpec((1, DIFF_V_DIM), lambda h, b, i: (0, 0)),
        ],
        out_specs=pl.BlockSpec((tq, DIFF_V_DIM), lambda h, b, i: (b * nqb + i, h)),
        out_shape=jax.ShapeDtypeStruct((T, DIFF_HEADS * DIFF_V_DIM), bf16),
        scratch_shapes=[
            pltpu.VMEM((2 * tq, LANES), f32),
            pltpu.VMEM((2 * tq, LANES), f32),
            pltpu.VMEM((2 * tq, DIFF_V_DIM), f32),
            pltpu.VMEM((2 * tq, tk), f32),
            pltpu.VMEM((2 * tq, tk), f32),
        ],
        compiler_params=_cparams(("parallel", "parallel", "arbitrary")),
        name="diff_attn",
    )(Pn, Pn, Pr, lam_vecs, subln_g)


def _outproj_kernel(x_ref, on_ref, od_ref, w_ref, g_ref, wr_ref, br_ref,
                    h_ref, hn_ref, lg_ref):
    half = NSA_HEADS * HEAD_DIM
    tm = x_ref.shape[0]
    rc = min(ROW_CHUNK, tm)
    for c in range(tm // rc):
        rows = slice(c * rc, (c + 1) * rc)
        h = x_ref[rows, :] + jnp.dot(on_ref[rows, :], w_ref[0:half], preferred_element_type=f32) \
            + jnp.dot(od_ref[rows, :], w_ref[half:], preferred_element_type=f32)
        h_ref[rows, :] = h
        hn = _rms(h, g_ref[...])
        hn_ref[rows, :] = hn
        hn_hi = hn.astype(bf16)
        hn_lo = (hn - hn_hi.astype(f32)).astype(bf16)
        r = (jnp.dot(hn_hi, wr_ref[...], preferred_element_type=f32)
             + jnp.dot(hn_lo, wr_ref[...], preferred_element_type=f32))
        lg_ref[rows, :] = r[:, 0:LANES] + r[:, LANES:2 * LANES] + br_ref[...]


def _outproj(x2, o_nsa, o_diff, w_out_b, ffn_g, w_router_p, b_router_p, tm):
    T = x2.shape[0]
    full = lambda shape: pl.BlockSpec(shape, lambda i: (0,) * len(shape))
    return pl.pallas_call(
        _outproj_kernel,
        grid=(T // tm,),
        in_specs=[
            pl.BlockSpec((tm, D_MODEL), lambda i: (i, 0)),
            pl.BlockSpec((tm, NSA_HEADS * HEAD_DIM), lambda i: (i, 0)),
            pl.BlockSpec((tm, DIFF_HEADS * DIFF_V_DIM), lambda i: (i, 0)),
            full((D_MODEL, D_MODEL)),
            full((1, D_MODEL)),
            full((D_MODEL, 2 * LANES)),
            full((1, LANES)),
        ],
        out_specs=[
            pl.BlockSpec((tm, D_MODEL), lambda i: (i, 0)),
            pl.BlockSpec((tm, D_MODEL), lambda i: (i, 0)),
            pl.BlockSpec((tm, LANES), lambda i: (i, 0)),
        ],
        out_shape=[
            jax.ShapeDtypeStruct((T, D_MODEL), f32),
            jax.ShapeDtypeStruct((T, D_MODEL), f32),
            jax.ShapeDtypeStruct((T, LANES), f32),
        ],
        compiler_params=_cparams(("parallel",)),
        name="outproj_router",
    )(x2, o_nsa, o_diff, w_out_b, ffn_g, w_router_p, b_router_p)


def _start_row_gathers(idx_fn, n, src_ref, dst_ref, sem):
    for r in range(n):
        pltpu.make_async_copy(src_ref.at[pl.ds(idx_fn(r), 1)], dst_ref.at[pl.ds(r, 1)], sem).start()


def _wait_row_gathers(n, src_ref, dst_ref, sem):
    for r in range(n):
        pltpu.make_async_copy(src_ref.at[pl.ds(0, 1)], dst_ref.at[pl.ds(r, 1)], sem).wait()


def _expert_row_blocks(first, nblk, x_hbm, o_dst, xbuf, obuf, xsem, osem, tm, compute):
    def x_copy(blk, slot):
        return pltpu.make_async_copy(x_hbm.at[pl.ds(pl.multiple_of(blk * tm, tm), tm)],
                                     xbuf.at[slot], xsem.at[slot])

    def o_copy(blk, slot):
        return pltpu.make_async_copy(obuf.at[slot], o_dst(pl.multiple_of(blk * tm, tm)), osem.at[slot])

    @pl.when(nblk > 0)
    def _():
        x_copy(first, 0).start()

    def body(i, carry):
        slot = i % 2

        @pl.when(i + 1 < nblk)
        def _():
            x_copy(first + i + 1, 1 - slot).start()

        x_copy(first + i, slot).wait()

        @pl.when(i >= 2)
        def _():
            o_copy(first + i - 2, slot).wait()

        obuf[slot] = compute(xbuf[slot])
        o_copy(first + i, slot).start()
        return carry

    lax.fori_loop(0, nblk, body, 0)

    @pl.when(nblk >= 2)
    def _():
        o_copy(first + nblk - 2, nblk % 2).wait()

    @pl.when(nblk >= 1)
    def _():
        o_copy(first + nblk - 1, (nblk - 1) % 2).wait()

    return o_copy


def _zero_unused_blocks(o_copy, obuf, lo, hi):
    obuf[0] = jnp.zeros(obuf.shape[1:], obuf.dtype)

    def body(blk, carry):
        cp = o_copy(blk, 0)
        cp.start()
        cp.wait()
        return carry

    lax.fori_loop(lo, hi, body, 0)


def _gateup_kernel(fb_ref, nbk_ref, nu_ref, st_ref, sb_ref, hn_hbm, wg_ref, wu_ref, bg_ref, bu_ref, hb_hbm,
                   wgb_ref, wub_ref, xa_ref, xb_ref, obuf, xsem, osem, *, tm, tn, n_blocks):
    j = pl.program_id(0)
    e = pl.program_id(1)
    wgb_ref[...] = wg_ref[0].astype(bf16)
    wub_ref[...] = wu_ref[0].astype(bf16)
    col0 = pl.multiple_of(j * tn, tn)
    first = fb_ref[e]
    nblk = nbk_ref[e]
    xbufs = (xa_ref, xb_ref)

    def compute(x):
        gl = jnp.dot(x, wgb_ref[...], preferred_element_type=f32) + bg_ref[0]
        up = jnp.dot(x, wub_ref[...], preferred_element_type=f32) + bu_ref[0]
        gl = jnp.minimum(gl, SWIGLU_LIMIT)
        up = jnp.clip(up, -SWIGLU_LIMIT, SWIGLU_LIMIT)
        return ((up + 1.0) * (gl * _sigmoid(SWIGLU_ALPHA * gl))).astype(bf16)

    def issue(g, p):
        base = sb_ref[g]
        _start_row_gathers(lambda r: st_ref[base + r], tm, hn_hbm, xbufs[p], xsem.at[p])

    def wait_x(p):
        _wait_row_gathers(tm, hn_hbm, xbufs[p], xsem.at[p])

    def o_copy(g, slot):
        return pltpu.make_async_copy(
            obuf.at[slot], hb_hbm.at[pl.ds(pl.multiple_of(g * tm, tm), tm), pl.ds(col0, tn)], osem.at[slot])

    @pl.when(e == 0)
    def _():
        issue(0, 0)

    def body(i, carry):
        g = first + i
        oslot = i % 2

        @pl.when(i >= 2)
        def _():
            o_copy(g - 2, oslot).wait()

        for p in range(2):
            @pl.when(g % 2 == p)
            def _():
                wait_x(p)
                issue(g + 1, 1 - p)
                obuf[oslot] = compute(xbufs[p][...].astype(bf16))

        o_copy(g, oslot).start()
        return carry

    lax.fori_loop(0, nblk, body, 0)

    @pl.when(nblk >= 2)
    def _():
        o_copy(first + nblk - 2, nblk % 2).wait()

    @pl.when(nblk >= 1)
    def _():
        o_copy(first + nblk - 1, (nblk - 1) % 2).wait()

    @pl.when(e == N_EXPERTS - 1)
    def _():
        nu = nu_ref[0]
        for p in range(2):
            @pl.when(nu % 2 == p)
            def _():
                wait_x(p)
        _zero_unused_blocks(o_copy, obuf, nu, n_blocks)


def _gateup(first_blk, n_blk, n_used, sorted_tok, src_base, hn, w_gate_up, b_gate_up3, n_blocks, tm, tn):
    nj = D_FF // tn
    return pl.pallas_call(
        functools.partial(_gateup_kernel, tm=tm, tn=tn, n_blocks=n_blocks),
        grid_spec=pltpu.PrefetchScalarGridSpec(
            num_scalar_prefetch=5,
            grid=(nj, N_EXPERTS),
            in_specs=[
                pl.BlockSpec(memory_space=pl.ANY),
                pl.BlockSpec((1, D_MODEL, tn), lambda j, e, *_: (e, 0, j)),
                pl.BlockSpec((1, D_MODEL, tn), lambda j, e, *_: (e, 0, nj + j)),
                pl.BlockSpec((1, 1, tn), lambda j, e, *_: (e, 0, j)),
                pl.BlockSpec((1, 1, tn), lambda j, e, *_: (e, 0, nj + j)),
            ],
            out_specs=pl.BlockSpec(memory_space=pl.ANY),
            scratch_shapes=[
                pltpu.VMEM((D_MODEL, tn), bf16), pltpu.VMEM((D_MODEL, tn), bf16),
                pltpu.VMEM((tm, D_MODEL), hn.dtype), pltpu.VMEM((tm, D_MODEL), hn.dtype),
                pltpu.VMEM((2, tm, tn), bf16),
                pltpu.SemaphoreType.DMA((2,)), pltpu.SemaphoreType.DMA((2,)),
            ],
        ),
        out_shape=jax.ShapeDtypeStruct((n_blocks * tm, D_FF), bf16),
        compiler_params=_cparams(("arbitrary", "arbitrary")),
        name="moe_gate_up",
    )(first_blk, n_blk, n_used, sorted_tok, src_base, hn, w_gate_up, w_gate_up, b_gate_up3, b_gate_up3)


def _down_kernel(fb_ref, nbk_ref, nu_ref, hb_hbm, w_ref, b_ref, y_hbm,
                 wb_ref, xbuf, obuf, xsem, osem, *, tm, n_blocks):
    e = pl.program_id(0)
    wb_ref[...] = w_ref[0].astype(bf16)

    def compute(x):
        return jnp.dot(x, wb_ref[...], preferred_element_type=f32) + b_ref[0]

    o_copy = _expert_row_blocks(fb_ref[e], nbk_ref[e], hb_hbm, lambda r0: y_hbm.at[pl.ds(r0, tm)],
                                xbuf, obuf, xsem, osem, tm, compute)

    @pl.when(e == N_EXPERTS - 1)
    def _():
        _zero_unused_blocks(o_copy, obuf, nu_ref[0], n_blocks)


def _down(first_blk, n_blk, n_used, hb, w_down, b_down3, tm):
    n_rows = hb.shape[0]
    return pl.pallas_call(
        functools.partial(_down_kernel, tm=tm, n_blocks=n_rows // tm),
        grid_spec=pltpu.PrefetchScalarGridSpec(
            num_scalar_prefetch=3,
            grid=(N_EXPERTS,),
            in_specs=[
                pl.BlockSpec(memory_space=pl.ANY),
                pl.BlockSpec((1, D_FF, D_MODEL), lambda e, *_: (e, 0, 0)),
                pl.BlockSpec((1, 1, D_MODEL), lambda e, *_: (e, 0, 0)),
            ],
            out_specs=pl.BlockSpec(memory_space=pl.ANY),
            scratch_shapes=[
                pltpu.VMEM((D_FF, D_MODEL), bf16),
                pltpu.VMEM((2, tm, D_FF), bf16), pltpu.VMEM((2, tm, D_MODEL), f32),
                pltpu.SemaphoreType.DMA((2,)), pltpu.SemaphoreType.DMA((2,)),
            ],
        ),
        out_shape=jax.ShapeDtypeStruct((n_rows, D_MODEL), f32),
        compiler_params=_cparams(("arbitrary",)),
        name="moe_down",
    )(first_blk, n_blk, n_used, hb, w_down, b_down3)


def _combine_kernel(pos_ref, h_ref, gt_ref, y_ref, o_ref, buf_ref, sem, *, tc):
    i = pl.program_id(0)
    n = pl.num_programs(0)
    rows = TOP_K * tc

    def issue(step, slot):
        base = step * rows
        _start_row_gathers(lambda r: pos_ref[base + r], rows, y_ref, buf_ref.at[slot], sem.at[slot])

    @pl.when(i == 0)
    def _():
        issue(0, 0)

    @pl.when(i + 1 < n)
    def _():
        issue(i + 1, (i + 1) % 2)

    slot = i % 2
    _wait_row_gathers(rows, y_ref, buf_ref.at[slot], sem.at[slot])
    gt = gt_ref[...]
    out = h_ref[...]
    for k in range(TOP_K):
        out = out + gt[:, k:k + 1] * buf_ref[slot, k * tc:(k + 1) * tc, :]
    o_ref[...] = out


def _combine(pos_blocks, h, gate_pad, y, tc):
    T = h.shape[0]
    return pl.pallas_call(
        functools.partial(_combine_kernel, tc=tc),
        grid_spec=pltpu.PrefetchScalarGridSpec(
            num_scalar_prefetch=1,
            grid=(T // tc,),
            in_specs=[
                pl.BlockSpec((tc, D_MODEL), lambda i, pos: (i, 0)),
                pl.BlockSpec((tc, LANES), lambda i, pos: (i, 0)),
                pl.BlockSpec(memory_space=pl.ANY),
            ],
            out_specs=pl.BlockSpec((tc, D_MODEL), lambda i, pos: (i, 0)),
            scratch_shapes=[pltpu.VMEM((2, TOP_K * tc, D_MODEL), f32), pltpu.SemaphoreType.DMA((2,))],
        ),
        out_shape=jax.ShapeDtypeStruct((T, D_MODEL), f32),
        compiler_params=_cparams(("arbitrary",)),
        name="moe_combine",
    )(pos_blocks, h, gate_pad, y)


def _rearranged_w_in(w_in):
    splits = np.cumsum([0, 1024, 256, 256, 256, 256, 256, 256, 24, 1024, 1024, 1024])
    q, kc, vc, ks, vs, kw, vw, g, qd, kd, vd = [w_in[:, splits[n]:splits[n + 1]] for n in range(11)]
    gcols = NSA_GROUP * N_BRANCH
    zpad = lambda n: jnp.zeros((D_MODEL, n), w_in.dtype)
    w_norm = jnp.concatenate([q, ks, kw, qd, kd], axis=1)
    w_raw = jnp.concatenate([kc, vc, vs, vw, vd,
                             g[:, :gcols], zpad(HEAD_DIM - gcols), g[:, gcols:], zpad(HEAD_DIM - gcols),
                             zpad((N_RAW_HEADS - HR_GATE - 2) * HEAD_DIM)], axis=1)
    return w_norm.astype(bf16), w_raw.astype(bf16)


def _sel_map_t(S, nselp):
    n_chunks = S // CMP_STRIDE
    n_cmp = (S - CMP_LEN) // CMP_STRIDE + 1
    n_sel = S // SEL_BLOCK
    c_start = np.arange(n_chunks) * CMP_STRIDE
    s_start = np.arange(nselp) * SEL_BLOCK
    ov = (c_start[None, :] < s_start[:, None] + SEL_BLOCK) & (c_start[None, :] + CMP_LEN > s_start[:, None])
    ov &= (np.arange(n_chunks)[None, :] < n_cmp) & (np.arange(nselp)[:, None] < n_sel)
    return jnp.asarray(ov, bf16)


def _attention_block(x2, B, S, attn_norm_g, w_in, nsa_q_norm_g, nsa_k_norm_g, pos_k, pos_v,
                     k_w1, k_w2, v_w1, v_w2, diff_q_norm_g, diff_k_norm_g, lq1, lk1, lq2, lk2,
                     subln_g, lambda_init):
    T = B * S
    scale = HEAD_DIM ** -0.5
    tiles = _tiles(S)
    tm = tiles["proj_rows"]
    rope_c, rope_sa, rope_sb = _rope_tables(np.arange(S))
    gains = jnp.concatenate([
        jnp.broadcast_to(nsa_q_norm_g * scale, (NSA_HEADS, HEAD_DIM)),
        jnp.broadcast_to(nsa_k_norm_g[1], (NSA_KV_HEADS, HEAD_DIM)),
        jnp.broadcast_to(nsa_k_norm_g[2], (NSA_KV_HEADS, HEAD_DIM)),
        jnp.broadcast_to(diff_q_norm_g * scale, (2 * DIFF_HEADS, HEAD_DIM)),
        jnp.broadcast_to(diff_k_norm_g, (2 * DIFF_HEADS, HEAD_DIM)),
    ], axis=0).reshape(N_NORM_HEADS, 1, HEAD_DIM)
    w_norm, w_raw = _rearranged_w_in(w_in)
    Pn, Pr, gate = _inproj(x2, attn_norm_g.reshape(1, D_MODEL), w_norm, w_raw, gains,
                           rope_c, rope_sa, rope_sb, S, tm)

    n_chunks = S // CMP_STRIDE
    cmp_pos = np.arange(n_chunks) * CMP_STRIDE + (CMP_LEN - 1)
    cc, csa, csb = _rope_tables(cmp_pos)
    chunk_view = lambda hd: Pr[hd:hd + NSA_KV_HEADS].reshape(NSA_KV_HEADS * B, n_chunks, CMP_STRIDE * HEAD_DIM)
    kcmp = _compress(chunk_view(HR_KC), pos_k, k_w1, k_w2, nsa_k_norm_g[0].reshape(1, HEAD_DIM),
                     cc, csa, csb, True)
    vcmp = _compress(chunk_view(HR_VC), pos_v, v_w1, v_w2, nsa_k_norm_g[0].reshape(1, HEAD_DIM),
                     cc, csa, csb, False)

    nselp = -(-(S // SEL_BLOCK) // LANES) * LANES
    tq = tiles["attn_q"]
    o_c, bias = _nsa_cmp(Pn, kcmp, vcmp, gate, _sel_map_t(S, nselp), B, S, tq)
    o_nsa = _nsa_sel(Pn, Pr, bias, gate, o_c, B, S, tq, tiles["sel_keys"], tq)

    lam_vecs = jnp.concatenate([lq1[None], lk1[None], lq2[None], lk2[None],
                                jnp.zeros((4, HEAD_DIM), f32)], axis=0)
    o_diff = _diff_attn(Pn, Pr, lam_vecs, subln_g.reshape(1, DIFF_V_DIM), B, S,
                        tiles["diff_q"], tiles["diff_q"], lambda_init)
    return o_nsa, o_diff


def _moe_block(h, hn, logits, w_gate_up, b_gate_up, w_down, b_down, tiles):
    T = h.shape[0]
    tm = tiles["moe_rows"]
    TK = T * TOP_K
    top_v, top_e = lax.top_k(logits[:, :N_EXPERTS], TOP_K)
    gate = jax.nn.softmax(top_v, axis=-1)
    flat_e = top_e.reshape(TK).astype(i32)
    se, order = lax.sort((flat_e, jnp.arange(TK, dtype=i32)), num_keys=1, is_stable=True)
    st = order // TOP_K
    experts = jnp.arange(N_EXPERTS, dtype=i32)
    counts = jnp.sum((flat_e[:, None] == experts[None, :]).astype(i32), axis=0)
    padded = (counts + tm - 1) // tm * tm
    pad_end = jnp.cumsum(padded)
    pad_start = pad_end - padded
    cnt_start = jnp.cumsum(counts) - counts
    dest = pad_start[se] + (jnp.arange(TK, dtype=i32) - cnt_start[se])
    _, pos = lax.sort((order, dest), num_keys=1)
    n_blocks = -(-TK // tm) + N_EXPERTS
    blk_start = jnp.arange(n_blocks, dtype=i32) * tm
    blk_e = jnp.minimum(jnp.sum((pad_end[None, :] <= blk_start[:, None]).astype(i32), axis=1), N_EXPERTS - 1)
    src_base = cnt_start[blk_e] + (blk_start - pad_start[blk_e])
    n_used = (pad_end[-1] // tm).astype(i32).reshape(1)

    src_base = jnp.minimum(jnp.concatenate([src_base, src_base[-1:]]), TK).astype(i32)
    st = jnp.concatenate([st, jnp.zeros((tm,), i32)])
    first_blk = (pad_start // tm).astype(i32)
    n_blk = (padded // tm).astype(i32)
    hb = _gateup(first_blk, n_blk, n_used, st, src_base, hn, w_gate_up,
                 b_gate_up.reshape(N_EXPERTS, 1, 2 * D_FF), n_blocks, tm, tiles["moe_cols"])
    y = _down(first_blk, n_blk, n_used, hb, w_down, b_down.reshape(N_EXPERTS, 1, D_MODEL), tm)

    tc = tiles["combine_rows"]
    pos_blocks = pos.reshape(T // tc, tc, TOP_K).transpose(0, 2, 1).reshape(TK)
    gate_pad = jnp.pad(gate, ((0, 0), (0, LANES - TOP_K)))
    return _combine(pos_blocks, h, gate_pad, y, tc)


def kernel(x, attn_norm_g, w_in, nsa_q_norm_g, nsa_k_norm_g, nsa_cmp_pos_k, nsa_cmp_pos_v, nsa_cmp_k_w1, nsa_cmp_k_w2, nsa_cmp_v_w1, nsa_cmp_v_w2, diff_q_norm_g, diff_k_norm_g, diff_lambda_q1, diff_lambda_k1, diff_lambda_q2, diff_lambda_k2, diff_subln_g, w_out, ffn_norm_g, w_router, b_router, w_gate_up, b_gate_up, w_down, b_down):
    B, S, _ = x.shape
    T = B * S
    depth = attn_norm_g.shape[0]
    h = x.reshape(T, D_MODEL)
    for l in range(depth):
        lambda_init = 0.8 - 0.6 * math.exp(-0.3 * l)
        o_nsa, o_diff = _attention_block(
            h, B, S, attn_norm_g[l], w_in[l], nsa_q_norm_g[l], nsa_k_norm_g[l], nsa_cmp_pos_k[l],
            nsa_cmp_pos_v[l], nsa_cmp_k_w1[l], nsa_cmp_k_w2[l], nsa_cmp_v_w1[l], nsa_cmp_v_w2[l],
            diff_q_norm_g[l], diff_k_norm_g[l], diff_lambda_q1[l], diff_lambda_k1[l],
            diff_lambda_q2[l], diff_lambda_k2[l], diff_subln_g[l], lambda_init)
        w_router_f = jnp.pad(w_router[l], ((0, 0), (0, LANES - N_EXPERTS)))
        w_router_hi = w_router_f.astype(bf16)
        w_router_lo = (w_router_f - w_router_hi.astype(f32)).astype(bf16)
        w_router_p = jnp.concatenate([w_router_hi, w_router_lo], axis=1)
        b_router_p = jnp.pad(b_router[l], (0, LANES - N_EXPERTS)).reshape(1, LANES)
        h_mid, hn, logits = _outproj(h, o_nsa, o_diff, w_out[l].astype(bf16),
                                     ffn_norm_g[l].reshape(1, D_MODEL), w_router_p, b_router_p,
                                     _tiles(S)["out_rows"])
        h = _moe_block(h_mid, hn, logits, w_gate_up[l], b_gate_up[l], w_down[l], b_down[l], _tiles(S))
    return h.reshape(B, S, D_MODEL)
```

```python
import functools
import math

import numpy as np
import jax
import jax.numpy as jnp
from jax import lax
from jax.experimental import pallas as pl
from jax.experimental.pallas import tpu as pltpu

f32 = jnp.float32
bf16 = jnp.bfloat16
i32 = jnp.int32

D_MODEL = 2048
HEAD_DIM = 128
ROT_DIM = HEAD_DIM // 4
ROPE_THETA = 500000.0
NORM_EPS = 1e-6
NEG_INF = -1e30

NSA_HEADS = 8
NSA_KV_HEADS = 2
NSA_GROUP = NSA_HEADS // NSA_KV_HEADS
N_BRANCH = 3
CMP_LEN = 32
CMP_STRIDE = 16
CMP_HIDDEN = 2 * HEAD_DIM
SEL_BLOCK = 64
SEL_SHIFT = 6
N_SELECTED = 16
WINDOW = 512
FORCED_SCORE = 1e9

DIFF_HEADS = 4
DIFF_V_DIM = 2 * HEAD_DIM

N_EXPERTS = 32
TOP_K = 4
D_FF = D_MODEL
SWIGLU_ALPHA = 1.702
SWIGLU_LIMIT = 7.0

LANES = 128

HD_Q = 0
HD_KS = 8
HD_KW = 10
HD_QD = 12
HD_KD = 20
N_NORM_HEADS = 28
HR_KC = 0
HR_VC = 2
HR_VS = 4
HR_VW = 6
HR_VD = 8
HR_GATE = 16
N_RAW_HEADS = 20
HP = 4

MASK_VAL = -1e30
M_INIT = -5e29

VMEM_LIMIT = 56 * 1024 * 1024


def _tiles(S):
    return dict(
        proj_rows=min(1024, S),
        attn_q=min(256, S),
        sel_keys=min(1024, S),
        diff_q=min(512, S),
        out_rows=512,
        moe_rows=256,
        moe_cols=1024,
        combine_rows=128,
    )


def _cparams(sem):
    return pltpu.CompilerParams(dimension_semantics=sem, vmem_limit_bytes=VMEM_LIMIT)


def _rope_tables(pos):
    inv = np.power(ROPE_THETA, -np.arange(0, ROT_DIM, 2, dtype=np.float64) / ROT_DIM)
    ang = pos.astype(np.float64)[:, None] * inv[None, :]
    cos, sin = np.cos(ang), np.sin(ang)
    n = pos.shape[0]
    half = ROT_DIM // 2
    c = np.concatenate([cos, cos, np.ones((n, HEAD_DIM - ROT_DIM))], axis=1)
    sa = np.concatenate([-sin, np.zeros((n, HEAD_DIM - half))], axis=1)
    sb = np.concatenate([np.zeros((n, half)), sin, np.zeros((n, HEAD_DIM - ROT_DIM))], axis=1)
    return (jnp.asarray(c, f32), jnp.asarray(sa, f32), jnp.asarray(sb, f32))


def _rope(y, c, sa, sb):
    half = ROT_DIM // 2
    return (y * c + pltpu.roll(y, HEAD_DIM - half, 1) * sa + pltpu.roll(y, half, 1) * sb)


def _rms(y, gain):
    ms = jnp.mean(y * y, axis=-1, keepdims=True)
    return y * lax.rsqrt(ms + NORM_EPS) * gain


def _inproj_norm_kernel(x_ref, g_ref, w_ref, gain_ref, c_ref, sa_ref, sb_ref, o_ref, xn_ref):
    @pl.when(pl.program_id(1) == 0)
    def _():
        xn_ref[...] = _rms(x_ref[...], g_ref[...]).astype(bf16)

    tm = x_ref.shape[0]
    rc = min(ROW_CHUNK, tm)
    for r in range(tm // rc):
        rows = slice(r * rc, (r + 1) * rc)
        acc = jnp.dot(xn_ref[rows, :], w_ref[...], preferred_element_type=f32)
        c, sa, sb = c_ref[rows, :], sa_ref[rows, :], sb_ref[rows, :]
        for u in range(HP):
            y = _rms(acc[:, u * HEAD_DIM:(u + 1) * HEAD_DIM], gain_ref[u])
            o_ref[u, rows, :] = _rope(y, c, sa, sb).astype(bf16)


def _inproj_raw_kernel(x_ref, g_ref, w_ref, o_ref, gate_ref, xn_ref):
    j = pl.program_id(1)

    @pl.when(j == 0)
    def _():
        xn_ref[...] = _rms(x_ref[...], g_ref[...]).astype(bf16)

    tm = x_ref.shape[0]
    rc = min(ROW_CHUNK, tm)
    for r in range(tm // rc):
        rows = slice(r * rc, (r + 1) * rc)
        acc = jnp.dot(xn_ref[rows, :], w_ref[...], preferred_element_type=f32)
        for u in range(HP):
            o_ref[u, rows, :] = acc[:, u * HEAD_DIM:(u + 1) * HEAD_DIM].astype(bf16)

    @pl.when(j == HR_GATE // HP)
    def _():
        gate_ref[0] = jnp.dot(xn_ref[...], w_ref[:, 0:HEAD_DIM], preferred_element_type=f32)
        gate_ref[1] = jnp.dot(xn_ref[...], w_ref[:, HEAD_DIM:2 * HEAD_DIM], preferred_element_type=f32)


def _inproj(x2, attn_g, w_norm, w_raw, gains, rope_c, rope_sa, rope_sb, S, tm):
    T = x2.shape[0]
    nsb = S // tm
    x_specs = [
        pl.BlockSpec((tm, D_MODEL), lambda i, j: (i, 0)),
        pl.BlockSpec((1, D_MODEL), lambda i, j: (0, 0)),
        pl.BlockSpec((D_MODEL, HP * HEAD_DIM), lambda i, j: (0, j)),
    ]
    head_spec = pl.BlockSpec((HP, tm, HEAD_DIM), lambda i, j: (j, i, 0))
    pos_spec = pl.BlockSpec((tm, HEAD_DIM), lambda i, j: (i % nsb, 0))
    Pn = pl.pallas_call(
        _inproj_norm_kernel,
        grid=(T // tm, N_NORM_HEADS // HP),
        in_specs=x_specs + [pl.BlockSpec((HP, 1, HEAD_DIM), lambda i, j: (j, 0, 0)),
                            pos_spec, pos_spec, pos_spec],
        out_specs=head_spec,
        out_shape=jax.ShapeDtypeStruct((N_NORM_HEADS, T, HEAD_DIM), bf16),
        scratch_shapes=[pltpu.VMEM((tm, D_MODEL), bf16)],
        compiler_params=_cparams(("parallel", "arbitrary")),
        name="inproj_norm",
    )(x2, attn_g, w_norm, gains, rope_c, rope_sa, rope_sb)
    Pr, gate = pl.pallas_call(
        _inproj_raw_kernel,
        grid=(T // tm, N_RAW_HEADS // HP),
        in_specs=x_specs,
        out_specs=[head_spec, pl.BlockSpec((2, tm, HEAD_DIM), lambda i, j: (0, i, 0))],
        out_shape=[
            jax.ShapeDtypeStruct((N_RAW_HEADS, T, HEAD_DIM), bf16),
            jax.ShapeDtypeStruct((2, T, HEAD_DIM), f32),
        ],
        scratch_shapes=[pltpu.VMEM((tm, D_MODEL), bf16)],
        compiler_params=_cparams(("parallel", "arbitrary")),
        name="inproj_raw",
    )(x2, attn_g, w_raw)
    return Pn, Pr, gate


def _gelu_tanh(x):
    return 0.5 * x * (1.0 + jnp.tanh(math.sqrt(2.0 / math.pi) * (x + 0.044715 * (x * x * x))))


def _compress_kernel(c_ref, pe_ref, w1_ref, w2_ref, gain_ref, rc_ref, rsa_ref, rsb_ref,
                     o_ref, *, do_norm):
    half = CMP_STRIDE * HEAD_DIM
    c = c_ref[0]
    nc = c.shape[0]
    w1 = w1_ref[...].astype(bf16)
    a = jnp.dot(c, w1[:half], preferred_element_type=f32)
    b = jnp.dot(c, w1[half:], preferred_element_type=f32)
    peb = jnp.dot(pe_ref[...], w1, preferred_element_type=f32)[0:1]
    hid = a + pltpu.roll(b, nc - 1, 0) + peb
    act = _gelu_tanh(hid)
    out = jnp.dot(act.astype(bf16), w2_ref[...].astype(bf16), preferred_element_type=f32)
    if do_norm:
        out = _rope(_rms(out, gain_ref[...]), rc_ref[...], rsa_ref[...], rsb_ref[...])
    o_ref[0] = out.astype(bf16)


def _compress(chunks, pe, w1, w2, gain, rc, rsa, rsb, do_norm):
    ng, nc, width = chunks.shape
    pe8 = jnp.broadcast_to(pe.reshape(1, CMP_LEN * HEAD_DIM), (8, CMP_LEN * HEAD_DIM)).astype(bf16)
    full = lambda shape: pl.BlockSpec(shape, lambda g: (0,) * len(shape))
    return pl.pallas_call(
        functools.partial(_compress_kernel, do_norm=do_norm),
        grid=(ng,),
        in_specs=[
            pl.BlockSpec((1, nc, width), lambda g: (g, 0, 0)),
            full((8, CMP_LEN * HEAD_DIM)),
            full((CMP_LEN * HEAD_DIM, CMP_HIDDEN)),
            full((CMP_HIDDEN, HEAD_DIM)),
            full((1, HEAD_DIM)),
            full((nc, HEAD_DIM)), full((nc, HEAD_DIM)), full((nc, HEAD_DIM)),
        ],
        out_specs=pl.BlockSpec((1, nc, HEAD_DIM), lambda g: (g, 0, 0)),
        out_shape=jax.ShapeDtypeStruct((ng, nc, HEAD_DIM), bf16),
        compiler_params=_cparams(("parallel",)),
        name="compress_k" if do_norm else "compress_v",
    )(chunks, pe8, w1, w2, gain, rc, rsa, rsb)


def _sigmoid(x):
    return 1.0 / (1.0 + jnp.exp(-x))


def _nsa_cmp_kernel(q_ref, kc_ref, vc_ref, g_ref, smt_ref, oc_ref, bias_ref, *, tq):
    q0 = pl.program_id(2) * tq
    kc = kc_ref[0]
    vc = vc_ref[0]
    nc = kc.shape[0]
    R = NSA_GROUP * tq
    sig = _sigmoid(g_ref[0])
    q = q_ref[...].reshape(R, HEAD_DIM)
    s = lax.dot_general(q, kc, (((1,), (1,)), ((), ())), preferred_element_type=f32)
    t = q0 + (lax.broadcasted_iota(i32, (R, nc), 0) & (tq - 1))
    cend = lax.broadcasted_iota(i32, (R, nc), 1) * CMP_STRIDE + (CMP_LEN - 1)
    mask = cend <= t
    s = jnp.where(mask, s, NEG_INF)
    m = jnp.max(s, axis=-1, keepdims=True)
    p = jnp.where(mask, jnp.exp(s - m), 0.0)
    p = p / jnp.maximum(jnp.sum(p, axis=-1, keepdims=True), 1e-30)
    o = jnp.dot(p.astype(bf16), vc, preferred_element_type=f32)
    for gi in range(NSA_GROUP):
        col = gi * N_BRANCH
        oc_ref[:, gi * HEAD_DIM:(gi + 1) * HEAD_DIM] = o[gi * tq:(gi + 1) * tq] * sig[:, col:col + 1]

    psum = p[0:tq] + p[tq:2 * tq] + p[2 * tq:3 * tq] + p[3 * tq:4 * tq]
    ps_hi = psum.astype(bf16)
    ps_lo = (psum - ps_hi.astype(f32)).astype(bf16)
    nt = (((1,), (1,)), ((), ()))
    imp = (lax.dot_general(smt_ref[...], ps_hi, nt, preferred_element_type=f32)
           + lax.dot_general(smt_ref[...], ps_lo, nt, preferred_element_type=f32))
    nselp = imp.shape[0]
    jf = lax.broadcasted_iota(i32, (nselp, tq), 0).astype(f32)
    cur = ((q0 + lax.broadcasted_iota(i32, (nselp, tq), 1)) >> SEL_SHIFT).astype(f32)
    forced = (jf == 0.0) | (jf == cur) | (jf == cur - 1.0)
    imp = jnp.where(forced, FORCED_SCORE, imp)
    imp = jnp.where(jf <= cur, imp, NEG_INF)
    PICKED = -3e38
    work = imp
    for _ in range(N_SELECTED):
        mx = jnp.max(work, axis=0, keepdims=True)
        first = jnp.min(jnp.where(work == mx, jf, float(nselp)), axis=0, keepdims=True)
        work = jnp.where(jf == first, PICKED, work)
    bias_t = jnp.where((work < -2e38) & (imp > -1.0), 0.0, MASK_VAL)
    bias_ref[0] = bias_t.T.astype(bf16)


def _nsa_cmp(P, kcmp, vcmp, gate, smt, B, S, tq):
    T = B * S
    nqb = S // tq
    nc = kcmp.shape[1]
    nselp = smt.shape[0]
    return pl.pallas_call(
        functools.partial(_nsa_cmp_kernel, tq=tq),
        grid=(NSA_KV_HEADS, B, nqb),
        in_specs=[
            pl.BlockSpec((NSA_GROUP, tq, HEAD_DIM), lambda h, b, i: (h, b * nqb + i, 0)),
            pl.BlockSpec((1, nc, HEAD_DIM), lambda h, b, i: (h * B + b, 0, 0)),
            pl.BlockSpec((1, nc, HEAD_DIM), lambda h, b, i: (h * B + b, 0, 0)),
            pl.BlockSpec((1, tq, HEAD_DIM), lambda h, b, i: (h, b * nqb + i, 0)),
            pl.BlockSpec((nselp, nc), lambda h, b, i: (0, 0)),
        ],
        out_specs=[
            pl.BlockSpec((tq, NSA_GROUP * HEAD_DIM), lambda h, b, i: (b * nqb + i, h)),
            pl.BlockSpec((1, tq, nselp), lambda h, b, i: (h, b * nqb + i, 0)),
        ],
        out_shape=[
            jax.ShapeDtypeStruct((T, NSA_HEADS * HEAD_DIM), f32),
            jax.ShapeDtypeStruct((NSA_KV_HEADS, T, nselp), bf16),
        ],
        compiler_params=_cparams(("parallel", "parallel", "parallel")),
        name="nsa_cmp",
    )(P, kcmp, vcmp, gate, smt)


def _lane_tile(x, n):
    return x if n == 1 else jnp.concatenate([x] * n, axis=1)


ROW_CHUNK = 256


def _softmax_step(rows, s, v, m_ref, acc_ref, l_ref=None):
    tk = s.shape[1]
    m_prev = m_ref[rows, :]
    m_new = jnp.maximum(m_prev, jnp.max(s, axis=1, keepdims=True))
    alpha = jnp.exp(m_prev - m_new)
    p = jnp.exp(s - _lane_tile(m_new, tk // LANES))
    if l_ref is not None:
        l_ref[rows, :] = alpha * l_ref[rows, :] + jnp.sum(p, axis=1, keepdims=True)
    acc_ref[rows, :] = _lane_tile(alpha, 2) * acc_ref[rows, :] + jnp.dot(
        p.astype(bf16), v, preferred_element_type=f32)
    m_ref[rows, :] = m_new


def _pipelined_key_tiles(n_full, scores, softmax_pv, sa_ref, sb_ref):
    def pair(j, carry):
        scores(2 * j + 1, sb_ref)
        softmax_pv(2 * j, sa_ref, False)
        scores(2 * j + 2, sa_ref)
        softmax_pv(2 * j + 1, sb_ref, False)
        return carry

    scores(0, sa_ref)
    lax.fori_loop(0, n_full // 2, pair, 0)

    @pl.when(n_full % 2 == 0)
    def _():
        softmax_pv(n_full, sa_ref, True)

    @pl.when(n_full % 2 == 1)
    def _():
        scores(n_full, sb_ref)
        softmax_pv(n_full - 1, sa_ref, False)
        softmax_pv(n_full, sb_ref, True)


def _nsa_sel_kernel(q_ref, bias_ref, ks_ref, vs_ref, kw_ref, vw_ref, g_ref, oc_ref,
                    o_ref, qaug_ref, m_ref, acc_ref, sa_ref, sb_ref, *, tq, tk, tkw):
    q0 = pl.program_id(2) * tq
    R = NSA_GROUP * tq
    nselp = bias_ref.shape[2]
    q = q_ref[...].reshape(R, HEAD_DIM)
    qaug_ref[:, 0:HEAD_DIM] = q
    bias = bias_ref[0]
    for gi in range(NSA_GROUP):
        qaug_ref[gi * tq:(gi + 1) * tq, HEAD_DIM:HEAD_DIM + nselp] = bias
    sig = _sigmoid(g_ref[0])
    rc = min(ROW_CHUNK, tq)

    def row_t(c, width):
        return q0 + ((c * rc + lax.broadcasted_iota(i32, (rc, width), 0)) & (tq - 1))

    def init():
        m_ref[...] = jnp.full(m_ref.shape, M_INIT, f32)
        acc_ref[...] = jnp.zeros(acc_ref.shape, f32)

    def finish():
        return acc_ref[:, 0:HEAD_DIM] / jnp.maximum(acc_ref[:, HEAD_DIM:2 * HEAD_DIM], 1e-30)

    init()
    ones_k = jnp.ones((tk, HEAD_DIM), bf16)

    def sel_scores(kt, s_ref):
        k0 = pl.multiple_of(kt * tk, tk)
        k = ks_ref[0, pl.ds(k0, tk), :]
        blk = (k0 + lax.broadcasted_iota(i32, (tk, nselp), 0)) >> SEL_SHIFT
        onehot = jnp.where(blk == lax.broadcasted_iota(i32, (tk, nselp), 1), 1.0, 0.0).astype(bf16)
        kaug = jnp.concatenate([k, onehot], axis=1)
        for c in range(R // rc):
            rows = slice(c * rc, (c + 1) * rc)
            s_ref[rows, :] = lax.dot_general(qaug_ref[rows, :], kaug, (((1,), (1,)), ((), ())),
                                             preferred_element_type=f32)

    def sel_softmax_pv(kt, s_ref, diagonal):
        k0 = pl.multiple_of(kt * tk, tk)
        vaug = jnp.concatenate([vs_ref[0, pl.ds(k0, tk), :], ones_k], axis=1)
        for c in range(R // rc):
            rows = slice(c * rc, (c + 1) * rc)
            s = s_ref[rows, :]
            if diagonal:
                s = jnp.where(k0 + lax.broadcasted_iota(i32, (rc, tk), 1) <= row_t(c, tk), s, MASK_VAL)
            _softmax_step(rows, s, vaug, m_ref, acc_ref)

    _pipelined_key_tiles(q0 // tk, sel_scores, sel_softmax_pv, sa_ref, sb_ref)
    o_s = finish()

    init()
    ones_w = jnp.ones((tkw, HEAD_DIM), bf16)
    qi = pl.program_id(2)

    def win_step(kt, mode):
        k0 = pl.multiple_of(kt * tkw, tkw)
        k = kw_ref[0, pl.ds(k0, tkw), :]
        vaug = jnp.concatenate([vw_ref[0, pl.ds(k0, tkw), :], ones_w], axis=1)
        for c in range(R // rc):
            rows = slice(c * rc, (c + 1) * rc)
            s = lax.dot_general(qaug_ref[rows, 0:HEAD_DIM], k, (((1,), (1,)), ((), ())),
                                preferred_element_type=f32)
            if mode != "full":
                t = row_t(c, tkw)
                kp = k0 + lax.broadcasted_iota(i32, (rc, tkw), 1)
                s = jnp.where(kp <= t if mode == "diagonal" else t - kp < WINDOW, s, MASK_VAL)
            _softmax_step(rows, s, vaug, m_ref, acc_ref)

    @pl.when(qi >= 2)
    def _():
        win_step(qi - 2, "edge")

    @pl.when(qi >= 1)
    def _():
        win_step(qi - 1, "full")

    win_step(qi, "diagonal")
    o_w = finish()

    for gi in range(NSA_GROUP):
        rows = slice(gi * tq, (gi + 1) * tq)
        cols = slice(gi * HEAD_DIM, (gi + 1) * HEAD_DIM)
        c = gi * N_BRANCH
        o_ref[:, cols] = (oc_ref[:, cols] + o_s[rows] * sig[:, c + 1:c + 2]
                          + o_w[rows] * sig[:, c + 2:c + 3]).astype(bf16)


def _nsa_sel(Pn, Pr, bias, gate, o_c, B, S, tq, tk, tkw):
    assert tk % tq == 0 and tkw == tq and WINDOW == 2 * tq
    T = B * S
    nqb = S // tq
    nselp = bias.shape[2]
    R = NSA_GROUP * tq
    kv_spec = lambda base: pl.BlockSpec((1, S, HEAD_DIM), lambda h, b, i: (base + h, b, 0))
    return pl.pallas_call(
        functools.partial(_nsa_sel_kernel, tq=tq, tk=tk, tkw=tkw),
        grid=(NSA_KV_HEADS, B, nqb),
        in_specs=[
            pl.BlockSpec((NSA_GROUP, tq, HEAD_DIM), lambda h, b, i: (h, b * nqb + i, 0)),
            pl.BlockSpec((1, tq, nselp), lambda h, b, i: (h, b * nqb + i, 0)),
            kv_spec(HD_KS), kv_spec(HR_VS), kv_spec(HD_KW), kv_spec(HR_VW),
            pl.BlockSpec((1, tq, HEAD_DIM), lambda h, b, i: (h, b * nqb + i, 0)),
            pl.BlockSpec((tq, NSA_GROUP * HEAD_DIM), lambda h, b, i: (b * nqb + i, h)),
        ],
        out_specs=pl.BlockSpec((tq, NSA_GROUP * HEAD_DIM), lambda h, b, i: (b * nqb + i, h)),
        out_shape=jax.ShapeDtypeStruct((T, NSA_HEADS * HEAD_DIM), bf16),
        scratch_shapes=[
            pltpu.VMEM((R, HEAD_DIM + nselp), bf16),
            pltpu.VMEM((R, LANES), f32),
            pltpu.VMEM((R, 2 * HEAD_DIM), f32),
            pltpu.VMEM((R, tk), f32),
            pltpu.VMEM((R, tk), f32),
        ],
        compiler_params=_cparams(("parallel", "parallel", "arbitrary")),
        name="nsa_sel_win",
    )(Pn, bias, Pn, Pr, Pn, Pr, gate, o_c)


def _diff_kernel(q_ref, k_ref, v_ref, lam_ref, sg_ref, o_ref, m_ref, l_ref, acc_ref, sa_ref, sb_ref,
                 *, tq, tk, lambda_init):
    q0 = pl.program_id(2) * tq
    R = 2 * tq
    rc = tq
    m_ref[...] = jnp.full(m_ref.shape, M_INIT, f32)
    l_ref[...] = jnp.zeros(l_ref.shape, f32)
    acc_ref[...] = jnp.zeros(acc_ref.shape, f32)

    def scores(kt, s_ref):
        k0 = pl.multiple_of(kt * tk, tk)
        for comp in range(2):
            s_ref[comp * tq:(comp + 1) * tq, :] = lax.dot_general(
                q_ref[comp], k_ref[comp, pl.ds(k0, tk), :], (((1,), (1,)), ((), ())),
                preferred_element_type=f32)

    def softmax_pv(kt, s_ref, diagonal):
        k0 = pl.multiple_of(kt * tk, tk)
        v = jnp.concatenate([v_ref[0, pl.ds(k0, tk), :], v_ref[1, pl.ds(k0, tk), :]], axis=1)
        for c in range(R // rc):
            rows = slice(c * rc, (c + 1) * rc)
            s = s_ref[rows, :]
            if diagonal:
                t = q0 + ((c * rc + lax.broadcasted_iota(i32, (rc, tk), 0)) & (tq - 1))
                s = jnp.where(k0 + lax.broadcasted_iota(i32, (rc, tk), 1) <= t, s, MASK_VAL)
            _softmax_step(rows, s, v, m_ref, acc_ref, l_ref)

    _pipelined_key_tiles(pl.program_id(2), scores, softmax_pv, sa_ref, sb_ref)

    lq = lam_ref[...]
    lam = (jnp.exp(jnp.sum(lq[0:1] * lq[1:2], axis=1, keepdims=True))
           - jnp.exp(jnp.sum(lq[2:3] * lq[3:4], axis=1, keepdims=True)) + lambda_init)
    o = acc_ref[...] / _lane_tile(jnp.maximum(l_ref[...], 1e-30), 2)
    a = o[0:tq] - lam * o[tq:R]
    o_ref[...] = (_rms(a, sg_ref[...]) * (1.0 - lambda_init)).astype(bf16)


def _diff_attn(Pn, Pr, lam_vecs, subln_g, B, S, tq, tk, lambda_init):
    assert tq == tk
    T = B * S
    nqb = S // tq
    return pl.pallas_call(
        functools.partial(_diff_kernel, tq=tq, tk=tk, lambda_init=lambda_init),
        grid=(DIFF_HEADS, B, nqb),
        in_specs=[
            pl.BlockSpec((2, tq, HEAD_DIM), lambda h, b, i: (HD_QD // 2 + h, b * nqb + i, 0)),
            pl.BlockSpec((2, S, HEAD_DIM), lambda h, b, i: (HD_KD // 2 + h, b, 0)),
            pl.BlockSpec((2, S, HEAD_DIM), lambda h, b, i: (HR_VD // 2 + h, b, 0)),
            pl.BlockSpec((8, HEAD_DIM), lambda h, b, i: (0, 0)),
            pl.BlockSpec((1, DIFF_V_DIM), lambda h, b, i: (0, 0)),
        ],
        out_specs=pl.BlockSpec((tq, DIFF_V_DIM), lambda h, b, i: (b * nqb + i, h)),
        out_shape=jax.ShapeDtypeStruct((T, DIFF_HEADS * DIFF_V_DIM), bf16),
        scratch_shapes=[
            pltpu.VMEM((2 * tq, LANES), f32),
            pltpu.VMEM((2 * tq, LANES), f32),
            pltpu.VMEM((2 * tq, DIFF_V_DIM), f32),
            pltpu.VMEM((2 * tq, tk), f32),
            pltpu.VMEM((2 * tq, tk), f32),
        ],
        compiler_params=_cparams(("parallel", "parallel", "arbitrary")),
        name="diff_attn",
    )(Pn, Pn, Pr, lam_vecs, subln_g)


def _outproj_kernel(x_ref, on_ref, od_ref, w_ref, g_ref, wr_ref, br_ref,
                    h_ref, hn_ref, lg_ref):
    half = NSA_HEADS * HEAD_DIM
    tm = x_ref.shape[0]
    rc = min(ROW_CHUNK, tm)
    for c in range(tm // rc):
        rows = slice(c * rc, (c + 1) * rc)
        h = x_ref[rows, :] + jnp.dot(on_ref[rows, :], w_ref[0:half], preferred_element_type=f32) \
            + jnp.dot(od_ref[rows, :], w_ref[half:], preferred_element_type=f32)
        h_ref[rows, :] = h
        hn = _rms(h, g_ref[...])
        hn_ref[rows, :] = hn
        hn_hi = hn.astype(bf16)
        hn_lo = (hn - hn_hi.astype(f32)).astype(bf16)
        r = (jnp.dot(hn_hi, wr_ref[...], preferred_element_type=f32)
             + jnp.dot(hn_lo, wr_ref[...], preferred_element_type=f32))
        lg_ref[rows, :] = r[:, 0:LANES] + r[:, LANES:2 * LANES] + br_ref[...]


def _outproj(x2, o_nsa, o_diff, w_out_b, ffn_g, w_router_p, b_router_p, tm):
    T = x2.shape[0]
    full = lambda shape: pl.BlockSpec(shape, lambda i: (0,) * len(shape))
    return pl.pallas_call(
        _outproj_kernel,
        grid=(T // tm,),
        in_specs=[
            pl.BlockSpec((tm, D_MODEL), lambda i: (i, 0)),
            pl.BlockSpec((tm, NSA_HEADS * HEAD_DIM), lambda i: (i, 0)),
            pl.BlockSpec((tm, DIFF_HEADS * DIFF_V_DIM), lambda i: (i, 0)),
            full((D_MODEL, D_MODEL)),
            full((1, D_MODEL)),
            full((D_MODEL, 2 * LANES)),
            full((1, LANES)),
        ],
        out_specs=[
            pl.BlockSpec((tm, D_MODEL), lambda i: (i, 0)),
            pl.BlockSpec((tm, D_MODEL), lambda i: (i, 0)),
            pl.BlockSpec((tm, LANES), lambda i: (i, 0)),
        ],
        out_shape=[
            jax.ShapeDtypeStruct((T, D_MODEL), f32),
            jax.ShapeDtypeStruct((T, D_MODEL), f32),
            jax.ShapeDtypeStruct((T, LANES), f32),
        ],
        compiler_params=_cparams(("parallel",)),
        name="outproj_router",
    )(x2, o_nsa, o_diff, w_out_b, ffn_g, w_router_p, b_router_p)


def _start_row_gathers(idx_fn, n, src_ref, dst_ref, sem):
    for r in range(n):
        pltpu.make_async_copy(src_ref.at[pl.ds(idx_fn(r), 1)], dst_ref.at[pl.ds(r, 1)], sem).start()


def _wait_row_gathers(n, src_ref, dst_ref, sem):
    for r in range(n):
        pltpu.make_async_copy(src_ref.at[pl.ds(0, 1)], dst_ref.at[pl.ds(r, 1)], sem).wait()


def _expert_row_blocks(first, nblk, x_hbm, o_dst, xbuf, obuf, xsem, osem, tm, compute):
    def x_copy(blk, slot):
        return pltpu.make_async_copy(x_hbm.at[pl.ds(pl.multiple_of(blk * tm, tm), tm)],
                                     xbuf.at[slot], xsem.at[slot])

    def o_copy(blk, slot):
        return pltpu.make_async_copy(obuf.at[slot], o_dst(pl.multiple_of(blk * tm, tm)), osem.at[slot])

    @pl.when(nblk > 0)
    def _():
        x_copy(first, 0).start()

    def body(i, carry):
        slot = i % 2

        @pl.when(i + 1 < nblk)
        def _():
            x_copy(first + i + 1, 1 - slot).start()

        x_copy(first + i, slot).wait()

        @pl.when(i >= 2)
        def _():
            o_copy(first + i - 2, slot).wait()

        obuf[slot] = compute(xbuf[slot])
        o_copy(first + i, slot).start()
        return carry

    lax.fori_loop(0, nblk, body, 0)

    @pl.when(nblk >= 2)
    def _():
        o_copy(first + nblk - 2, nblk % 2).wait()

    @pl.when(nblk >= 1)
    def _():
        o_copy(first + nblk - 1, (nblk - 1) % 2).wait()

    return o_copy


def _zero_unused_blocks(o_copy, obuf, lo, hi):
    obuf[0] = jnp.zeros(obuf.shape[1:], obuf.dtype)

    def body(blk, carry):
        cp = o_copy(blk, 0)
        cp.start()
        cp.wait()
        return carry

    lax.fori_loop(lo, hi, body, 0)


def _gateup_kernel(fb_ref, nbk_ref, nu_ref, st_ref, sb_ref, hn_hbm, wg_ref, wu_ref, bg_ref, bu_ref,
                   hb_hbm, xs_hbm, wgb_ref, wub_ref, xa_ref, xb_ref, obuf, xsbuf, xsem, osem, xssem,
                   *, tm, tn, n_blocks):
    j = pl.program_id(0)
    e = pl.program_id(1)
    wgb_ref[...] = wg_ref[0].astype(bf16)
    wub_ref[...] = wu_ref[0].astype(bf16)
    col0 = pl.multiple_of(j * tn, tn)
    first = fb_ref[e]
    nblk = nbk_ref[e]
    xbufs = (xa_ref, xb_ref)

    def compute(x):
        gl = jnp.dot(x, wgb_ref[...], preferred_element_type=f32) + bg_ref[0]
        up = jnp.dot(x, wub_ref[...], preferred_element_type=f32) + bu_ref[0]
        gl = jnp.minimum(gl, SWIGLU_LIMIT)
        up = jnp.clip(up, -SWIGLU_LIMIT, SWIGLU_LIMIT)
        return ((up + 1.0) * (gl * _sigmoid(SWIGLU_ALPHA * gl))).astype(bf16)

    def hb_dst(r0):
        return hb_hbm.at[pl.ds(r0, tm), pl.ds(col0, tn)]

    @pl.when(j == 0)
    def _gather_pass():
        def issue(g, p):
            base = sb_ref[g]
            _start_row_gathers(lambda r: st_ref[base + r], tm, hn_hbm, xbufs[p], xsem.at[p])

        def wait_x(p):
            _wait_row_gathers(tm, hn_hbm, xbufs[p], xsem.at[p])

        def o_copy(g, slot):
            return pltpu.make_async_copy(obuf.at[slot], hb_dst(pl.multiple_of(g * tm, tm)), osem.at[slot])

        def xs_copy(g, slot):
            return pltpu.make_async_copy(
                xsbuf.at[slot], xs_hbm.at[pl.ds(pl.multiple_of(g * tm, tm), tm)], xssem.at[slot])

        @pl.when(e == 0)
        def _():
            issue(0, 0)

        def body(i, carry):
            g = first + i
            oslot = i % 2

            @pl.when(i >= 2)
            def _():
                o_copy(g - 2, oslot).wait()
                xs_copy(g - 2, oslot).wait()

            for p in range(2):
                @pl.when(g % 2 == p)
                def _():
                    wait_x(p)
                    issue(g + 1, 1 - p)
                    x = xbufs[p][...].astype(bf16)
                    xsbuf[oslot] = x
                    obuf[oslot] = compute(x)

            o_copy(g, oslot).start()
            xs_copy(g, oslot).start()
            return carry

        lax.fori_loop(0, nblk, body, 0)

        @pl.when(nblk >= 2)
        def _():
            o_copy(first + nblk - 2, nblk % 2).wait()
            xs_copy(first + nblk - 2, nblk % 2).wait()

        @pl.when(nblk >= 1)
        def _():
            o_copy(first + nblk - 1, (nblk - 1) % 2).wait()
            xs_copy(first + nblk - 1, (nblk - 1) % 2).wait()

        @pl.when(e == N_EXPERTS - 1)
        def _():
            nu = nu_ref[0]
            for p in range(2):
                @pl.when(nu % 2 == p)
                def _():
                    wait_x(p)
            _zero_unused_blocks(o_copy, obuf, nu, n_blocks)
            _zero_unused_blocks(xs_copy, xsbuf, nu, n_blocks)

    @pl.when(j > 0)
    def _dense_pass():
        o_copy = _expert_row_blocks(first, nblk, xs_hbm, hb_dst, xsbuf, obuf, xssem, osem, tm, compute)

        @pl.when(e == N_EXPERTS - 1)
        def _():
            _zero_unused_blocks(o_copy, obuf, nu_ref[0], n_blocks)


def _gateup(first_blk, n_blk, n_used, sorted_tok, src_base, hn, w_gate_up, b_gate_up3, n_blocks, tm, tn):
    nj = D_FF // tn
    hb, _ = pl.pallas_call(
        functools.partial(_gateup_kernel, tm=tm, tn=tn, n_blocks=n_blocks),
        grid_spec=pltpu.PrefetchScalarGridSpec(
            num_scalar_prefetch=5,
            grid=(nj, N_EXPERTS),
            in_specs=[
                pl.BlockSpec(memory_space=pl.ANY),
                pl.BlockSpec((1, D_MODEL, tn), lambda j, e, *_: (e, 0, j)),
                pl.BlockSpec((1, D_MODEL, tn), lambda j, e, *_: (e, 0, nj + j)),
                pl.BlockSpec((1, 1, tn), lambda j, e, *_: (e, 0, j)),
                pl.BlockSpec((1, 1, tn), lambda j, e, *_: (e, 0, nj + j)),
            ],
            out_specs=[pl.BlockSpec(memory_space=pl.ANY), pl.BlockSpec(memory_space=pl.ANY)],
            scratch_shapes=[
                pltpu.VMEM((D_MODEL, tn), bf16), pltpu.VMEM((D_MODEL, tn), bf16),
                pltpu.VMEM((tm, D_MODEL), hn.dtype), pltpu.VMEM((tm, D_MODEL), hn.dtype),
                pltpu.VMEM((2, tm, tn), bf16), pltpu.VMEM((2, tm, D_MODEL), bf16),
                pltpu.SemaphoreType.DMA((2,)), pltpu.SemaphoreType.DMA((2,)), pltpu.SemaphoreType.DMA((2,)),
            ],
        ),
        out_shape=[jax.ShapeDtypeStruct((n_blocks * tm, D_FF), bf16),
                   jax.ShapeDtypeStruct((n_blocks * tm, D_MODEL), bf16)],
        compiler_params=_cparams(("arbitrary", "arbitrary")),
        name="moe_gate_up",
    )(first_blk, n_blk, n_used, sorted_tok, src_base, hn, w_gate_up, w_gate_up, b_gate_up3, b_gate_up3)
    return hb


def _down_kernel(fb_ref, nbk_ref, nu_ref, hb_hbm, w_ref, b_ref, y_hbm,
                 wb_ref, xbuf, obuf, xsem, osem, *, tm, n_blocks):
    e = pl.program_id(0)
    wb_ref[...] = w_ref[0].astype(bf16)

    def compute(x):
        return jnp.dot(x, wb_ref[...], preferred_element_type=f32) + b_ref[0]

    o_copy = _expert_row_blocks(fb_ref[e], nbk_ref[e], hb_hbm, lambda r0: y_hbm.at[pl.ds(r0, tm)],
                                xbuf, obuf, xsem, osem, tm, compute)

    @pl.when(e == N_EXPERTS - 1)
    def _():
        _zero_unused_blocks(o_copy, obuf, nu_ref[0], n_blocks)


def _down(first_blk, n_blk, n_used, hb, w_down, b_down3, tm):
    n_rows = hb.shape[0]
    return pl.pallas_call(
        functools.partial(_down_kernel, tm=tm, n_blocks=n_rows // tm),
        grid_spec=pltpu.PrefetchScalarGridSpec(
            num_scalar_prefetch=3,
            grid=(N_EXPERTS,),
            in_specs=[
                pl.BlockSpec(memory_space=pl.ANY),
                pl.BlockSpec((1, D_FF, D_MODEL), lambda e, *_: (e, 0, 0)),
                pl.BlockSpec((1, 1, D_MODEL), lambda e, *_: (e, 0, 0)),
            ],
            out_specs=pl.BlockSpec(memory_space=pl.ANY),
            scratch_shapes=[
                pltpu.VMEM((D_FF, D_MODEL), bf16),
                pltpu.VMEM((2, tm, D_FF), bf16), pltpu.VMEM((2, tm, D_MODEL), f32),
                pltpu.SemaphoreType.DMA((2,)), pltpu.SemaphoreType.DMA((2,)),
            ],
        ),
        out_shape=jax.ShapeDtypeStruct((n_rows, D_MODEL), f32),
        compiler_params=_cparams(("arbitrary",)),
        name="moe_down",
    )(first_blk, n_blk, n_used, hb, w_down, b_down3)


def _combine_kernel(pos_ref, h_ref, gt_ref, y_ref, o_ref, buf_ref, sem, *, tc):
    i = pl.program_id(0)
    n = pl.num_programs(0)
    rows = TOP_K * tc

    def issue(step, slot):
        base = step * rows
        _start_row_gathers(lambda r: pos_ref[base + r], rows, y_ref, buf_ref.at[slot], sem.at[slot])

    @pl.when(i == 0)
    def _():
        issue(0, 0)

    @pl.when(i + 1 < n)
    def _():
        issue(i + 1, (i + 1) % 2)

    slot = i % 2
    _wait_row_gathers(rows, y_ref, buf_ref.at[slot], sem.at[slot])
    gt = gt_ref[...]
    out = h_ref[...]
    for k in range(TOP_K):
        out = out + gt[:, k:k + 1] * buf_ref[slot, k * tc:(k + 1) * tc, :]
    o_ref[...] = out


def _combine(pos_blocks, h, gate_pad, y, tc):
    T = h.shape[0]
    return pl.pallas_call(
        functools.partial(_combine_kernel, tc=tc),
        grid_spec=pltpu.PrefetchScalarGridSpec(
            num_scalar_prefetch=1,
            grid=(T // tc,),
            in_specs=[
                pl.BlockSpec((tc, D_MODEL), lambda i, pos: (i, 0)),
                pl.BlockSpec((tc, LANES), lambda i, pos: (i, 0)),
                pl.BlockSpec(memory_space=pl.ANY),
            ],
            out_specs=pl.BlockSpec((tc, D_MODEL), lambda i, pos: (i, 0)),
            scratch_shapes=[pltpu.VMEM((2, TOP_K * tc, D_MODEL), f32), pltpu.SemaphoreType.DMA((2,))],
        ),
        out_shape=jax.ShapeDtypeStruct((T, D_MODEL), f32),
        compiler_params=_cparams(("arbitrary",)),
        name="moe_combine",
    )(pos_blocks, h, gate_pad, y)


def _rearranged_w_in(w_in):
    splits = np.cumsum([0, 1024, 256, 256, 256, 256, 256, 256, 24, 1024, 1024, 1024])
    q, kc, vc, ks, vs, kw, vw, g, qd, kd, vd = [w_in[:, splits[n]:splits[n + 1]] for n in range(11)]
    gcols = NSA_GROUP * N_BRANCH
    zpad = lambda n: jnp.zeros((D_MODEL, n), w_in.dtype)
    w_norm = jnp.concatenate([q, ks, kw, qd, kd], axis=1)
    w_raw = jnp.concatenate([kc, vc, vs, vw, vd,
                             g[:, :gcols], zpad(HEAD_DIM - gcols), g[:, gcols:], zpad(HEAD_DIM - gcols),
                             zpad((N_RAW_HEADS - HR_GATE - 2) * HEAD_DIM)], axis=1)
    return w_norm.astype(bf16), w_raw.astype(bf16)


def _sel_map_t(S, nselp):
    n_chunks = S // CMP_STRIDE
    n_cmp = (S - CMP_LEN) // CMP_STRIDE + 1
    n_sel = S // SEL_BLOCK
    c_start = np.arange(n_chunks) * CMP_STRIDE
    s_start = np.arange(nselp) * SEL_BLOCK
    ov = (c_start[None, :] < s_start[:, None] + SEL_BLOCK) & (c_start[None, :] + CMP_LEN > s_start[:, None])
    ov &= (np.arange(n_chunks)[None, :] < n_cmp) & (np.arange(nselp)[:, None] < n_sel)
    return jnp.asarray(ov, bf16)


def _attention_block(x2, B, S, attn_norm_g, w_in, nsa_q_norm_g, nsa_k_norm_g, pos_k, pos_v,
                     k_w1, k_w2, v_w1, v_w2, diff_q_norm_g, diff_k_norm_g, lq1, lk1, lq2, lk2,
                     subln_g, lambda_init):
    T = B * S
    scale = HEAD_DIM ** -0.5
    tiles = _tiles(S)
    tm = tiles["proj_rows"]
    rope_c, rope_sa, rope_sb = _rope_tables(np.arange(S))
    gains = jnp.concatenate([
        jnp.broadcast_to(nsa_q_norm_g * scale, (NSA_HEADS, HEAD_DIM)),
        jnp.broadcast_to(nsa_k_norm_g[1], (NSA_KV_HEADS, HEAD_DIM)),
        jnp.broadcast_to(nsa_k_norm_g[2], (NSA_KV_HEADS, HEAD_DIM)),
        jnp.broadcast_to(diff_q_norm_g * scale, (2 * DIFF_HEADS, HEAD_DIM)),
        jnp.broadcast_to(diff_k_norm_g, (2 * DIFF_HEADS, HEAD_DIM)),
    ], axis=0).reshape(N_NORM_HEADS, 1, HEAD_DIM)
    w_norm, w_raw = _rearranged_w_in(w_in)
    Pn, Pr, gate = _inproj(x2, attn_norm_g.reshape(1, D_MODEL), w_norm, w_raw, gains,
                           rope_c, rope_sa, rope_sb, S, tm)

    n_chunks = S // CMP_STRIDE
    cmp_pos = np.arange(n_chunks) * CMP_STRIDE + (CMP_LEN - 1)
    cc, csa, csb = _rope_tables(cmp_pos)
    chunk_view = lambda hd: Pr[hd:hd + NSA_KV_HEADS].reshape(NSA_KV_HEADS * B, n_chunks, CMP_STRIDE * HEAD_DIM)
    kcmp = _compress(chunk_view(HR_KC), pos_k, k_w1, k_w2, nsa_k_norm_g[0].reshape(1, HEAD_DIM),
                     cc, csa, csb, True)
    vcmp = _compress(chunk_view(HR_VC), pos_v, v_w1, v_w2, nsa_k_norm_g[0].reshape(1, HEAD_DIM),
                     cc, csa, csb, False)

    nselp = -(-(S // SEL_BLOCK) // LANES) * LANES
    tq = tiles["attn_q"]
    o_c, bias = _nsa_cmp(Pn, kcmp, vcmp, gate, _sel_map_t(S, nselp), B, S, tq)
    o_nsa = _nsa_sel(Pn, Pr, bias, gate, o_c, B, S, tq, tiles["sel_keys"], tq)

    lam_vecs = jnp.concatenate([lq1[None], lk1[None], lq2[None], lk2[None],
                                jnp.zeros((4, HEAD_DIM), f32)], axis=0)
    o_diff = _diff_attn(Pn, Pr, lam_vecs, subln_g.reshape(1, DIFF_V_DIM), B, S,
                        tiles["diff_q"], tiles["diff_q"], lambda_init)
    return o_nsa, o_diff


def _moe_block(h, hn, logits, w_gate_up, b_gate_up, w_down, b_down, tiles):
    T = h.shape[0]
    tm = tiles["moe_rows"]
    TK = T * TOP_K
    top_v, top_e = lax.top_k(logits[:, :N_EXPERTS], TOP_K)
    gate = jax.nn.softmax(top_v, axis=-1)
    flat_e = top_e.reshape(TK).astype(i32)
    se, order = lax.sort((flat_e, jnp.arange(TK, dtype=i32)), num_keys=1, is_stable=True)
    st = order // TOP_K
    experts = jnp.arange(N_EXPERTS, dtype=i32)
    counts = jnp.sum((flat_e[:, None] == experts[None, :]).astype(i32), axis=0)
    padded = (counts + tm - 1) // tm * tm
    pad_end = jnp.cumsum(padded)
    pad_start = pad_end - padded
    cnt_start = jnp.cumsum(counts) - counts
    dest = pad_start[se] + (jnp.arange(TK, dtype=i32) - cnt_start[se])
    _, pos = lax.sort((order, dest), num_keys=1)
    n_blocks = -(-TK // tm) + N_EXPERTS
    blk_start = jnp.arange(n_blocks, dtype=i32) * tm
    blk_e = jnp.minimum(jnp.sum((pad_end[None, :] <= blk_start[:, None]).astype(i32), axis=1), N_EXPERTS - 1)
    src_base = cnt_start[blk_e] + (blk_start - pad_start[blk_e])
    n_used = (pad_end[-1] // tm).astype(i32).reshape(1)

    src_base = jnp.minimum(jnp.concatenate([src_base, src_base[-1:]]), TK).astype(i32)
    st = jnp.concatenate([st, jnp.zeros((tm,), i32)])
    first_blk = (pad_start // tm).astype(i32)
    n_blk = (padded // tm).astype(i32)
    hb = _gateup(first_blk, n_blk, n_used, st, src_base, hn, w_gate_up,
                 b_gate_up.reshape(N_EXPERTS, 1, 2 * D_FF), n_blocks, tm, tiles["moe_cols"])
    y = _down(first_blk, n_blk, n_used, hb, w_down, b_down.reshape(N_EXPERTS, 1, D_MODEL), tm)

    tc = tiles["combine_rows"]
    pos_blocks = pos.reshape(T // tc, tc, TOP_K).transpose(0, 2, 1).reshape(TK)
    gate_pad = jnp.pad(gate, ((0, 0), (0, LANES - TOP_K)))
    return _combine(pos_blocks, h, gate_pad, y, tc)


def kernel(x, attn_norm_g, w_in, nsa_q_norm_g, nsa_k_norm_g, nsa_cmp_pos_k, nsa_cmp_pos_v, nsa_cmp_k_w1, nsa_cmp_k_w2, nsa_cmp_v_w1, nsa_cmp_v_w2, diff_q_norm_g, diff_k_norm_g, diff_lambda_q1, diff_lambda_k1, diff_lambda_q2, diff_lambda_k2, diff_subln_g, w_out, ffn_norm_g, w_router, b_router, w_gate_up, b_gate_up, w_down, b_down):
    B, S, _ = x.shape
    T = B * S
    depth = attn_norm_g.shape[0]
    h = x.reshape(T, D_MODEL)
    for l in range(depth):
        lambda_init = 0.8 - 0.6 * math.exp(-0.3 * l)
        o_nsa, o_diff = _attention_block(
            h, B, S, attn_norm_g[l], w_in[l], nsa_q_norm_g[l], nsa_k_norm_g[l], nsa_cmp_pos_k[l],
            nsa_cmp_pos_v[l], nsa_cmp_k_w1[l], nsa_cmp_k_w2[l], nsa_cmp_v_w1[l], nsa_cmp_v_w2[l],
            diff_q_norm_g[l], diff_k_norm_g[l], diff_lambda_q1[l], diff_lambda_k1[l],
            diff_lambda_q2[l], diff_lambda_k2[l], diff_subln_g[l], lambda_init)
        w_router_f = jnp.pad(w_router[l], ((0, 0), (0, LANES - N_EXPERTS)))
        w_router_hi = w_router_f.astype(bf16)
        w_router_lo = (w_router_f - w_router_hi.astype(f32)).astype(bf16)
        w_router_p = jnp.concatenate([w_router_hi, w_router_lo], axis=1)
        b_router_p = jnp.pad(b_router[l], (0, LANES - N_EXPERTS)).reshape(1, LANES)
        h_mid, hn, logits = _outproj(h, o_nsa, o_diff, w_out[l].astype(bf16),
                                     ffn_norm_g[l].reshape(1, D_MODEL), w_router_p, b_router_p,
                                     _tiles(S)["out_rows"])
        h = _moe_block(h_mid, hn, logits, w_gate_up[l], b_gate_up[l], w_down[l], b_down[l], _tiles(S))
    return h.reshape(B, S, D_MODEL)
```

```python
import functools
import math

import numpy as np
import jax
import jax.numpy as jnp
from jax import lax
from jax.experimental import pallas as pl
from jax.experimental.pallas import tpu as pltpu

f32 = jnp.float32
bf16 = jnp.bfloat16
i32 = jnp.int32

D_MODEL = 2048
HEAD_DIM = 128
ROT_DIM = HEAD_DIM // 4
ROPE_THETA = 500000.0
NORM_EPS = 1e-6
NEG_INF = -1e30

NSA_HEADS = 8
NSA_KV_HEADS = 2
NSA_GROUP = NSA_HEADS // NSA_KV_HEADS
N_BRANCH = 3
CMP_LEN = 32
CMP_STRIDE = 16
CMP_HIDDEN = 2 * HEAD_DIM
SEL_BLOCK = 64
SEL_SHIFT = 6
N_SELECTED = 16
WINDOW = 512
FORCED_SCORE = 1e9

DIFF_HEADS = 4
DIFF_V_DIM = 2 * HEAD_DIM

N_EXPERTS = 32
TOP_K = 4
D_FF = D_MODEL
SWIGLU_ALPHA = 1.702
SWIGLU_LIMIT = 7.0

LANES = 128

HD_Q = 0
HD_KS = 8
HD_KW = 10
HD_QD = 12
HD_KD = 20
N_NORM_HEADS = 28
HR_KC = 0
HR_VC = 2
HR_VS = 4
HR_VW = 6
HR_VD = 8
HR_GATE = 16
N_RAW_HEADS = 20
HP = 4

MASK_VAL = -1e30
M_INIT = -5e29

V7X_VMEM_BYTES = 64 * 1024 * 1024
VMEM_LIMIT = V7X_VMEM_BYTES - 8 * 1024 * 1024


def _tiles(S):
    return dict(
        proj_rows=min(1024, S),
        attn_q=min(256, S),
        sel_keys=min(1024, S),
        diff_q=min(512, S),
        out_rows=512,
        moe_rows=256,
        moe_cols=1024,
        combine_rows=128,
    )


def _cparams(sem):
    return pltpu.CompilerParams(dimension_semantics=sem, vmem_limit_bytes=VMEM_LIMIT)


def _rope_tables(pos):
    inv = np.power(ROPE_THETA, -np.arange(0, ROT_DIM, 2, dtype=np.float64) / ROT_DIM)
    ang = pos.astype(np.float64)[:, None] * inv[None, :]
    cos, sin = np.cos(ang), np.sin(ang)
    n = pos.shape[0]
    half = ROT_DIM // 2
    c = np.concatenate([cos, cos, np.ones((n, HEAD_DIM - ROT_DIM))], axis=1)
    sa = np.concatenate([-sin, np.zeros((n, HEAD_DIM - half))], axis=1)
    sb = np.concatenate([np.zeros((n, half)), sin, np.zeros((n, HEAD_DIM - ROT_DIM))], axis=1)
    return (jnp.asarray(c, f32), jnp.asarray(sa, f32), jnp.asarray(sb, f32))


def _rope(y, c, sa, sb):
    half = ROT_DIM // 2
    return (y * c + pltpu.roll(y, HEAD_DIM - half, 1) * sa + pltpu.roll(y, half, 1) * sb)


def _rms(y, gain):
    ms = jnp.mean(y * y, axis=-1, keepdims=True)
    return y * lax.rsqrt(ms + NORM_EPS) * gain


def _inproj_norm_kernel(x_ref, g_ref, w_ref, gain_ref, c_ref, sa_ref, sb_ref, o_ref, xn_ref):
    @pl.when(pl.program_id(1) == 0)
    def _():
        xn_ref[...] = _rms(x_ref[...], g_ref[...]).astype(bf16)

    tm = x_ref.shape[0]
    rc = min(ROW_CHUNK, tm)
    for r in range(tm // rc):
        rows = slice(r * rc, (r + 1) * rc)
        acc = jnp.dot(xn_ref[rows, :], w_ref[...], preferred_element_type=f32)
        c, sa, sb = c_ref[rows, :], sa_ref[rows, :], sb_ref[rows, :]
        for u in range(HP):
            y = _rms(acc[:, u * HEAD_DIM:(u + 1) * HEAD_DIM], gain_ref[u])
            o_ref[u, rows, :] = _rope(y, c, sa, sb).astype(bf16)


def _inproj_raw_kernel(x_ref, g_ref, w_ref, o_ref, gate_ref, xn_ref):
    j = pl.program_id(1)

    @pl.when(j == 0)
    def _():
        xn_ref[...] = _rms(x_ref[...], g_ref[...]).astype(bf16)

    tm = x_ref.shape[0]
    rc = min(ROW_CHUNK, tm)
    for r in range(tm // rc):
        rows = slice(r * rc, (r + 1) * rc)
        acc = jnp.dot(xn_ref[rows, :], w_ref[...], preferred_element_type=f32)
        for u in range(HP):
            o_ref[u, rows, :] = acc[:, u * HEAD_DIM:(u + 1) * HEAD_DIM].astype(bf16)

    @pl.when(j == HR_GATE // HP)
    def _():
        gate_ref[0] = jnp.dot(xn_ref[...], w_ref[:, 0:HEAD_DIM], preferred_element_type=f32)
        gate_ref[1] = jnp.dot(xn_ref[...], w_ref[:, HEAD_DIM:2 * HEAD_DIM], preferred_element_type=f32)


def _inproj(x2, attn_g, w_norm, w_raw, gains, rope_c, rope_sa, rope_sb, S, tm):
    T = x2.shape[0]
    nsb = S // tm
    x_specs = [
        pl.BlockSpec((tm, D_MODEL), lambda i, j: (i, 0)),
        pl.BlockSpec((1, D_MODEL), lambda i, j: (0, 0)),
        pl.BlockSpec((D_MODEL, HP * HEAD_DIM), lambda i, j: (0, j)),
    ]
    head_spec = pl.BlockSpec((HP, tm, HEAD_DIM), lambda i, j: (j, i, 0))
    pos_spec = pl.BlockSpec((tm, HEAD_DIM), lambda i, j: (i % nsb, 0))
    Pn = pl.pallas_call(
        _inproj_norm_kernel,
        grid=(T // tm, N_NORM_HEADS // HP),
        in_specs=x_specs + [pl.BlockSpec((HP, 1, HEAD_DIM), lambda i, j: (j, 0, 0)),
                            pos_spec, pos_spec, pos_spec],
        out_specs=head_spec,
        out_shape=jax.ShapeDtypeStruct((N_NORM_HEADS, T, HEAD_DIM), bf16),
        scratch_shapes=[pltpu.VMEM((tm, D_MODEL), bf16)],
        compiler_params=_cparams(("parallel", "arbitrary")),
        name="inproj_norm",
    )(x2, attn_g, w_norm, gains, rope_c, rope_sa, rope_sb)
    Pr, gate = pl.pallas_call(
        _inproj_raw_kernel,
        grid=(T // tm, N_RAW_HEADS // HP),
        in_specs=x_specs,
        out_specs=[head_spec, pl.BlockSpec((2, tm, HEAD_DIM), lambda i, j: (0, i, 0))],
        out_shape=[
            jax.ShapeDtypeStruct((N_RAW_HEADS, T, HEAD_DIM), bf16),
            jax.ShapeDtypeStruct((2, T, HEAD_DIM), f32),
        ],
        scratch_shapes=[pltpu.VMEM((tm, D_MODEL), bf16)],
        compiler_params=_cparams(("parallel", "arbitrary")),
        name="inproj_raw",
    )(x2, attn_g, w_raw)
    return Pn, Pr, gate


def _gelu_tanh(x):
    return 0.5 * x * (1.0 + jnp.tanh(math.sqrt(2.0 / math.pi) * (x + 0.044715 * (x * x * x))))


def _compress_kernel(c_ref, pe_ref, w1_ref, w2_ref, gain_ref, rc_ref, rsa_ref, rsb_ref,
                     o_ref, *, do_norm):
    half = CMP_STRIDE * HEAD_DIM
    c = c_ref[0]
    nc = c.shape[0]
    w1 = w1_ref[...].astype(bf16)
    a = jnp.dot(c, w1[:half], preferred_element_type=f32)
    b = jnp.dot(c, w1[half:], preferred_element_type=f32)
    peb = jnp.dot(pe_ref[...], w1, preferred_element_type=f32)[0:1]
    hid = a + pltpu.roll(b, nc - 1, 0) + peb
    act = _gelu_tanh(hid)
    out = jnp.dot(act.astype(bf16), w2_ref[...].astype(bf16), preferred_element_type=f32)
    if do_norm:
        out = _rope(_rms(out, gain_ref[...]), rc_ref[...], rsa_ref[...], rsb_ref[...])
    o_ref[0] = out.astype(bf16)


def _compress(chunks, pe, w1, w2, gain, rc, rsa, rsb, do_norm):
    ng, nc, width = chunks.shape
    pe8 = jnp.broadcast_to(pe.reshape(1, CMP_LEN * HEAD_DIM), (8, CMP_LEN * HEAD_DIM)).astype(bf16)
    full = lambda shape: pl.BlockSpec(shape, lambda g: (0,) * len(shape))
    return pl.pallas_call(
        functools.partial(_compress_kernel, do_norm=do_norm),
        grid=(ng,),
        in_specs=[
            pl.BlockSpec((1, nc, width), lambda g: (g, 0, 0)),
            full((8, CMP_LEN * HEAD_DIM)),
            full((CMP_LEN * HEAD_DIM, CMP_HIDDEN)),
            full((CMP_HIDDEN, HEAD_DIM)),
            full((1, HEAD_DIM)),
            full((nc, HEAD_DIM)), full((nc, HEAD_DIM)), full((nc, HEAD_DIM)),
        ],
        out_specs=pl.BlockSpec((1, nc, HEAD_DIM), lambda g: (g, 0, 0)),
        out_shape=jax.ShapeDtypeStruct((ng, nc, HEAD_DIM), bf16),
        compiler_params=_cparams(("parallel",)),
        name="compress_k" if do_norm else "compress_v",
    )(chunks, pe8, w1, w2, gain, rc, rsa, rsb)


def _sigmoid(x):
    return 1.0 / (1.0 + jnp.exp(-x))


def _nsa_cmp_kernel(q_ref, kc_ref, vc_ref, g_ref, smt_ref, oc_ref, bias_ref, *, tq):
    q0 = pl.program_id(2) * tq
    kc = kc_ref[0]
    vc = vc_ref[0]
    nc = kc.shape[0]
    R = NSA_GROUP * tq
    sig = _sigmoid(g_ref[0])
    q = q_ref[...].reshape(R, HEAD_DIM)
    s = lax.dot_general(q, kc, (((1,), (1,)), ((), ())), preferred_element_type=f32)
    t = q0 + (lax.broadcasted_iota(i32, (R, nc), 0) & (tq - 1))
    cend = lax.broadcasted_iota(i32, (R, nc), 1) * CMP_STRIDE + (CMP_LEN - 1)
    mask = cend <= t
    s = jnp.where(mask, s, NEG_INF)
    m = jnp.max(s, axis=-1, keepdims=True)
    p = jnp.where(mask, jnp.exp(s - m), 0.0)
    p = p / jnp.maximum(jnp.sum(p, axis=-1, keepdims=True), 1e-30)
    o = jnp.dot(p.astype(bf16), vc, preferred_element_type=f32)
    for gi in range(NSA_GROUP):
        col = gi * N_BRANCH
        oc_ref[:, gi * HEAD_DIM:(gi + 1) * HEAD_DIM] = o[gi * tq:(gi + 1) * tq] * sig[:, col:col + 1]

    psum = p[0:tq] + p[tq:2 * tq] + p[2 * tq:3 * tq] + p[3 * tq:4 * tq]
    ps_hi = psum.astype(bf16)
    ps_lo = (psum - ps_hi.astype(f32)).astype(bf16)
    nt = (((1,), (1,)), ((), ()))
    imp = (lax.dot_general(smt_ref[...], ps_hi, nt, preferred_element_type=f32)
           + lax.dot_general(smt_ref[...], ps_lo, nt, preferred_element_type=f32))
    nselp = imp.shape[0]
    jf = lax.broadcasted_iota(i32, (nselp, tq), 0).astype(f32)
    cur = ((q0 + lax.broadcasted_iota(i32, (nselp, tq), 1)) >> SEL_SHIFT).astype(f32)
    forced = (jf == 0.0) | (jf == cur) | (jf == cur - 1.0)
    imp = jnp.where(forced, FORCED_SCORE, imp)
    imp = jnp.where(jf <= cur, imp, NEG_INF)
    PICKED = -3e38
    work = imp
    for _ in range(N_SELECTED):
        mx = jnp.max(work, axis=0, keepdims=True)
        first = jnp.min(jnp.where(work == mx, jf, float(nselp)), axis=0, keepdims=True)
        work = jnp.where(jf == first, PICKED, work)
    bias_t = jnp.where((work < -2e38) & (imp > -1.0), 0.0, MASK_VAL)
    bias_ref[0] = bias_t.T.astype(bf16)


def _nsa_cmp(P, kcmp, vcmp, gate, smt, B, S, tq):
    T = B * S
    nqb = S // tq
    nc = kcmp.shape[1]
    nselp = smt.shape[0]
    return pl.pallas_call(
        functools.partial(_nsa_cmp_kernel, tq=tq),
        grid=(NSA_KV_HEADS, B, nqb),
        in_specs=[
            pl.BlockSpec((NSA_GROUP, tq, HEAD_DIM), lambda h, b, i: (h, b * nqb + i, 0)),
            pl.BlockSpec((1, nc, HEAD_DIM), lambda h, b, i: (h * B + b, 0, 0)),
            pl.BlockSpec((1, nc, HEAD_DIM), lambda h, b, i: (h * B + b, 0, 0)),
            pl.BlockSpec((1, tq, HEAD_DIM), lambda h, b, i: (h, b * nqb + i, 0)),
            pl.BlockSpec((nselp, nc), lambda h, b, i: (0, 0)),
        ],
        out_specs=[
            pl.BlockSpec((tq, NSA_GROUP * HEAD_DIM), lambda h, b, i: (b * nqb + i, h)),
            pl.BlockSpec((1, tq, nselp), lambda h, b, i: (h, b * nqb + i, 0)),
        ],
        out_shape=[
            jax.ShapeDtypeStruct((T, NSA_HEADS * HEAD_DIM), f32),
            jax.ShapeDtypeStruct((NSA_KV_HEADS, T, nselp), bf16),
        ],
        compiler_params=_cparams(("parallel", "parallel", "parallel")),
        name="nsa_cmp",
    )(P, kcmp, vcmp, gate, smt)


def _lane_tile(x, n):
    return x if n == 1 else jnp.concatenate([x] * n, axis=1)


ROW_CHUNK = 256


def _softmax_step(rows, s, v, m_ref, acc_ref, l_ref=None):
    tk = s.shape[1]
    m_prev = m_ref[rows, :]
    m_new = jnp.maximum(m_prev, jnp.max(s, axis=1, keepdims=True))
    alpha = jnp.exp(m_prev - m_new)
    p = jnp.exp(s - _lane_tile(m_new, tk // LANES))
    if l_ref is not None:
        l_ref[rows, :] = alpha * l_ref[rows, :] + jnp.sum(p, axis=1, keepdims=True)
    acc_ref[rows, :] = _lane_tile(alpha, 2) * acc_ref[rows, :] + jnp.dot(
        p.astype(bf16), v, preferred_element_type=f32)
    m_ref[rows, :] = m_new


def _pipelined_key_tiles(n_full, scores, softmax_pv, sa_ref, sb_ref):
    def pair(j, carry):
        scores(2 * j + 1, sb_ref)
        softmax_pv(2 * j, sa_ref, False)
        scores(2 * j + 2, sa_ref)
        softmax_pv(2 * j + 1, sb_ref, False)
        return carry

    scores(0, sa_ref)
    lax.fori_loop(0, n_full // 2, pair, 0)

    @pl.when(n_full % 2 == 0)
    def _():
        softmax_pv(n_full, sa_ref, True)

    @pl.when(n_full % 2 == 1)
    def _():
        scores(n_full, sb_ref)
        softmax_pv(n_full - 1, sa_ref, False)
        softmax_pv(n_full, sb_ref, True)


def _nsa_sel_kernel(q_ref, bias_ref, ks_ref, vs_ref, kw_ref, vw_ref, g_ref, oc_ref,
                    o_ref, qaug_ref, m_ref, acc_ref, sa_ref, sb_ref, *, tq, tk, tkw):
    q0 = pl.program_id(2) * tq
    R = NSA_GROUP * tq
    nselp = bias_ref.shape[2]
    q = q_ref[...].reshape(R, HEAD_DIM)
    qaug_ref[:, 0:HEAD_DIM] = q
    bias = bias_ref[0]
    for gi in range(NSA_GROUP):
        qaug_ref[gi * tq:(gi + 1) * tq, HEAD_DIM:HEAD_DIM + nselp] = bias
    sig = _sigmoid(g_ref[0])
    rc = min(ROW_CHUNK, tq)

    def row_t(c, width):
        return q0 + ((c * rc + lax.broadcasted_iota(i32, (rc, width), 0)) & (tq - 1))

    def init():
        m_ref[...] = jnp.full(m_ref.shape, M_INIT, f32)
        acc_ref[...] = jnp.zeros(acc_ref.shape, f32)

    def finish():
        return acc_ref[:, 0:HEAD_DIM] / jnp.maximum(acc_ref[:, HEAD_DIM:2 * HEAD_DIM], 1e-30)

    init()
    ones_k = jnp.ones((tk, HEAD_DIM), bf16)

    def sel_scores(kt, s_ref):
        k0 = pl.multiple_of(kt * tk, tk)
        k = ks_ref[0, pl.ds(k0, tk), :]
        blk = (k0 + lax.broadcasted_iota(i32, (tk, nselp), 0)) >> SEL_SHIFT
        onehot = jnp.where(blk == lax.broadcasted_iota(i32, (tk, nselp), 1), 1.0, 0.0).astype(bf16)
        kaug = jnp.concatenate([k, onehot], axis=1)
        for c in range(R // rc):
            rows = slice(c * rc, (c + 1) * rc)
            s_ref[rows, :] = lax.dot_general(qaug_ref[rows, :], kaug, (((1,), (1,)), ((), ())),
                                             preferred_element_type=f32)

    def sel_softmax_pv(kt, s_ref, diagonal):
        k0 = pl.multiple_of(kt * tk, tk)
        vaug = jnp.concatenate([vs_ref[0, pl.ds(k0, tk), :], ones_k], axis=1)
        for c in range(R // rc):
            rows = slice(c * rc, (c + 1) * rc)
            s = s_ref[rows, :]
            if diagonal:
                s = jnp.where(k0 + lax.broadcasted_iota(i32, (rc, tk), 1) <= row_t(c, tk), s, MASK_VAL)
            _softmax_step(rows, s, vaug, m_ref, acc_ref)

    _pipelined_key_tiles(q0 // tk, sel_scores, sel_softmax_pv, sa_ref, sb_ref)
    o_s = finish()

    init()
    ones_w = jnp.ones((tkw, HEAD_DIM), bf16)
    qi = pl.program_id(2)

    def win_step(kt, mode):
        k0 = pl.multiple_of(kt * tkw, tkw)
        k = kw_ref[0, pl.ds(k0, tkw), :]
        vaug = jnp.concatenate([vw_ref[0, pl.ds(k0, tkw), :], ones_w], axis=1)
        for c in range(R // rc):
            rows = slice(c * rc, (c + 1) * rc)
            s = lax.dot_general(qaug_ref[rows, 0:HEAD_DIM], k, (((1,), (1,)), ((), ())),
                                preferred_element_type=f32)
            if mode != "full":
                t = row_t(c, tkw)
                kp = k0 + lax.broadcasted_iota(i32, (rc, tkw), 1)
                s = jnp.where(kp <= t if mode == "diagonal" else t - kp < WINDOW, s, MASK_VAL)
            _softmax_step(rows, s, vaug, m_ref, acc_ref)

    @pl.when(qi >= 2)
    def _():
        win_step(qi - 2, "edge")

    @pl.when(qi >= 1)
    def _():
        win_step(qi - 1, "full")

    win_step(qi, "diagonal")
    o_w = finish()

    for gi in range(NSA_GROUP):
        rows = slice(gi * tq, (gi + 1) * tq)
        cols = slice(gi * HEAD_DIM, (gi + 1) * HEAD_DIM)
        c = gi * N_BRANCH
        o_ref[:, cols] = (oc_ref[:, cols] + o_s[rows] * sig[:, c + 1:c + 2]
                          + o_w[rows] * sig[:, c + 2:c + 3]).astype(bf16)


def _nsa_sel(Pn, Pr, bias, gate, o_c, B, S, tq, tk, tkw):
    assert tk % tq == 0 and tkw == tq and WINDOW == 2 * tq
    T = B * S
    nqb = S // tq
    nselp = bias.shape[2]
    R = NSA_GROUP * tq
    kv_spec = lambda base: pl.BlockSpec((1, S, HEAD_DIM), lambda h, b, i: (base + h, b, 0))
    return pl.pallas_call(
        functools.partial(_nsa_sel_kernel, tq=tq, tk=tk, tkw=tkw),
        grid=(NSA_KV_HEADS, B, nqb),
        in_specs=[
            pl.BlockSpec((NSA_GROUP, tq, HEAD_DIM), lambda h, b, i: (h, b * nqb + i, 0)),
            pl.BlockSpec((1, tq, nselp), lambda h, b, i: (h, b * nqb + i, 0)),
            kv_spec(HD_KS), kv_spec(HR_VS), kv_spec(HD_KW), kv_spec(HR_VW),
            pl.BlockSpec((1, tq, HEAD_DIM), lambda h, b, i: (h, b * nqb + i, 0)),
            pl.BlockSpec((tq, NSA_GROUP * HEAD_DIM), lambda h, b, i: (b * nqb + i, h)),
        ],
        out_specs=pl.BlockSpec((tq, NSA_GROUP * HEAD_DIM), lambda h, b, i: (b * nqb + i, h)),
        out_shape=jax.ShapeDtypeStruct((T, NSA_HEADS * HEAD_DIM), bf16),
        scratch_shapes=[
            pltpu.VMEM((R, HEAD_DIM + nselp), bf16),
            pltpu.VMEM((R, LANES), f32),
            pltpu.VMEM((R, 2 * HEAD_DIM), f32),
            pltpu.VMEM((R, tk), f32),
            pltpu.VMEM((R, tk), f32),
        ],
        compiler_params=_cparams(("parallel", "parallel", "arbitrary")),
        name="nsa_sel_win",
    )(Pn, bias, Pn, Pr, Pn, Pr, gate, o_c)


def _diff_kernel(q_ref, k_ref, v_ref, lam_ref, sg_ref, o_ref, m_ref, l_ref, acc_ref, sa_ref, sb_ref,
                 *, tq, tk, lambda_init):
    q0 = pl.program_id(2) * tq
    R = 2 * tq
    rc = tq
    m_ref[...] = jnp.full(m_ref.shape, M_INIT, f32)
    l_ref[...] = jnp.zeros(l_ref.shape, f32)
    acc_ref[...] = jnp.zeros(acc_ref.shape, f32)

    def scores(kt, s_ref):
        k0 = pl.multiple_of(kt * tk, tk)
        for comp in range(2):
            s_ref[comp * tq:(comp + 1) * tq, :] = lax.dot_general(
                q_ref[comp], k_ref[comp, pl.ds(k0, tk), :], (((1,), (1,)), ((), ())),
                preferred_element_type=f32)

    def softmax_pv(kt, s_ref, diagonal):
        k0 = pl.multiple_of(kt * tk, tk)
        v = jnp.concatenate([v_ref[0, pl.ds(k0, tk), :], v_ref[1, pl.ds(k0, tk), :]], axis=1)
        for c in range(R // rc):
            rows = slice(c * rc, (c + 1) * rc)
            s = s_ref[rows, :]
            if diagonal:
                t = q0 + ((c * rc + lax.broadcasted_iota(i32, (rc, tk), 0)) & (tq - 1))
                s = jnp.where(k0 + lax.broadcasted_iota(i32, (rc, tk), 1) <= t, s, MASK_VAL)
            _softmax_step(rows, s, v, m_ref, acc_ref, l_ref)

    _pipelined_key_tiles(pl.program_id(2), scores, softmax_pv, sa_ref, sb_ref)

    lq = lam_ref[...]
    lam = (jnp.exp(jnp.sum(lq[0:1] * lq[1:2], axis=1, keepdims=True))
           - jnp.exp(jnp.sum(lq[2:3] * lq[3:4], axis=1, keepdims=True)) + lambda_init)
    o = acc_ref[...] / _lane_tile(jnp.maximum(l_ref[...], 1e-30), 2)
    a = o[0:tq] - lam * o[tq:R]
    o_ref[...] = (_rms(a, sg_ref[...]) * (1.0 - lambda_init)).astype(bf16)


def _diff_attn(Pn, Pr, lam_vecs, subln_g, B, S, tq, tk, lambda_init):
    assert tq == tk
    T = B * S
    nqb = S // tq
    return pl.pallas_call(
        functools.partial(_diff_kernel, tq=tq, tk=tk, lambda_init=lambda_init),
        grid=(DIFF_HEADS, B, nqb),
        in_specs=[
            pl.BlockSpec((2, tq, HEAD_DIM), lambda h, b, i: (HD_QD // 2 + h, b * nqb + i, 0)),
            pl.BlockSpec((2, S, HEAD_DIM), lambda h, b, i: (HD_KD // 2 + h, b, 0)),
            pl.BlockSpec((2, S, HEAD_DIM), lambda h, b, i: (HR_VD // 2 + h, b, 0)),
            pl.BlockSpec((8, HEAD_DIM), lambda h, b, i: (0, 0)),
            pl.BlockSpec((1, DIFF_V_DIM), lambda h, b, i: (0, 0)),
        ],
        out_specs=pl.BlockSpec((tq, DIFF_V_DIM), lambda h, b, i: (b * nqb + i, h)),
        out_shape=jax.ShapeDtypeStruct((T, DIFF_HEADS * DIFF_V_DIM), bf16),
        scratch_shapes=[
            pltpu.VMEM((2 * tq, LANES), f32),
            pltpu.VMEM((2 * tq, LANES), f32),
            pltpu.VMEM((2 * tq, DIFF_V_DIM), f32),
            pltpu.VMEM((2 * tq, tk), f32),
            pltpu.VMEM((2 * tq, tk), f32),
        ],
        compiler_params=_cparams(("parallel", "parallel", "arbitrary")),
        name="diff_attn",
    )(Pn, Pn, Pr, lam_vecs, subln_g)


def _outproj_kernel(x_ref, on_ref, od_ref, w_ref, g_ref, wr_ref, br_ref,
                    h_ref, hn_ref, lg_ref):
    half = NSA_HEADS * HEAD_DIM
    tm = x_ref.shape[0]
    rc = min(ROW_CHUNK, tm)
    for c in range(tm // rc):
        rows = slice(c * rc, (c + 1) * rc)
        h = x_ref[rows, :] + jnp.dot(on_ref[rows, :], w_ref[0:half], preferred_element_type=f32) \
            + jnp.dot(od_ref[rows, :], w_ref[half:], preferred_element_type=f32)
        h_ref[rows, :] = h
        hn = _rms(h, g_ref[...])
        hn_ref[rows, :] = hn
        hn_hi = hn.astype(bf16)
        hn_lo = (hn - hn_hi.astype(f32)).astype(bf16)
        r = (jnp.dot(hn_hi, wr_ref[...], preferred_element_type=f32)
             + jnp.dot(hn_lo, wr_ref[...], preferred_element_type=f32))
        lg_ref[rows, :] = r[:, 0:LANES] + r[:, LANES:2 * LANES] + br_ref[...]


def _outproj(x2, o_nsa, o_diff, w_out_b, ffn_g, w_router_p, b_router_p, tm):
    T = x2.shape[0]
    full = lambda shape: pl.BlockSpec(shape, lambda i: (0,) * len(shape))
    return pl.pallas_call(
        _outproj_kernel,
        grid=(T // tm,),
        in_specs=[
            pl.BlockSpec((tm, D_MODEL), lambda i: (i, 0)),
            pl.BlockSpec((tm, NSA_HEADS * HEAD_DIM), lambda i: (i, 0)),
            pl.BlockSpec((tm, DIFF_HEADS * DIFF_V_DIM), lambda i: (i, 0)),
            full((D_MODEL, D_MODEL)),
            full((1, D_MODEL)),
            full((D_MODEL, 2 * LANES)),
            full((1, LANES)),
        ],
        out_specs=[
            pl.BlockSpec((tm, D_MODEL), lambda i: (i, 0)),
            pl.BlockSpec((tm, D_MODEL), lambda i: (i, 0)),
            pl.BlockSpec((tm, LANES), lambda i: (i, 0)),
        ],
        out_shape=[
            jax.ShapeDtypeStruct((T, D_MODEL), f32),
            jax.ShapeDtypeStruct((T, D_MODEL), f32),
            jax.ShapeDtypeStruct((T, LANES), f32),
        ],
        compiler_params=_cparams(("parallel",)),
        name="outproj_router",
    )(x2, o_nsa, o_diff, w_out_b, ffn_g, w_router_p, b_router_p)


def _start_row_gathers(idx_fn, n, src_ref, dst_ref, sem):
    for r in range(n):
        pltpu.make_async_copy(src_ref.at[pl.ds(idx_fn(r), 1)], dst_ref.at[pl.ds(r, 1)], sem).start()


def _wait_row_gathers(n, src_ref, dst_ref, sem):
    for r in range(n):
        pltpu.make_async_copy(src_ref.at[pl.ds(0, 1)], dst_ref.at[pl.ds(r, 1)], sem).wait()


def _expert_row_blocks(first, nblk, x_hbm, o_dst, xbuf, obuf, xsem, osem, tm, compute):
    def x_copy(blk, slot):
        return pltpu.make_async_copy(x_hbm.at[pl.ds(pl.multiple_of(blk * tm, tm), tm)],
                                     xbuf.at[slot], xsem.at[slot])

    def o_copy(blk, slot):
        return pltpu.make_async_copy(obuf.at[slot], o_dst(pl.multiple_of(blk * tm, tm)), osem.at[slot])

    @pl.when(nblk > 0)
    def _():
        x_copy(first, 0).start()

    def body(i, carry):
        slot = i % 2

        @pl.when(i + 1 < nblk)
        def _():
            x_copy(first + i + 1, 1 - slot).start()

        x_copy(first + i, slot).wait()

        @pl.when(i >= 2)
        def _():
            o_copy(first + i - 2, slot).wait()

        obuf[slot] = compute(xbuf[slot])
        o_copy(first + i, slot).start()
        return carry

    lax.fori_loop(0, nblk, body, 0)

    @pl.when(nblk >= 2)
    def _():
        o_copy(first + nblk - 2, nblk % 2).wait()

    @pl.when(nblk >= 1)
    def _():
        o_copy(first + nblk - 1, (nblk - 1) % 2).wait()

    return o_copy


def _zero_unused_blocks(o_copy, obuf, lo, hi):
    obuf[0] = jnp.zeros(obuf.shape[1:], obuf.dtype)

    def body(blk, carry):
        cp = o_copy(blk, 0)
        cp.start()
        cp.wait()
        return carry

    lax.fori_loop(lo, hi, body, 0)


def _gateup_kernel(fb_ref, nbk_ref, nu_ref, st_ref, sb_ref, hn_hbm, wg_ref, wu_ref, bg_ref, bu_ref,
                   hb_hbm, xs_hbm, wgb_ref, wub_ref, xa_ref, xb_ref, obuf, xsbuf, xsem, osem, xssem,
                   *, tm, tn, n_blocks):
    j = pl.program_id(0)
    e = pl.program_id(1)
    wgb_ref[...] = wg_ref[0].astype(bf16)
    wub_ref[...] = wu_ref[0].astype(bf16)
    col0 = pl.multiple_of(j * tn, tn)
    first = fb_ref[e]
    nblk = nbk_ref[e]
    xbufs = (xa_ref, xb_ref)

    def compute(x):
        gl = jnp.dot(x, wgb_ref[...], preferred_element_type=f32) + bg_ref[0]
        up = jnp.dot(x, wub_ref[...], preferred_element_type=f32) + bu_ref[0]
        gl = jnp.minimum(gl, SWIGLU_LIMIT)
        up = jnp.clip(up, -SWIGLU_LIMIT, SWIGLU_LIMIT)
        return ((up + 1.0) * (gl * _sigmoid(SWIGLU_ALPHA * gl))).astype(bf16)

    def hb_dst(r0):
        return hb_hbm.at[pl.ds(r0, tm), pl.ds(col0, tn)]

    @pl.when(j == 0)
    def _gather_pass():
        def issue(g, p):
            base = sb_ref[g]
            _start_row_gathers(lambda r: st_ref[base + r], tm, hn_hbm, xbufs[p], xsem.at[p])

        def wait_x(p):
            _wait_row_gathers(tm, hn_hbm, xbufs[p], xsem.at[p])

        def o_copy(g, slot):
            return pltpu.make_async_copy(obuf.at[slot], hb_dst(pl.multiple_of(g * tm, tm)), osem.at[slot])

        def xs_copy(g, slot):
            return pltpu.make_async_copy(
                xsbuf.at[slot], xs_hbm.at[pl.ds(pl.multiple_of(g * tm, tm), tm)], xssem.at[slot])

        @pl.when(e == 0)
        def _():
            issue(0, 0)

        def body(i, carry):
            g = first + i
            oslot = i % 2

            @pl.when(i >= 2)
            def _():
                o_copy(g - 2, oslot).wait()
                xs_copy(g - 2, oslot).wait()

            for p in range(2):
                @pl.when(g % 2 == p)
                def _():
                    wait_x(p)
                    issue(g + 1, 1 - p)
                    x = xbufs[p][...].astype(bf16)
                    xsbuf[oslot] = x
                    obuf[oslot] = compute(x)

            o_copy(g, oslot).start()
            xs_copy(g, oslot).start()
            return carry

        lax.fori_loop(0, nblk, body, 0)

        @pl.when(nblk >= 2)
        def _():
            o_copy(first + nblk - 2, nblk % 2).wait()
            xs_copy(first + nblk - 2, nblk % 2).wait()

        @pl.when(nblk >= 1)
        def _():
            o_copy(first + nblk - 1, (nblk - 1) % 2).wait()
            xs_copy(first + nblk - 1, (nblk - 1) % 2).wait()

        @pl.when(e == N_EXPERTS - 1)
        def _():
            nu = nu_ref[0]
            for p in range(2):
                @pl.when(nu % 2 == p)
                def _():
                    wait_x(p)
            _zero_unused_blocks(o_copy, obuf, nu, n_blocks)
            _zero_unused_blocks(xs_copy, xsbuf, nu, n_blocks)

    @pl.when(j > 0)
    def _dense_pass():
        o_copy = _expert_row_blocks(first, nblk, xs_hbm, hb_dst, xsbuf, obuf, xssem, osem, tm, compute)

        @pl.when(e == N_EXPERTS - 1)
        def _():
            _zero_unused_blocks(o_copy, obuf, nu_ref[0], n_blocks)


def _gateup(first_blk, n_blk, n_used, sorted_tok, src_base, hn, w_gate_up, b_gate_up3, n_blocks, tm, tn):
    nj = D_FF // tn
    hb, _ = pl.pallas_call(
        functools.partial(_gateup_kernel, tm=tm, tn=tn, n_blocks=n_blocks),
        grid_spec=pltpu.PrefetchScalarGridSpec(
            num_scalar_prefetch=5,
            grid=(nj, N_EXPERTS),
            in_specs=[
                pl.BlockSpec(memory_space=pl.ANY),
                pl.BlockSpec((1, D_MODEL, tn), lambda j, e, *_: (e, 0, j)),
                pl.BlockSpec((1, D_MODEL, tn), lambda j, e, *_: (e, 0, nj + j)),
                pl.BlockSpec((1, 1, tn), lambda j, e, *_: (e, 0, j)),
                pl.BlockSpec((1, 1, tn), lambda j, e, *_: (e, 0, nj + j)),
            ],
            out_specs=[pl.BlockSpec(memory_space=pl.ANY), pl.BlockSpec(memory_space=pl.ANY)],
            scratch_shapes=[
                pltpu.VMEM((D_MODEL, tn), bf16), pltpu.VMEM((D_MODEL, tn), bf16),
                pltpu.VMEM((tm, D_MODEL), hn.dtype), pltpu.VMEM((tm, D_MODEL), hn.dtype),
                pltpu.VMEM((2, tm, tn), bf16), pltpu.VMEM((2, tm, D_MODEL), bf16),
                pltpu.SemaphoreType.DMA((2,)), pltpu.SemaphoreType.DMA((2,)), pltpu.SemaphoreType.DMA((2,)),
            ],
        ),
        out_shape=[jax.ShapeDtypeStruct((n_blocks * tm, D_FF), bf16),
                   jax.ShapeDtypeStruct((n_blocks * tm, D_MODEL), bf16)],
        compiler_params=_cparams(("arbitrary", "arbitrary")),
        name="moe_gate_up",
    )(first_blk, n_blk, n_used, sorted_tok, src_base, hn, w_gate_up, w_gate_up, b_gate_up3, b_gate_up3)
    return hb


def _down_kernel(fb_ref, nbk_ref, nu_ref, hb_hbm, w_ref, b_ref, y_hbm,
                 wb_ref, xbuf, obuf, xsem, osem, *, tm, n_blocks):
    e = pl.program_id(0)
    wb_ref[...] = w_ref[0].astype(bf16)

    def compute(x):
        return jnp.dot(x, wb_ref[...], preferred_element_type=f32) + b_ref[0]

    o_copy = _expert_row_blocks(fb_ref[e], nbk_ref[e], hb_hbm, lambda r0: y_hbm.at[pl.ds(r0, tm)],
                                xbuf, obuf, xsem, osem, tm, compute)

    @pl.when(e == N_EXPERTS - 1)
    def _():
        _zero_unused_blocks(o_copy, obuf, nu_ref[0], n_blocks)


def _down(first_blk, n_blk, n_used, hb, w_down, b_down3, tm):
    n_rows = hb.shape[0]
    return pl.pallas_call(
        functools.partial(_down_kernel, tm=tm, n_blocks=n_rows // tm),
        grid_spec=pltpu.PrefetchScalarGridSpec(
            num_scalar_prefetch=3,
            grid=(N_EXPERTS,),
            in_specs=[
                pl.BlockSpec(memory_space=pl.ANY),
                pl.BlockSpec((1, D_FF, D_MODEL), lambda e, *_: (e, 0, 0)),
                pl.BlockSpec((1, 1, D_MODEL), lambda e, *_: (e, 0, 0)),
            ],
            out_specs=pl.BlockSpec(memory_space=pl.ANY),
            scratch_shapes=[
                pltpu.VMEM((D_FF, D_MODEL), bf16),
                pltpu.VMEM((2, tm, D_FF), bf16), pltpu.VMEM((2, tm, D_MODEL), f32),
                pltpu.SemaphoreType.DMA((2,)), pltpu.SemaphoreType.DMA((2,)),
            ],
        ),
        out_shape=jax.ShapeDtypeStruct((n_rows, D_MODEL), f32),
        compiler_params=_cparams(("arbitrary",)),
        name="moe_down",
    )(first_blk, n_blk, n_used, hb, w_down, b_down3)


def _combine_kernel(pos_ref, h_ref, gt_ref, y_ref, o_ref, buf_ref, sem, *, tc):
    i = pl.program_id(0)
    n = pl.num_programs(0)
    rows = TOP_K * tc

    def issue(step, slot):
        base = step * rows
        _start_row_gathers(lambda r: pos_ref[base + r], rows, y_ref, buf_ref.at[slot], sem.at[slot])

    @pl.when(i == 0)
    def _():
        issue(0, 0)

    @pl.when(i + 1 < n)
    def _():
        issue(i + 1, (i + 1) % 2)

    slot = i % 2
    _wait_row_gathers(rows, y_ref, buf_ref.at[slot], sem.at[slot])
    gt = gt_ref[...]
    out = h_ref[...]
    for k in range(TOP_K):
        out = out + gt[:, k:k + 1] * buf_ref[slot, k * tc:(k + 1) * tc, :]
    o_ref[...] = out


def _combine(pos_blocks, h, gate_pad, y, tc):
    T = h.shape[0]
    return pl.pallas_call(
        functools.partial(_combine_kernel, tc=tc),
        grid_spec=pltpu.PrefetchScalarGridSpec(
            num_scalar_prefetch=1,
            grid=(T // tc,),
            in_specs=[
                pl.BlockSpec((tc, D_MODEL), lambda i, pos: (i, 0)),
                pl.BlockSpec((tc, LANES), lambda i, pos: (i, 0)),
                pl.BlockSpec(memory_space=pl.ANY),
            ],
            out_specs=pl.BlockSpec((tc, D_MODEL), lambda i, pos: (i, 0)),
            scratch_shapes=[pltpu.VMEM((2, TOP_K * tc, D_MODEL), f32), pltpu.SemaphoreType.DMA((2,))],
        ),
        out_shape=jax.ShapeDtypeStruct((T, D_MODEL), f32),
        compiler_params=_cparams(("arbitrary",)),
        name="moe_combine",
    )(pos_blocks, h, gate_pad, y)


def _rearranged_w_in(w_in):
    splits = np.cumsum([0, 1024, 256, 256, 256, 256, 256, 256, 24, 1024, 1024, 1024])
    q, kc, vc, ks, vs, kw, vw, g, qd, kd, vd = [w_in[:, splits[n]:splits[n + 1]] for n in range(11)]
    gcols = NSA_GROUP * N_BRANCH
    zpad = lambda n: jnp.zeros((D_MODEL, n), w_in.dtype)
    w_norm = jnp.concatenate([q, ks, kw, qd, kd], axis=1)
    w_raw = jnp.concatenate([kc, vc, vs, vw, vd,
                             g[:, :gcols], zpad(HEAD_DIM - gcols), g[:, gcols:], zpad(HEAD_DIM - gcols),
                             zpad((N_RAW_HEADS - HR_GATE - 2) * HEAD_DIM)], axis=1)
    return w_norm.astype(bf16), w_raw.astype(bf16)


def _sel_map_t(S, nselp):
    n_chunks = S // CMP_STRIDE
    n_cmp = (S - CMP_LEN) // CMP_STRIDE + 1
    n_sel = S // SEL_BLOCK
    c_start = np.arange(n_chunks) * CMP_STRIDE
    s_start = np.arange(nselp) * SEL_BLOCK
    ov = (c_start[None, :] < s_start[:, None] + SEL_BLOCK) & (c_start[None, :] + CMP_LEN > s_start[:, None])
    ov &= (np.arange(n_chunks)[None, :] < n_cmp) & (np.arange(nselp)[:, None] < n_sel)
    return jnp.asarray(ov, bf16)


def _attention_block(x2, B, S, attn_norm_g, w_in, nsa_q_norm_g, nsa_k_norm_g, pos_k, pos_v,
                     k_w1, k_w2, v_w1, v_w2, diff_q_norm_g, diff_k_norm_g, lq1, lk1, lq2, lk2,
                     subln_g, lambda_init):
    T = B * S
    scale = HEAD_DIM ** -0.5
    tiles = _tiles(S)
    tm = tiles["proj_rows"]
    rope_c, rope_sa, rope_sb = _rope_tables(np.arange(S))
    gains = jnp.concatenate([
        jnp.broadcast_to(nsa_q_norm_g * scale, (NSA_HEADS, HEAD_DIM)),
        jnp.broadcast_to(nsa_k_norm_g[1], (NSA_KV_HEADS, HEAD_DIM)),
        jnp.broadcast_to(nsa_k_norm_g[2], (NSA_KV_HEADS, HEAD_DIM)),
        jnp.broadcast_to(diff_q_norm_g * scale, (2 * DIFF_HEADS, HEAD_DIM)),
        jnp.broadcast_to(diff_k_norm_g, (2 * DIFF_HEADS, HEAD_DIM)),
    ], axis=0).reshape(N_NORM_HEADS, 1, HEAD_DIM)
    w_norm, w_raw = _rearranged_w_in(w_in)
    Pn, Pr, gate = _inproj(x2, attn_norm_g.reshape(1, D_MODEL), w_norm, w_raw, gains,
                           rope_c, rope_sa, rope_sb, S, tm)

    n_chunks = S // CMP_STRIDE
    cmp_pos = np.arange(n_chunks) * CMP_STRIDE + (CMP_LEN - 1)
    cc, csa, csb = _rope_tables(cmp_pos)
    chunk_view = lambda hd: Pr[hd:hd + NSA_KV_HEADS].reshape(NSA_KV_HEADS * B, n_chunks, CMP_STRIDE * HEAD_DIM)
    kcmp = _compress(chunk_view(HR_KC), pos_k, k_w1, k_w2, nsa_k_norm_g[0].reshape(1, HEAD_DIM),
                     cc, csa, csb, True)
    vcmp = _compress(chunk_view(HR_VC), pos_v, v_w1, v_w2, nsa_k_norm_g[0].reshape(1, HEAD_DIM),
                     cc, csa, csb, False)

    nselp = -(-(S // SEL_BLOCK) // LANES) * LANES
    tq = tiles["attn_q"]
    o_c, bias = _nsa_cmp(Pn, kcmp, vcmp, gate, _sel_map_t(S, nselp), B, S, tq)
    o_nsa = _nsa_sel(Pn, Pr, bias, gate, o_c, B, S, tq, tiles["sel_keys"], tq)

    lam_vecs = jnp.concatenate([lq1[None], lk1[None], lq2[None], lk2[None],
                                jnp.zeros((4, HEAD_DIM), f32)], axis=0)
    o_diff = _diff_attn(Pn, Pr, lam_vecs, subln_g.reshape(1, DIFF_V_DIM), B, S,
                        tiles["diff_q"], tiles["diff_q"], lambda_init)
    return o_nsa, o_diff


def _moe_block(h, hn, logits, w_gate_up, b_gate_up, w_down, b_down, tiles):
    T = h.shape[0]
    tm = tiles["moe_rows"]
    TK = T * TOP_K
    top_v, top_e = lax.top_k(logits[:, :N_EXPERTS], TOP_K)
    gate = jax.nn.softmax(top_v, axis=-1)
    flat_e = top_e.reshape(TK).astype(i32)
    se, order = lax.sort((flat_e, jnp.arange(TK, dtype=i32)), num_keys=1, is_stable=True)
    st = order // TOP_K
    experts = jnp.arange(N_EXPERTS, dtype=i32)
    counts = jnp.sum((flat_e[:, None] == experts[None, :]).astype(i32), axis=0)
    padded = (counts + tm - 1) // tm * tm
    pad_end = jnp.cumsum(padded)
    pad_start = pad_end - padded
    cnt_start = jnp.cumsum(counts) - counts
    dest = jnp.arange(TK, dtype=i32) + (pad_start - cnt_start)[se]
    _, pos = lax.sort((order, dest), num_keys=1)
    n_blocks = -(-TK // tm) + N_EXPERTS
    blk_start = jnp.arange(n_blocks, dtype=i32) * tm
    blk_e = jnp.minimum(jnp.sum((pad_end[None, :] <= blk_start[:, None]).astype(i32), axis=1), N_EXPERTS - 1)
    src_base = cnt_start[blk_e] + (blk_start - pad_start[blk_e])
    n_used = (pad_end[-1] // tm).astype(i32).reshape(1)

    src_base = jnp.minimum(jnp.concatenate([src_base, src_base[-1:]]), TK).astype(i32)
    st = jnp.concatenate([st, jnp.zeros((tm,), i32)])
    first_blk = (pad_start // tm).astype(i32)
    n_blk = (padded // tm).astype(i32)
    hb = _gateup(first_blk, n_blk, n_used, st, src_base, hn, w_gate_up,
                 b_gate_up.reshape(N_EXPERTS, 1, 2 * D_FF), n_blocks, tm, tiles["moe_cols"])
    y = _down(first_blk, n_blk, n_used, hb, w_down, b_down.reshape(N_EXPERTS, 1, D_MODEL), tm)

    tc = tiles["combine_rows"]
    pos_blocks = pos.reshape(T // tc, tc, TOP_K).transpose(0, 2, 1).reshape(TK)
    gate_pad = jnp.pad(gate, ((0, 0), (0, LANES - TOP_K)))
    return _combine(pos_blocks, h, gate_pad, y, tc)


def kernel(x, attn_norm_g, w_in, nsa_q_norm_g, nsa_k_norm_g, nsa_cmp_pos_k, nsa_cmp_pos_v, nsa_cmp_k_w1, nsa_cmp_k_w2, nsa_cmp_v_w1, nsa_cmp_v_w2, diff_q_norm_g, diff_k_norm_g, diff_lambda_q1, diff_lambda_k1, diff_lambda_q2, diff_lambda_k2, diff_subln_g, w_out, ffn_norm_g, w_router, b_router, w_gate_up, b_gate_up, w_down, b_down):
    B, S, _ = x.shape
    T = B * S
    depth = attn_norm_g.shape[0]
    h = x.reshape(T, D_MODEL)
    for l in range(depth):
        lambda_init = 0.8 - 0.6 * math.exp(-0.3 * l)
        o_nsa, o_diff = _attention_block(
            h, B, S, attn_norm_g[l], w_in[l], nsa_q_norm_g[l], nsa_k_norm_g[l], nsa_cmp_pos_k[l],
            nsa_cmp_pos_v[l], nsa_cmp_k_w1[l], nsa_cmp_k_w2[l], nsa_cmp_v_w1[l], nsa_cmp_v_w2[l],
            diff_q_norm_g[l], diff_k_norm_g[l], diff_lambda_q1[l], diff_lambda_k1[l],
            diff_lambda_q2[l], diff_lambda_k2[l], diff_subln_g[l], lambda_init)
        w_router_f = jnp.pad(w_router[l], ((0, 0), (0, LANES - N_EXPERTS)))
        w_router_hi = w_router_f.astype(bf16)
        w_router_lo = (w_router_f - w_router_hi.astype(f32)).astype(bf16)
        w_router_p = jnp.concatenate([w_router_hi, w_router_lo], axis=1)
        b_router_p = jnp.pad(b_router[l], (0, LANES - N_EXPERTS)).reshape(1, LANES)
        h_mid, hn, logits = _outproj(h, o_nsa, o_diff, w_out[l].astype(bf16),
                                     ffn_norm_g[l].reshape(1, D_MODEL), w_router_p, b_router_p,
                                     _tiles(S)["out_rows"])
        h = _moe_block(h_mid, hn, logits, w_gate_up[l], b_gate_up[l], w_down[l], b_down[l], _tiles(S))
    return h.reshape(B, S, D_MODEL)
```

```python
import functools
import math

import numpy as np
import jax
import jax.numpy as jnp
from jax import lax
from jax.experimental import pallas as pl
from jax.experimental.pallas import tpu as pltpu

f32 = jnp.float32
bf16 = jnp.bfloat16
i32 = jnp.int32

D_MODEL = 2048
HEAD_DIM = 128
ROT_DIM = HEAD_DIM // 4
ROPE_THETA = 500000.0
NORM_EPS = 1e-6
NEG_INF = -1e30

NSA_HEADS = 8
NSA_KV_HEADS = 2
NSA_GROUP = NSA_HEADS // NSA_KV_HEADS
N_BRANCH = 3
CMP_LEN = 32
CMP_STRIDE = 16
CMP_HIDDEN = 2 * HEAD_DIM
SEL_BLOCK = 64
SEL_SHIFT = 6
N_SELECTED = 16
WINDOW = 512
FORCED_SCORE = 1e9

DIFF_HEADS = 4
DIFF_V_DIM = 2 * HEAD_DIM

N_EXPERTS = 32
TOP_K = 4
D_FF = D_MODEL
SWIGLU_ALPHA = 1.702
SWIGLU_LIMIT = 7.0

LANES = 128

HD_Q = 0
HD_KS = 8
HD_KW = 10
HD_QD = 12
HD_KD = 20
N_NORM_HEADS = 28
HR_KC = 0
HR_VC = 2
HR_VS = 4
HR_VW = 6
HR_VD = 8
HR_GATE = 16
N_RAW_HEADS = 20
HP = 4

MASK_VAL = -1e30
M_INIT = -5e29

V7X_VMEM_BYTES = 64 * 1024 * 1024
VMEM_LIMIT = V7X_VMEM_BYTES - 8 * 1024 * 1024


def _tiles(S):
    return dict(
        proj_rows=min(1024, S),
        attn_q=min(256, S),
        sel_keys=min(1024, S),
        diff_q=min(512, S),
        out_rows=512,
        moe_rows=256,
        moe_cols=1024,
        combine_rows=128,
    )


def _cparams(sem):
    return pltpu.CompilerParams(dimension_semantics=sem, vmem_limit_bytes=VMEM_LIMIT)


def _rope_tables(pos):
    inv = np.power(ROPE_THETA, -np.arange(0, ROT_DIM, 2, dtype=np.float64) / ROT_DIM)
    ang = pos.astype(np.float64)[:, None] * inv[None, :]
    cos, sin = np.cos(ang), np.sin(ang)
    n = pos.shape[0]
    half = ROT_DIM // 2
    c = np.concatenate([cos, cos, np.ones((n, HEAD_DIM - ROT_DIM))], axis=1)
    sa = np.concatenate([-sin, np.zeros((n, HEAD_DIM - half))], axis=1)
    sb = np.concatenate([np.zeros((n, half)), sin, np.zeros((n, HEAD_DIM - ROT_DIM))], axis=1)
    return (jnp.asarray(c, f32), jnp.asarray(sa, f32), jnp.asarray(sb, f32))


def _rope(y, c, sa, sb):
    half = ROT_DIM // 2
    return (y * c + pltpu.roll(y, HEAD_DIM - half, 1) * sa + pltpu.roll(y, half, 1) * sb)


def _rms(y, gain):
    ms = jnp.mean(y * y, axis=-1, keepdims=True)
    return y * lax.rsqrt(ms + NORM_EPS) * gain


def _inproj_norm_kernel(x_ref, g_ref, w_ref, gain_ref, c_ref, sa_ref, sb_ref, o_ref, xn_ref):
    @pl.when(pl.program_id(1) == 0)
    def _():
        xn_ref[...] = _rms(x_ref[...], g_ref[...]).astype(bf16)

    tm = x_ref.shape[0]
    rc = min(ROW_CHUNK, tm)
    for r in range(tm // rc):
        rows = slice(r * rc, (r + 1) * rc)
        acc = jnp.dot(xn_ref[rows, :], w_ref[...], preferred_element_type=f32)
        c, sa, sb = c_ref[rows, :], sa_ref[rows, :], sb_ref[rows, :]
        for u in range(HP):
            y = _rms(acc[:, u * HEAD_DIM:(u + 1) * HEAD_DIM], gain_ref[u])
            o_ref[u, rows, :] = _rope(y, c, sa, sb).astype(bf16)


def _inproj_raw_kernel(x_ref, g_ref, w_ref, o_ref, gate_ref, xn_ref):
    j = pl.program_id(1)

    @pl.when(j == 0)
    def _():
        xn_ref[...] = _rms(x_ref[...], g_ref[...]).astype(bf16)

    tm = x_ref.shape[0]
    rc = min(ROW_CHUNK, tm)
    for r in range(tm // rc):
        rows = slice(r * rc, (r + 1) * rc)
        acc = jnp.dot(xn_ref[rows, :], w_ref[...], preferred_element_type=f32)
        for u in range(HP):
            o_ref[u, rows, :] = acc[:, u * HEAD_DIM:(u + 1) * HEAD_DIM].astype(bf16)

    @pl.when(j == HR_GATE // HP)
    def _():
        gate_ref[0] = jnp.dot(xn_ref[...], w_ref[:, 0:HEAD_DIM], preferred_element_type=f32)
        gate_ref[1] = jnp.dot(xn_ref[...], w_ref[:, HEAD_DIM:2 * HEAD_DIM], preferred_element_type=f32)


def _inproj(x2, attn_g, w_norm, w_raw, gains, rope_c, rope_sa, rope_sb, S, tm):
    T = x2.shape[0]
    nsb = S // tm
    x_specs = [
        pl.BlockSpec((tm, D_MODEL), lambda i, j: (i, 0)),
        pl.BlockSpec((1, D_MODEL), lambda i, j: (0, 0)),
        pl.BlockSpec((D_MODEL, HP * HEAD_DIM), lambda i, j: (0, j)),
    ]
    head_spec = pl.BlockSpec((HP, tm, HEAD_DIM), lambda i, j: (j, i, 0))
    pos_spec = pl.BlockSpec((tm, HEAD_DIM), lambda i, j: (i % nsb, 0))
    Pn = pl.pallas_call(
        _inproj_norm_kernel,
        grid=(T // tm, N_NORM_HEADS // HP),
        in_specs=x_specs + [pl.BlockSpec((HP, 1, HEAD_DIM), lambda i, j: (j, 0, 0)),
                            pos_spec, pos_spec, pos_spec],
        out_specs=head_spec,
        out_shape=jax.ShapeDtypeStruct((N_NORM_HEADS, T, HEAD_DIM), bf16),
        scratch_shapes=[pltpu.VMEM((tm, D_MODEL), bf16)],
        compiler_params=_cparams(("parallel", "arbitrary")),
        name="inproj_norm",
    )(x2, attn_g, w_norm, gains, rope_c, rope_sa, rope_sb)
    Pr, gate = pl.pallas_call(
        _inproj_raw_kernel,
        grid=(T // tm, N_RAW_HEADS // HP),
        in_specs=x_specs,
        out_specs=[head_spec, pl.BlockSpec((2, tm, HEAD_DIM), lambda i, j: (0, i, 0))],
        out_shape=[
            jax.ShapeDtypeStruct((N_RAW_HEADS, T, HEAD_DIM), bf16),
            jax.ShapeDtypeStruct((2, T, HEAD_DIM), f32),
        ],
        scratch_shapes=[pltpu.VMEM((tm, D_MODEL), bf16)],
        compiler_params=_cparams(("parallel", "arbitrary")),
        name="inproj_raw",
    )(x2, attn_g, w_raw)
    return Pn, Pr, gate


def _gelu_tanh(x):
    return 0.5 * x * (1.0 + jnp.tanh(math.sqrt(2.0 / math.pi) * (x + 0.044715 * (x * x * x))))


def _compress_kernel(c_ref, pe_ref, w1_ref, w2_ref, gain_ref, rc_ref, rsa_ref, rsb_ref,
                     o_ref, *, do_norm):
    half = CMP_STRIDE * HEAD_DIM
    c = c_ref[0]
    nc = c.shape[0]
    w1 = w1_ref[...].astype(bf16)
    a = jnp.dot(c, w1[:half], preferred_element_type=f32)
    b = jnp.dot(c, w1[half:], preferred_element_type=f32)
    peb = jnp.dot(pe_ref[...], w1, preferred_element_type=f32)[0:1]
    hid = a + pltpu.roll(b, nc - 1, 0) + peb
    act = _gelu_tanh(hid)
    out = jnp.dot(act.astype(bf16), w2_ref[...].astype(bf16), preferred_element_type=f32)
    if do_norm:
        out = _rope(_rms(out, gain_ref[...]), rc_ref[...], rsa_ref[...], rsb_ref[...])
    o_ref[0] = out.astype(bf16)


def _compress(chunks, pe, w1, w2, gain, rc, rsa, rsb, do_norm):
    ng, nc, width = chunks.shape
    pe8 = jnp.broadcast_to(pe.reshape(1, CMP_LEN * HEAD_DIM), (8, CMP_LEN * HEAD_DIM)).astype(bf16)
    full = lambda shape: pl.BlockSpec(shape, lambda g: (0,) * len(shape))
    return pl.pallas_call(
        functools.partial(_compress_kernel, do_norm=do_norm),
        grid=(ng,),
        in_specs=[
            pl.BlockSpec((1, nc, width), lambda g: (g, 0, 0)),
            full((8, CMP_LEN * HEAD_DIM)),
            full((CMP_LEN * HEAD_DIM, CMP_HIDDEN)),
            full((CMP_HIDDEN, HEAD_DIM)),
            full((1, HEAD_DIM)),
            full((nc, HEAD_DIM)), full((nc, HEAD_DIM)), full((nc, HEAD_DIM)),
        ],
        out_specs=pl.BlockSpec((1, nc, HEAD_DIM), lambda g: (g, 0, 0)),
        out_shape=jax.ShapeDtypeStruct((ng, nc, HEAD_DIM), bf16),
        compiler_params=_cparams(("parallel",)),
        name="compress_k" if do_norm else "compress_v",
    )(chunks, pe8, w1, w2, gain, rc, rsa, rsb)


def _sigmoid(x):
    return 1.0 / (1.0 + jnp.exp(-x))


def _nsa_cmp_kernel(q_ref, kc_ref, vc_ref, g_ref, smt_ref, oc_ref, bias_ref, *, tq):
    q0 = pl.program_id(2) * tq
    kc = kc_ref[0]
    vc = vc_ref[0]
    nc = kc.shape[0]
    R = NSA_GROUP * tq
    sig = _sigmoid(g_ref[0])
    q = q_ref[...].reshape(R, HEAD_DIM)
    s = lax.dot_general(q, kc, (((1,), (1,)), ((), ())), preferred_element_type=f32)
    t = q0 + (lax.broadcasted_iota(i32, (R, nc), 0) & (tq - 1))
    cend = lax.broadcasted_iota(i32, (R, nc), 1) * CMP_STRIDE + (CMP_LEN - 1)
    mask = cend <= t
    s = jnp.where(mask, s, NEG_INF)
    m = jnp.max(s, axis=-1, keepdims=True)
    p = jnp.where(mask, jnp.exp2(s - m), 0.0)
    p = p / jnp.maximum(jnp.sum(p, axis=-1, keepdims=True), 1e-30)
    o = jnp.dot(p.astype(bf16), vc, preferred_element_type=f32)
    for gi in range(NSA_GROUP):
        col = gi * N_BRANCH
        oc_ref[:, gi * HEAD_DIM:(gi + 1) * HEAD_DIM] = o[gi * tq:(gi + 1) * tq] * sig[:, col:col + 1]

    psum = p[0:tq] + p[tq:2 * tq] + p[2 * tq:3 * tq] + p[3 * tq:4 * tq]
    ps_hi = psum.astype(bf16)
    ps_lo = (psum - ps_hi.astype(f32)).astype(bf16)
    nt = (((1,), (1,)), ((), ()))
    imp = (lax.dot_general(smt_ref[...], ps_hi, nt, preferred_element_type=f32)
           + lax.dot_general(smt_ref[...], ps_lo, nt, preferred_element_type=f32))
    nselp = imp.shape[0]
    jf = lax.broadcasted_iota(i32, (nselp, tq), 0).astype(f32)
    cur = ((q0 + lax.broadcasted_iota(i32, (nselp, tq), 1)) >> SEL_SHIFT).astype(f32)
    forced = (jf == 0.0) | (jf == cur) | (jf == cur - 1.0)
    imp = jnp.where(forced, FORCED_SCORE, imp)
    imp = jnp.where(jf <= cur, imp, NEG_INF)
    PICKED = -3e38
    work = imp
    for _ in range(N_SELECTED):
        mx = jnp.max(work, axis=0, keepdims=True)
        first = jnp.min(jnp.where(work == mx, jf, float(nselp)), axis=0, keepdims=True)
        work = jnp.where(jf == first, PICKED, work)
    bias_t = jnp.where((work < -2e38) & (imp > -1.0), 0.0, MASK_VAL)
    bias_ref[0] = bias_t.T.astype(bf16)


def _nsa_cmp(P, kcmp, vcmp, gate, smt, B, S, tq):
    T = B * S
    nqb = S // tq
    nc = kcmp.shape[1]
    nselp = smt.shape[0]
    return pl.pallas_call(
        functools.partial(_nsa_cmp_kernel, tq=tq),
        grid=(NSA_KV_HEADS, B, nqb),
        in_specs=[
            pl.BlockSpec((NSA_GROUP, tq, HEAD_DIM), lambda h, b, i: (h, b * nqb + i, 0)),
            pl.BlockSpec((1, nc, HEAD_DIM), lambda h, b, i: (h * B + b, 0, 0)),
            pl.BlockSpec((1, nc, HEAD_DIM), lambda h, b, i: (h * B + b, 0, 0)),
            pl.BlockSpec((1, tq, HEAD_DIM), lambda h, b, i: (h, b * nqb + i, 0)),
            pl.BlockSpec((nselp, nc), lambda h, b, i: (0, 0)),
        ],
        out_specs=[
            pl.BlockSpec((tq, NSA_GROUP * HEAD_DIM), lambda h, b, i: (b * nqb + i, h)),
            pl.BlockSpec((1, tq, nselp), lambda h, b, i: (h, b * nqb + i, 0)),
        ],
        out_shape=[
            jax.ShapeDtypeStruct((T, NSA_HEADS * HEAD_DIM), f32),
            jax.ShapeDtypeStruct((NSA_KV_HEADS, T, nselp), bf16),
        ],
        compiler_params=_cparams(("parallel", "parallel", "parallel")),
        name="nsa_cmp",
    )(P, kcmp, vcmp, gate, smt)


def _lane_tile(x, n):
    return x if n == 1 else jnp.concatenate([x] * n, axis=1)


ROW_CHUNK = 256


def _softmax_step(rows, s, v, m_ref, acc_ref, l_ref=None):
    tk = s.shape[1]
    m_prev = m_ref[rows, :]
    m_new = jnp.maximum(m_prev, jnp.max(s, axis=1, keepdims=True))
    alpha = jnp.exp2(m_prev - m_new)
    p = jnp.exp2(s - _lane_tile(m_new, tk // LANES))
    if l_ref is not None:
        l_ref[rows, :] = alpha * l_ref[rows, :] + jnp.sum(p, axis=1, keepdims=True)
    acc_ref[rows, :] = _lane_tile(alpha, 2) * acc_ref[rows, :] + jnp.dot(
        p.astype(bf16), v, preferred_element_type=f32)
    m_ref[rows, :] = m_new


def _pipelined_key_tiles(n_full, scores, softmax_pv, sa_ref, sb_ref):
    def pair(j, carry):
        scores(2 * j + 1, sb_ref)
        softmax_pv(2 * j, sa_ref, False)
        scores(2 * j + 2, sa_ref)
        softmax_pv(2 * j + 1, sb_ref, False)
        return carry

    scores(0, sa_ref)
    lax.fori_loop(0, n_full // 2, pair, 0)

    @pl.when(n_full % 2 == 0)
    def _():
        softmax_pv(n_full, sa_ref, True)

    @pl.when(n_full % 2 == 1)
    def _():
        scores(n_full, sb_ref)
        softmax_pv(n_full - 1, sa_ref, False)
        softmax_pv(n_full, sb_ref, True)


def _nsa_sel_kernel(q_ref, bias_ref, ks_ref, vs_ref, kw_ref, vw_ref, g_ref, oc_ref,
                    o_ref, qaug_ref, m_ref, acc_ref, sa_ref, sb_ref, *, tq, tk, tkw):
    q0 = pl.program_id(2) * tq
    R = NSA_GROUP * tq
    nselp = bias_ref.shape[2]
    q = q_ref[...].reshape(R, HEAD_DIM)
    qaug_ref[:, 0:HEAD_DIM] = q
    bias = bias_ref[0]
    for gi in range(NSA_GROUP):
        qaug_ref[gi * tq:(gi + 1) * tq, HEAD_DIM:HEAD_DIM + nselp] = bias
    sig = _sigmoid(g_ref[0])
    rc = min(ROW_CHUNK, tq)

    def row_t(c, width):
        return q0 + ((c * rc + lax.broadcasted_iota(i32, (rc, width), 0)) & (tq - 1))

    def init():
        m_ref[...] = jnp.full(m_ref.shape, M_INIT, f32)
        acc_ref[...] = jnp.zeros(acc_ref.shape, f32)

    def finish():
        return acc_ref[:, 0:HEAD_DIM] / jnp.maximum(acc_ref[:, HEAD_DIM:2 * HEAD_DIM], 1e-30)

    init()
    ones_k = jnp.ones((tk, HEAD_DIM), bf16)

    def sel_scores(kt, s_ref):
        k0 = pl.multiple_of(kt * tk, tk)
        k = ks_ref[0, pl.ds(k0, tk), :]
        blk = (k0 + lax.broadcasted_iota(i32, (tk, nselp), 0)) >> SEL_SHIFT
        onehot = jnp.where(blk == lax.broadcasted_iota(i32, (tk, nselp), 1), 1.0, 0.0).astype(bf16)
        kaug = jnp.concatenate([k, onehot], axis=1)
        for c in range(R // rc):
            rows = slice(c * rc, (c + 1) * rc)
            s_ref[rows, :] = lax.dot_general(qaug_ref[rows, :], kaug, (((1,), (1,)), ((), ())),
                                             preferred_element_type=f32)

    def sel_softmax_pv(kt, s_ref, diagonal):
        k0 = pl.multiple_of(kt * tk, tk)
        vaug = jnp.concatenate([vs_ref[0, pl.ds(k0, tk), :], ones_k], axis=1)
        for c in range(R // rc):
            rows = slice(c * rc, (c + 1) * rc)
            s = s_ref[rows, :]
            if diagonal:
                s = jnp.where(k0 + lax.broadcasted_iota(i32, (rc, tk), 1) <= row_t(c, tk), s, MASK_VAL)
            _softmax_step(rows, s, vaug, m_ref, acc_ref)

    _pipelined_key_tiles(q0 // tk, sel_scores, sel_softmax_pv, sa_ref, sb_ref)
    o_s = finish()

    qi = pl.program_id(2)
    n_win = WINDOW // tkw + 1
    w_tiles = [pl.multiple_of(jnp.maximum(qi - (n_win - 1) + w, 0) * tkw, tkw) for w in range(n_win)]
    kwin = jnp.concatenate([kw_ref[0, pl.ds(k0, tkw), :] for k0 in w_tiles], axis=0)
    vwin = jnp.concatenate([vw_ref[0, pl.ds(k0, tkw), :] for k0 in w_tiles], axis=0)
    vwin = jnp.concatenate([vwin, jnp.ones((n_win * tkw, HEAD_DIM), bf16)], axis=1)
    for c in range(R // rc):
        rows = slice(c * rc, (c + 1) * rc)
        s = lax.dot_general(qaug_ref[rows, 0:HEAD_DIM], kwin, (((1,), (1,)), ((), ())),
                            preferred_element_type=f32)
        t = row_t(c, n_win * tkw)
        kp = (qi - (n_win - 1)) * tkw + lax.broadcasted_iota(i32, (rc, n_win * tkw), 1)
        s = jnp.where((kp >= 0) & (kp <= t) & (t - kp < WINDOW), s, MASK_VAL)
        p = jnp.exp2(s - jnp.max(s, axis=1, keepdims=True))
        acc_ref[rows, :] = jnp.dot(p.astype(bf16), vwin, preferred_element_type=f32)
    o_w = finish()

    for gi in range(NSA_GROUP):
        rows = slice(gi * tq, (gi + 1) * tq)
        cols = slice(gi * HEAD_DIM, (gi + 1) * HEAD_DIM)
        c = gi * N_BRANCH
        o_ref[:, cols] = (oc_ref[:, cols] + o_s[rows] * sig[:, c + 1:c + 2]
                          + o_w[rows] * sig[:, c + 2:c + 3]).astype(bf16)


def _nsa_sel(Pn, Pr, bias, gate, o_c, B, S, tq, tk, tkw):
    assert tk % tq == 0 and tkw == tq and WINDOW % tq == 0
    T = B * S
    nqb = S // tq
    nselp = bias.shape[2]
    R = NSA_GROUP * tq
    kv_spec = lambda base: pl.BlockSpec((1, S, HEAD_DIM), lambda h, b, i: (base + h, b, 0))
    return pl.pallas_call(
        functools.partial(_nsa_sel_kernel, tq=tq, tk=tk, tkw=tkw),
        grid=(NSA_KV_HEADS, B, nqb),
        in_specs=[
            pl.BlockSpec((NSA_GROUP, tq, HEAD_DIM), lambda h, b, i: (h, b * nqb + i, 0)),
            pl.BlockSpec((1, tq, nselp), lambda h, b, i: (h, b * nqb + i, 0)),
            kv_spec(HD_KS), kv_spec(HR_VS), kv_spec(HD_KW), kv_spec(HR_VW),
            pl.BlockSpec((1, tq, HEAD_DIM), lambda h, b, i: (h, b * nqb + i, 0)),
            pl.BlockSpec((tq, NSA_GROUP * HEAD_DIM), lambda h, b, i: (b * nqb + i, h)),
        ],
        out_specs=pl.BlockSpec((tq, NSA_GROUP * HEAD_DIM), lambda h, b, i: (b * nqb + i, h)),
        out_shape=jax.ShapeDtypeStruct((T, NSA_HEADS * HEAD_DIM), bf16),
        scratch_shapes=[
            pltpu.VMEM((R, HEAD_DIM + nselp), bf16),
            pltpu.VMEM((R, LANES), f32),
            pltpu.VMEM((R, 2 * HEAD_DIM), f32),
            pltpu.VMEM((R, tk), f32),
            pltpu.VMEM((R, tk), f32),
        ],
        compiler_params=_cparams(("parallel", "parallel", "arbitrary")),
        name="nsa_sel_win",
    )(Pn, bias, Pn, Pr, Pn, Pr, gate, o_c)


def _diff_kernel(q_ref, k_ref, v_ref, lam_ref, sg_ref, o_ref, m_ref, l_ref, acc_ref, sa_ref, sb_ref,
                 *, tq, tk, lambda_init):
    q0 = pl.program_id(2) * tq
    R = 2 * tq
    rc = tq
    m_ref[...] = jnp.full(m_ref.shape, M_INIT, f32)
    l_ref[...] = jnp.zeros(l_ref.shape, f32)
    acc_ref[...] = jnp.zeros(acc_ref.shape, f32)

    def scores(kt, s_ref):
        k0 = pl.multiple_of(kt * tk, tk)
        for comp in range(2):
            s_ref[comp * tq:(comp + 1) * tq, :] = lax.dot_general(
                q_ref[comp], k_ref[comp, pl.ds(k0, tk), :], (((1,), (1,)), ((), ())),
                preferred_element_type=f32)

    def softmax_pv(kt, s_ref, diagonal):
        k0 = pl.multiple_of(kt * tk, tk)
        v = jnp.concatenate([v_ref[0, pl.ds(k0, tk), :], v_ref[1, pl.ds(k0, tk), :]], axis=1)
        for c in range(R // rc):
            rows = slice(c * rc, (c + 1) * rc)
            s = s_ref[rows, :]
            if diagonal:
                t = q0 + ((c * rc + lax.broadcasted_iota(i32, (rc, tk), 0)) & (tq - 1))
                s = jnp.where(k0 + lax.broadcasted_iota(i32, (rc, tk), 1) <= t, s, MASK_VAL)
            _softmax_step(rows, s, v, m_ref, acc_ref, l_ref)

    _pipelined_key_tiles(pl.program_id(2), scores, softmax_pv, sa_ref, sb_ref)

    lq = lam_ref[...]
    lam = (jnp.exp(jnp.sum(lq[0:1] * lq[1:2], axis=1, keepdims=True))
           - jnp.exp(jnp.sum(lq[2:3] * lq[3:4], axis=1, keepdims=True)) + lambda_init)
    o = acc_ref[...] / _lane_tile(jnp.maximum(l_ref[...], 1e-30), 2)
    a = o[0:tq] - lam * o[tq:R]
    o_ref[...] = (_rms(a, sg_ref[...]) * (1.0 - lambda_init)).astype(bf16)


def _diff_attn(Pn, Pr, lam_vecs, subln_g, B, S, tq, tk, lambda_init):
    assert tq == tk
    T = B * S
    nqb = S // tq
    return pl.pallas_call(
        functools.partial(_diff_kernel, tq=tq, tk=tk, lambda_init=lambda_init),
        grid=(DIFF_HEADS, B, nqb),
        in_specs=[
            pl.BlockSpec((2, tq, HEAD_DIM), lambda h, b, i: (HD_QD // 2 + h, b * nqb + i, 0)),
            pl.BlockSpec((2, S, HEAD_DIM), lambda h, b, i: (HD_KD // 2 + h, b, 0)),
            pl.BlockSpec((2, S, HEAD_DIM), lambda h, b, i: (HR_VD // 2 + h, b, 0)),
            pl.BlockSpec((8, HEAD_DIM), lambda h, b, i: (0, 0)),
            pl.BlockSpec((1, DIFF_V_DIM), lambda h, b, i: (0, 0)),
        ],
        out_specs=pl.BlockSpec((tq, DIFF_V_DIM), lambda h, b, i: (b * nqb + i, h)),
        out_shape=jax.ShapeDtypeStruct((T, DIFF_HEADS * DIFF_V_DIM), bf16),
        scratch_shapes=[
            pltpu.VMEM((2 * tq, LANES), f32),
            pltpu.VMEM((2 * tq, LANES), f32),
            pltpu.VMEM((2 * tq, DIFF_V_DIM), f32),
            pltpu.VMEM((2 * tq, tk), f32),
            pltpu.VMEM((2 * tq, tk), f32),
        ],
        compiler_params=_cparams(("parallel", "parallel", "arbitrary")),
        name="diff_attn",
    )(Pn, Pn, Pr, lam_vecs, subln_g)


def _outproj_kernel(x_ref, on_ref, od_ref, w_ref, g_ref, wr_ref, br_ref,
                    h_ref, hn_ref, lg_ref):
    half = NSA_HEADS * HEAD_DIM
    tm = x_ref.shape[0]
    rc = min(ROW_CHUNK, tm)
    for c in range(tm // rc):
        rows = slice(c * rc, (c + 1) * rc)
        h = x_ref[rows, :] + jnp.dot(on_ref[rows, :], w_ref[0:half], preferred_element_type=f32) \
            + jnp.dot(od_ref[rows, :], w_ref[half:], preferred_element_type=f32)
        h_ref[rows, :] = h
        hn = _rms(h, g_ref[...])
        hn_ref[rows, :] = hn
        hn_hi = hn.astype(bf16)
        hn_lo = (hn - hn_hi.astype(f32)).astype(bf16)
        r = (jnp.dot(hn_hi, wr_ref[...], preferred_element_type=f32)
             + jnp.dot(hn_lo, wr_ref[...], preferred_element_type=f32))
        lg_ref[rows, :] = r[:, 0:LANES] + r[:, LANES:2 * LANES] + br_ref[...]


def _outproj(x2, o_nsa, o_diff, w_out_b, ffn_g, w_router_p, b_router_p, tm):
    T = x2.shape[0]
    full = lambda shape: pl.BlockSpec(shape, lambda i: (0,) * len(shape))
    return pl.pallas_call(
        _outproj_kernel,
        grid=(T // tm,),
        in_specs=[
            pl.BlockSpec((tm, D_MODEL), lambda i: (i, 0)),
            pl.BlockSpec((tm, NSA_HEADS * HEAD_DIM), lambda i: (i, 0)),
            pl.BlockSpec((tm, DIFF_HEADS * DIFF_V_DIM), lambda i: (i, 0)),
            full((D_MODEL, D_MODEL)),
            full((1, D_MODEL)),
            full((D_MODEL, 2 * LANES)),
            full((1, LANES)),
        ],
        out_specs=[
            pl.BlockSpec((tm, D_MODEL), lambda i: (i, 0)),
            pl.BlockSpec((tm, D_MODEL), lambda i: (i, 0)),
            pl.BlockSpec((tm, LANES), lambda i: (i, 0)),
        ],
        out_shape=[
            jax.ShapeDtypeStruct((T, D_MODEL), f32),
            jax.ShapeDtypeStruct((T, D_MODEL), f32),
            jax.ShapeDtypeStruct((T, LANES), f32),
        ],
        compiler_params=_cparams(("parallel",)),
        name="outproj_router",
    )(x2, o_nsa, o_diff, w_out_b, ffn_g, w_router_p, b_router_p)


def _start_row_gathers(idx_fn, n, src_ref, dst_ref, sem):
    for r in range(n):
        pltpu.make_async_copy(src_ref.at[pl.ds(idx_fn(r), 1)], dst_ref.at[pl.ds(r, 1)], sem).start()


def _wait_row_gathers(n, src_ref, dst_ref, sem):
    for r in range(n):
        pltpu.make_async_copy(src_ref.at[pl.ds(0, 1)], dst_ref.at[pl.ds(r, 1)], sem).wait()


def _expert_row_blocks(first, nblk, x_hbm, o_dst, xbuf, obuf, xsem, osem, tm, compute):
    def x_copy(blk, slot):
        return pltpu.make_async_copy(x_hbm.at[pl.ds(pl.multiple_of(blk * tm, tm), tm)],
                                     xbuf.at[slot], xsem.at[slot])

    def o_copy(blk, slot):
        return pltpu.make_async_copy(obuf.at[slot], o_dst(pl.multiple_of(blk * tm, tm)), osem.at[slot])

    @pl.when(nblk > 0)
    def _():
        x_copy(first, 0).start()

    def body(i, carry):
        slot = i % 2

        @pl.when(i + 1 < nblk)
        def _():
            x_copy(first + i + 1, 1 - slot).start()

        x_copy(first + i, slot).wait()

        @pl.when(i >= 2)
        def _():
            o_copy(first + i - 2, slot).wait()

        obuf[slot] = compute(xbuf[slot])
        o_copy(first + i, slot).start()
        return carry

    lax.fori_loop(0, nblk, body, 0)

    @pl.when(nblk >= 2)
    def _():
        o_copy(first + nblk - 2, nblk % 2).wait()

    @pl.when(nblk >= 1)
    def _():
        o_copy(first + nblk - 1, (nblk - 1) % 2).wait()

    return o_copy


def _zero_unused_blocks(o_copy, obuf, lo, hi):
    obuf[0] = jnp.zeros(obuf.shape[1:], obuf.dtype)

    def body(blk, carry):
        cp = o_copy(blk, 0)
        cp.start()
        cp.wait()
        return carry

    lax.fori_loop(lo, hi, body, 0)


def _gateup_kernel(fb_ref, nbk_ref, nu_ref, st_ref, sb_ref, hn_hbm, wg_ref, wu_ref, bg_ref, bu_ref,
                   hb_hbm, xs_hbm, wgb_ref, wub_ref, xa_ref, xb_ref, obuf, xsbuf, xsem, osem, xssem,
                   *, tm, tn, n_blocks):
    j = pl.program_id(0)
    e = pl.program_id(1)
    wgb_ref[...] = wg_ref[0].astype(bf16)
    wub_ref[...] = wu_ref[0].astype(bf16)
    col0 = pl.multiple_of(j * tn, tn)
    first = fb_ref[e]
    nblk = nbk_ref[e]
    xbufs = (xa_ref, xb_ref)

    def compute(x):
        gl = jnp.dot(x, wgb_ref[...], preferred_element_type=f32) + bg_ref[0]
        up = jnp.dot(x, wub_ref[...], preferred_element_type=f32) + bu_ref[0]
        gl = jnp.minimum(gl, SWIGLU_LIMIT)
        up = jnp.clip(up, -SWIGLU_LIMIT, SWIGLU_LIMIT)
        return ((up + 1.0) * (gl * _sigmoid(SWIGLU_ALPHA * gl))).astype(bf16)

    def hb_dst(r0):
        return hb_hbm.at[pl.ds(r0, tm), pl.ds(col0, tn)]

    @pl.when(j == 0)
    def _gather_pass():
        def issue(g, p):
            base = sb_ref[g]
            _start_row_gathers(lambda r: st_ref[base + r], tm, hn_hbm, xbufs[p], xsem.at[p])

        def wait_x(p):
            _wait_row_gathers(tm, hn_hbm, xbufs[p], xsem.at[p])

        def o_copy(g, slot):
            return pltpu.make_async_copy(obuf.at[slot], hb_dst(pl.multiple_of(g * tm, tm)), osem.at[slot])

        def xs_copy(g, slot):
            return pltpu.make_async_copy(
                xsbuf.at[slot], xs_hbm.at[pl.ds(pl.multiple_of(g * tm, tm), tm)], xssem.at[slot])

        @pl.when(e == 0)
        def _():
            issue(0, 0)

        def body(i, carry):
            g = first + i
            oslot = i % 2

            @pl.when(i >= 2)
            def _():
                o_copy(g - 2, oslot).wait()
                xs_copy(g - 2, oslot).wait()

            for p in range(2):
                @pl.when(g % 2 == p)
                def _():
                    wait_x(p)
                    issue(g + 1, 1 - p)
                    x = xbufs[p][...].astype(bf16)
                    xsbuf[oslot] = x
                    obuf[oslot] = compute(x)

            o_copy(g, oslot).start()
            xs_copy(g, oslot).start()
            return carry

        lax.fori_loop(0, nblk, body, 0)

        @pl.when(nblk >= 2)
        def _():
            o_copy(first + nblk - 2, nblk % 2).wait()
            xs_copy(first + nblk - 2, nblk % 2).wait()

        @pl.when(nblk >= 1)
        def _():
            o_copy(first + nblk - 1, (nblk - 1) % 2).wait()
            xs_copy(first + nblk - 1, (nblk - 1) % 2).wait()

        @pl.when(e == N_EXPERTS - 1)
        def _():
            nu = nu_ref[0]
            for p in range(2):
                @pl.when(nu % 2 == p)
                def _():
                    wait_x(p)
            _zero_unused_blocks(o_copy, obuf, nu, n_blocks)
            _zero_unused_blocks(xs_copy, xsbuf, nu, n_blocks)

    @pl.when(j > 0)
    def _dense_pass():
        o_copy = _expert_row_blocks(first, nblk, xs_hbm, hb_dst, xsbuf, obuf, xssem, osem, tm, compute)

        @pl.when(e == N_EXPERTS - 1)
        def _():
            _zero_unused_blocks(o_copy, obuf, nu_ref[0], n_blocks)


def _gateup(first_blk, n_blk, n_used, sorted_tok, src_base, hn, w_gate_up, b_gate_up3, n_blocks, tm, tn):
    nj = D_FF // tn
    hb, _ = pl.pallas_call(
        functools.partial(_gateup_kernel, tm=tm, tn=tn, n_blocks=n_blocks),
        grid_spec=pltpu.PrefetchScalarGridSpec(
            num_scalar_prefetch=5,
            grid=(nj, N_EXPERTS),
            in_specs=[
                pl.BlockSpec(memory_space=pl.ANY),
                pl.BlockSpec((1, D_MODEL, tn), lambda j, e, *_: (e, 0, j)),
                pl.BlockSpec((1, D_MODEL, tn), lambda j, e, *_: (e, 0, nj + j)),
                pl.BlockSpec((1, 1, tn), lambda j, e, *_: (e, 0, j)),
                pl.BlockSpec((1, 1, tn), lambda j, e, *_: (e, 0, nj + j)),
            ],
            out_specs=[pl.BlockSpec(memory_space=pl.ANY), pl.BlockSpec(memory_space=pl.ANY)],
            scratch_shapes=[
                pltpu.VMEM((D_MODEL, tn), bf16), pltpu.VMEM((D_MODEL, tn), bf16),
                pltpu.VMEM((tm, D_MODEL), hn.dtype), pltpu.VMEM((tm, D_MODEL), hn.dtype),
                pltpu.VMEM((2, tm, tn), bf16), pltpu.VMEM((2, tm, D_MODEL), bf16),
                pltpu.SemaphoreType.DMA((2,)), pltpu.SemaphoreType.DMA((2,)), pltpu.SemaphoreType.DMA((2,)),
            ],
        ),
        out_shape=[jax.ShapeDtypeStruct((n_blocks * tm, D_FF), bf16),
                   jax.ShapeDtypeStruct((n_blocks * tm, D_MODEL), bf16)],
        compiler_params=_cparams(("arbitrary", "arbitrary")),
        name="moe_gate_up",
    )(first_blk, n_blk, n_used, sorted_tok, src_base, hn, w_gate_up, w_gate_up, b_gate_up3, b_gate_up3)
    return hb


def _down_kernel(fb_ref, nbk_ref, nu_ref, hb_hbm, w_ref, b_ref, y_hbm,
                 wb_ref, xbuf, obuf, xsem, osem, *, tm, n_blocks):
    e = pl.program_id(0)
    wb_ref[...] = w_ref[0].astype(bf16)

    def compute(x):
        return jnp.dot(x, wb_ref[...], preferred_element_type=f32) + b_ref[0]

    o_copy = _expert_row_blocks(fb_ref[e], nbk_ref[e], hb_hbm, lambda r0: y_hbm.at[pl.ds(r0, tm)],
                                xbuf, obuf, xsem, osem, tm, compute)

    @pl.when(e == N_EXPERTS - 1)
    def _():
        _zero_unused_blocks(o_copy, obuf, nu_ref[0], n_blocks)


def _down(first_blk, n_blk, n_used, hb, w_down, b_down3, tm):
    n_rows = hb.shape[0]
    return pl.pallas_call(
        functools.partial(_down_kernel, tm=tm, n_blocks=n_rows // tm),
        grid_spec=pltpu.PrefetchScalarGridSpec(
            num_scalar_prefetch=3,
            grid=(N_EXPERTS,),
            in_specs=[
                pl.BlockSpec(memory_space=pl.ANY),
                pl.BlockSpec((1, D_FF, D_MODEL), lambda e, *_: (e, 0, 0)),
                pl.BlockSpec((1, 1, D_MODEL), lambda e, *_: (e, 0, 0)),
            ],
            out_specs=pl.BlockSpec(memory_space=pl.ANY),
            scratch_shapes=[
                pltpu.VMEM((D_FF, D_MODEL), bf16),
                pltpu.VMEM((2, tm, D_FF), bf16), pltpu.VMEM((2, tm, D_MODEL), f32),
                pltpu.SemaphoreType.DMA((2,)), pltpu.SemaphoreType.DMA((2,)),
            ],
        ),
        out_shape=jax.ShapeDtypeStruct((n_rows, D_MODEL), f32),
        compiler_params=_cparams(("arbitrary",)),
        name="moe_down",
    )(first_blk, n_blk, n_used, hb, w_down, b_down3)


def _combine_kernel(pos_ref, h_ref, gt_ref, y_ref, o_ref, buf_ref, sem, *, tc):
    i = pl.program_id(0)
    n = pl.num_programs(0)
    rows = TOP_K * tc

    def issue(step, slot):
        base = step * rows
        _start_row_gathers(lambda r: pos_ref[base + r], rows, y_ref, buf_ref.at[slot], sem.at[slot])

    @pl.when(i == 0)
    def _():
        issue(0, 0)

    @pl.when(i + 1 < n)
    def _():
        issue(i + 1, (i + 1) % 2)

    slot = i % 2
    _wait_row_gathers(rows, y_ref, buf_ref.at[slot], sem.at[slot])
    gt = gt_ref[...]
    out = h_ref[...]
    for k in range(TOP_K):
        out = out + gt[:, k:k + 1] * buf_ref[slot, k * tc:(k + 1) * tc, :]
    o_ref[...] = out


def _combine(pos_blocks, h, gate_pad, y, tc):
    T = h.shape[0]
    return pl.pallas_call(
        functools.partial(_combine_kernel, tc=tc),
        grid_spec=pltpu.PrefetchScalarGridSpec(
            num_scalar_prefetch=1,
            grid=(T // tc,),
            in_specs=[
                pl.BlockSpec((tc, D_MODEL), lambda i, pos: (i, 0)),
                pl.BlockSpec((tc, LANES), lambda i, pos: (i, 0)),
                pl.BlockSpec(memory_space=pl.ANY),
            ],
            out_specs=pl.BlockSpec((tc, D_MODEL), lambda i, pos: (i, 0)),
            scratch_shapes=[pltpu.VMEM((2, TOP_K * tc, D_MODEL), f32), pltpu.SemaphoreType.DMA((2,))],
        ),
        out_shape=jax.ShapeDtypeStruct((T, D_MODEL), f32),
        compiler_params=_cparams(("arbitrary",)),
        name="moe_combine",
    )(pos_blocks, h, gate_pad, y)


def _rearranged_w_in(w_in):
    splits = np.cumsum([0, 1024, 256, 256, 256, 256, 256, 256, 24, 1024, 1024, 1024])
    q, kc, vc, ks, vs, kw, vw, g, qd, kd, vd = [w_in[:, splits[n]:splits[n + 1]] for n in range(11)]
    gcols = NSA_GROUP * N_BRANCH
    zpad = lambda n: jnp.zeros((D_MODEL, n), w_in.dtype)
    w_norm = jnp.concatenate([q, ks, kw, qd, kd], axis=1)
    w_raw = jnp.concatenate([kc, vc, vs, vw, vd,
                             g[:, :gcols], zpad(HEAD_DIM - gcols), g[:, gcols:], zpad(HEAD_DIM - gcols),
                             zpad((N_RAW_HEADS - HR_GATE - 2) * HEAD_DIM)], axis=1)
    return w_norm.astype(bf16), w_raw.astype(bf16)


def _sel_map_t(S, nselp):
    n_chunks = S // CMP_STRIDE
    n_cmp = (S - CMP_LEN) // CMP_STRIDE + 1
    n_sel = S // SEL_BLOCK
    c_start = np.arange(n_chunks) * CMP_STRIDE
    s_start = np.arange(nselp) * SEL_BLOCK
    ov = (c_start[None, :] < s_start[:, None] + SEL_BLOCK) & (c_start[None, :] + CMP_LEN > s_start[:, None])
    ov &= (np.arange(n_chunks)[None, :] < n_cmp) & (np.arange(nselp)[:, None] < n_sel)
    return jnp.asarray(ov, bf16)


def _attention_block(x2, B, S, attn_norm_g, w_in, nsa_q_norm_g, nsa_k_norm_g, pos_k, pos_v,
                     k_w1, k_w2, v_w1, v_w2, diff_q_norm_g, diff_k_norm_g, lq1, lk1, lq2, lk2,
                     subln_g, lambda_init):
    T = B * S
    scale = HEAD_DIM ** -0.5 * math.log2(math.e)
    tiles = _tiles(S)
    tm = tiles["proj_rows"]
    rope_c, rope_sa, rope_sb = _rope_tables(np.arange(S))
    gains = jnp.concatenate([
        jnp.broadcast_to(nsa_q_norm_g * scale, (NSA_HEADS, HEAD_DIM)),
        jnp.broadcast_to(nsa_k_norm_g[1], (NSA_KV_HEADS, HEAD_DIM)),
        jnp.broadcast_to(nsa_k_norm_g[2], (NSA_KV_HEADS, HEAD_DIM)),
        jnp.broadcast_to(diff_q_norm_g * scale, (2 * DIFF_HEADS, HEAD_DIM)),
        jnp.broadcast_to(diff_k_norm_g, (2 * DIFF_HEADS, HEAD_DIM)),
    ], axis=0).reshape(N_NORM_HEADS, 1, HEAD_DIM)
    w_norm, w_raw = _rearranged_w_in(w_in)
    Pn, Pr, gate = _inproj(x2, attn_norm_g.reshape(1, D_MODEL), w_norm, w_raw, gains,
                           rope_c, rope_sa, rope_sb, S, tm)

    n_chunks = S // CMP_STRIDE
    cmp_pos = np.arange(n_chunks) * CMP_STRIDE + (CMP_LEN - 1)
    cc, csa, csb = _rope_tables(cmp_pos)
    chunk_view = lambda hd: Pr[hd:hd + NSA_KV_HEADS].reshape(NSA_KV_HEADS * B, n_chunks, CMP_STRIDE * HEAD_DIM)
    kcmp = _compress(chunk_view(HR_KC), pos_k, k_w1, k_w2, nsa_k_norm_g[0].reshape(1, HEAD_DIM),
                     cc, csa, csb, True)
    vcmp = _compress(chunk_view(HR_VC), pos_v, v_w1, v_w2, nsa_k_norm_g[0].reshape(1, HEAD_DIM),
                     cc, csa, csb, False)

    nselp = -(-(S // SEL_BLOCK) // LANES) * LANES
    tq = tiles["attn_q"]
    o_c, bias = _nsa_cmp(Pn, kcmp, vcmp, gate, _sel_map_t(S, nselp), B, S, tq)
    o_nsa = _nsa_sel(Pn, Pr, bias, gate, o_c, B, S, tq, tiles["sel_keys"], tq)

    lam_vecs = jnp.concatenate([lq1[None], lk1[None], lq2[None], lk2[None],
                                jnp.zeros((4, HEAD_DIM), f32)], axis=0)
    o_diff = _diff_attn(Pn, Pr, lam_vecs, subln_g.reshape(1, DIFF_V_DIM), B, S,
                        tiles["diff_q"], tiles["diff_q"], lambda_init)
    return o_nsa, o_diff


def _moe_block(h, hn, logits, w_gate_up, b_gate_up, w_down, b_down, tiles):
    T = h.shape[0]
    tm = tiles["moe_rows"]
    TK = T * TOP_K
    top_v, top_e = lax.top_k(logits[:, :N_EXPERTS], TOP_K)
    gate = jax.nn.softmax(top_v, axis=-1)
    flat_e = top_e.reshape(TK).astype(i32)
    se, order = lax.sort((flat_e, jnp.arange(TK, dtype=i32)), num_keys=1, is_stable=True)
    st = order // TOP_K
    experts = jnp.arange(N_EXPERTS, dtype=i32)
    counts = jnp.sum((flat_e[:, None] == experts[None, :]).astype(i32), axis=0)
    padded = (counts + tm - 1) // tm * tm
    pad_end = jnp.cumsum(padded)
    pad_start = pad_end - padded
    cnt_start = jnp.cumsum(counts) - counts
    dest = jnp.arange(TK, dtype=i32) + (pad_start - cnt_start)[se]
    _, pos = lax.sort((order, dest), num_keys=1)
    n_blocks = -(-TK // tm) + N_EXPERTS
    blk_start = jnp.arange(n_blocks, dtype=i32) * tm
    blk_e = jnp.minimum(jnp.sum((pad_end[None, :] <= blk_start[:, None]).astype(i32), axis=1), N_EXPERTS - 1)
    src_base = cnt_start[blk_e] + (blk_start - pad_start[blk_e])
    n_used = (pad_end[-1] // tm).astype(i32).reshape(1)

    src_base = jnp.minimum(jnp.concatenate([src_base, src_base[-1:]]), TK).astype(i32)
    st = jnp.concatenate([st, jnp.zeros((tm,), i32)])
    first_blk = (pad_start // tm).astype(i32)
    n_blk = (padded // tm).astype(i32)
    hb = _gateup(first_blk, n_blk, n_used, st, src_base, hn, w_gate_up,
                 b_gate_up.reshape(N_EXPERTS, 1, 2 * D_FF), n_blocks, tm, tiles["moe_cols"])
    y = _down(first_blk, n_blk, n_used, hb, w_down, b_down.reshape(N_EXPERTS, 1, D_MODEL), tm)

    tc = tiles["combine_rows"]
    pos_blocks = pos.reshape(T // tc, tc, TOP_K).transpose(0, 2, 1).reshape(TK)
    gate_pad = jnp.pad(gate, ((0, 0), (0, LANES - TOP_K)))
    return _combine(pos_blocks, h, gate_pad, y, tc)


def kernel(x, attn_norm_g, w_in, nsa_q_norm_g, nsa_k_norm_g, nsa_cmp_pos_k, nsa_cmp_pos_v, nsa_cmp_k_w1, nsa_cmp_k_w2, nsa_cmp_v_w1, nsa_cmp_v_w2, diff_q_norm_g, diff_k_norm_g, diff_lambda_q1, diff_lambda_k1, diff_lambda_q2, diff_lambda_k2, diff_subln_g, w_out, ffn_norm_g, w_router, b_router, w_gate_up, b_gate_up, w_down, b_down):
    B, S, _ = x.shape
    T = B * S
    depth = attn_norm_g.shape[0]
    h = x.reshape(T, D_MODEL)
    for l in range(depth):
        lambda_init = 0.8 - 0.6 * math.exp(-0.3 * l)
        o_nsa, o_diff = _attention_block(
            h, B, S, attn_norm_g[l], w_in[l], nsa_q_norm_g[l], nsa_k_norm_g[l], nsa_cmp_pos_k[l],
            nsa_cmp_pos_v[l], nsa_cmp_k_w1[l], nsa_cmp_k_w2[l], nsa_cmp_v_w1[l], nsa_cmp_v_w2[l],
            diff_q_norm_g[l], diff_k_norm_g[l], diff_lambda_q1[l], diff_lambda_k1[l],
            diff_lambda_q2[l], diff_lambda_k2[l], diff_subln_g[l], lambda_init)
        w_router_f = jnp.pad(w_router[l], ((0, 0), (0, LANES - N_EXPERTS)))
        w_router_hi = w_router_f.astype(bf16)
        w_router_lo = (w_router_f - w_router_hi.astype(f32)).astype(bf16)
        w_router_p = jnp.concatenate([w_router_hi, w_router_lo], axis=1)
        b_router_p = jnp.pad(b_router[l], (0, LANES - N_EXPERTS)).reshape(1, LANES)
        h_mid, hn, logits = _outproj(h, o_nsa, o_diff, w_out[l].astype(bf16),
                                     ffn_norm_g[l].reshape(1, D_MODEL), w_router_p, b_router_p,
                                     _tiles(S)["out_rows"])
        h = _moe_block(h_mid, hn, logits, w_gate_up[l], b_gate_up[l], w_down[l], b_down[l], _tiles(S))
    return h.reshape(B, S, D_MODEL)
```
